```python
import jax, jax.numpy as jnp
from jax import lax
import numpy as np

D_MODEL = 1024
BATCH = 16
SEQ = 2048
DEPTH = 1

MEM_LEN = 256
HEAD_DIM = 64
FOX_HEADS = D_MODEL // 128
FOX_W = FOX_HEADS * HEAD_DIM
RWKV_HEADS = D_MODEL // 128
RWKV_W = RWKV_HEADS * HEAD_DIM
MEM_HEADS = 4
MEM_W = D_MODEL // 2
MEM_HEAD_DIM = MEM_W // MEM_HEADS
DECAY_LORA = 64
AAA_LORA = 64
GATE_LORA = 128
N_BRANCH = 3
D_FF = -(-8 * D_MODEL // (3 * 256)) * 256
Q_BLOCK = 128
NORM_EPS = 1e-6
GN_EPS = 64e-5

FOX_COLS = 3 * FOX_W + FOX_HEADS
RWKV_WIDTHS = (RWKV_W, RWKV_W, RWKV_W, DECAY_LORA, AAA_LORA, GATE_LORA)
RWKV_COLS = sum(RWKV_WIDTHS)
GATE_COLS = N_BRANCH * D_MODEL
IN_COLS = FOX_COLS + RWKV_COLS + MEM_W + GATE_COLS

kernel_name = "fox_rwkv7_memxattn_gated_hybrid"


def _split(x, widths):
    idx = [int(i) for i in np.cumsum(widths)[:-1]]
    return jnp.split(x, idx, axis=-1)


def rmsnorm(x, g):
    xf = x.astype(jnp.float32)
    y = xf * lax.rsqrt(jnp.mean(xf * xf, axis=-1, keepdims=True) + NORM_EPS)
    return (y * g.astype(jnp.float32)).astype(x.dtype)


def forgetting_attention(q, k, v, f_logit):
    B, S, H, Dh = q.shape
    c = jnp.cumsum(jax.nn.log_sigmoid(f_logit.astype(jnp.float32)), axis=1)
    cT = jnp.transpose(c, (0, 2, 1))
    scale = Dh ** -0.5
    tri = jnp.tril(jnp.ones((Q_BLOCK, Q_BLOCK), dtype=bool))
    outs = []
    for i in range(S // Q_BLOCK):
        lo, hi = i * Q_BLOCK, (i + 1) * Q_BLOCK
        logits = jnp.einsum('bqhd,bkhd->bhqk', q[:, lo:hi], k[:, :hi]).astype(jnp.float32) * scale
        bias = cT[:, :, lo:hi, None] - cT[:, :, None, :hi]
        mask = jnp.concatenate([jnp.ones((Q_BLOCK, lo), dtype=bool), tri], axis=1)
        logits = jnp.where(mask, logits + bias, -jnp.inf)
        p = jax.nn.softmax(logits, axis=-1).astype(v.dtype)
        outs.append(jnp.einsum('bhqk,bkhd->bqhd', p, v[:, :hi]))
    return jnp.concatenate(outs, axis=1).reshape(B, S, H * Dh)


def rwkv7_time_mix(p, mu, w0, w_up, a0, a_up, g_up, k_k, k_a, r_k, gn_g, gn_b):
    B, S, _ = p.shape
    H, N = RWKV_HEADS, HEAD_DIM
    p = p.astype(jnp.float32)
    p_prev = jnp.pad(p, ((0, 0), (1, 0), (0, 0)))[:, :-1]
    p = p + (p_prev - p) * mu
    r, k, v, wd, ad, gd = _split(p, RWKV_WIDTHS)
    w_log = -jnp.exp(jax.nn.log_sigmoid(w0 + jnp.tanh(wd) @ w_up) - 0.5)
    a = jax.nn.sigmoid(a0 + ad @ a_up)
    g = jax.nn.sigmoid(gd) @ g_up
    kk = (k * k_k).reshape(B, S, H, N)
    kk = kk * lax.rsqrt(jnp.maximum(jnp.sum(kk * kk, axis=-1, keepdims=True), 1e-24))
    k = k * (1.0 + (a - 1.0) * k_a)
    rh = r.reshape(B, S, H, N)
    kh = k.reshape(B, S, H, N)
    vh = v.reshape(B, S, H, N)
    ah = a.reshape(B, S, H, N)
    wh = jnp.exp(w_log).reshape(B, S, H, N)
    a_vec = -kk
    b_vec = kk * ah

    def step(state, inp):
        r_t, w_t, k_t, v_t, a_t, b_t = inp
        sa = jnp.einsum('bhij,bhj->bhi', state, a_t)
        state = state * w_t[:, :, None, :] + sa[..., None] * b_t[:, :, None, :] + v_t[..., None] * k_t[:, :, None, :]
        y = jnp.einsum('bhij,bhj->bhi', state, r_t)
        return state, y

    xs = tuple(jnp.moveaxis(t, 1, 0) for t in (rh, wh, kh, vh, a_vec, b_vec))
    s0 = jnp.zeros((B, H, N, N), jnp.float32)
    _, ys = lax.scan(step, s0, xs)
    y = jnp.moveaxis(ys, 0, 1)
    mean = jnp.mean(y, axis=-1, keepdims=True)
    var = jnp.mean(jnp.square(y - mean), axis=-1, keepdims=True)
    y = ((y - mean) * lax.rsqrt(var + GN_EPS)).reshape(B, S, H * N) * gn_g + gn_b
    bonus = jnp.sum(rh * kh * r_k, axis=-1, keepdims=True) * vh
    return (y + bonus.reshape(B, S, H * N)) * g


def memory_cross_attention(q, mem_kv):
    B, S, _ = q.shape
    km, vm = _split(mem_kv, (MEM_W, MEM_W))
    qh = q.reshape(B, S, MEM_HEADS, MEM_HEAD_DIM)
    kh = km.reshape(B, -1, MEM_HEADS, MEM_HEAD_DIM)
    vh = vm.reshape(B, -1, MEM_HEADS, MEM_HEAD_DIM)
    logits = jnp.einsum('bqhd,bkhd->bhqk', qh, kh).astype(jnp.float32) * MEM_HEAD_DIM ** -0.5
    p = jax.nn.softmax(logits, axis=-1).astype(vh.dtype)
    return jnp.einsum('bhqk,bkhd->bqhd', p, vh).reshape(B, S, MEM_W)


def setup_inputs(seed: int = 0) -> dict:
    key = jax.random.key(seed)
    ks = jax.random.split(key, 32)
    f32 = jnp.float32
    L, D = DEPTH, D_MODEL

    def nrm(k, shape, fan_in):
        return jax.random.normal(k, shape, f32) * fan_in ** -0.5

    def gain(k, shape):
        return 1.0 + 0.1 * jax.random.normal(k, shape, f32)

    return {
        "x": jax.random.normal(ks[0], (BATCH, SEQ, D), f32),
        "mem": jax.random.normal(ks[1], (BATCH, MEM_LEN, D), f32),
        "pre1_g": gain(ks[2], (L, D)),
        "post1_g": gain(ks[3], (L, D)),
        "pre2_g": gain(ks[4], (L, D)),
        "post2_g": gain(ks[5], (L, D)),
        "mem_norm_g": gain(ks[6], (L, D)),
        "w_in": nrm(ks[7], (L, D, IN_COLS), D),
        "fox_f_bias": jax.random.uniform(ks[8], (L, FOX_HEADS), f32, 1.0, 4.0),
        "rwkv_mu": jax.random.uniform(ks[9], (L, RWKV_COLS), f32),
        "rwkv_w0": jax.random.normal(ks[10], (L, RWKV_W), f32),
        "rwkv_w_up": nrm(ks[11], (L, DECAY_LORA, RWKV_W), DECAY_LORA),
        "rwkv_a0": 0.5 * jax.random.normal(ks[12], (L, RWKV_W), f32),
        "rwkv_a_up": nrm(ks[13], (L, AAA_LORA, RWKV_W), AAA_LORA),
        "rwkv_g_up": nrm(ks[14], (L, GATE_LORA, RWKV_W), GATE_LORA),
        "rwkv_k_k": 0.85 + 0.05 * jax.random.normal(ks[15], (L, RWKV_W), f32),
        "rwkv_k_a": 1.0 + 0.05 * jax.random.normal(ks[16], (L, RWKV_W), f32),
        "rwkv_r_k": 0.1 * jax.random.normal(ks[17], (L, RWKV_HEADS, HEAD_DIM), f32),
        "rwkv_gn_g": gain(ks[18], (L, RWKV_W)),
        "rwkv_gn_b": 0.02 * jax.random.normal(ks[19], (L, RWKV_W), f32),
        "w_mem_kv": nrm(ks[20], (L, D, 2 * MEM_W), D),
        "w_fox_out": nrm(ks[21], (L, FOX_W, D), FOX_W),
        "w_rwkv_out": nrm(ks[22], (L, RWKV_W, D), RWKV_W),
        "w_mem_out": nrm(ks[23], (L, MEM_W, D), MEM_W),
        "w_o": nrm(ks[24], (L, D, D), D),
        "w_ffn_gate": nrm(ks[25], (L, D, D_FF), D),
        "w_ffn_up": nrm(ks[26], (L, D, D_FF), D),
        "w_ffn_down": nrm(ks[27], (L, D_FF, D), D_FF),
    }


def reference(x, mem, pre1_g, post1_g, pre2_g, post2_g, mem_norm_g, w_in, fox_f_bias,
              rwkv_mu, rwkv_w0, rwkv_w_up, rwkv_a0, rwkv_a_up, rwkv_g_up, rwkv_k_k, rwkv_k_a,
              rwkv_r_k, rwkv_gn_g, rwkv_gn_b, w_mem_kv, w_fox_out, w_rwkv_out, w_mem_out, w_o,
              w_ffn_gate, w_ffn_up, w_ffn_down):
    B, S, D = x.shape
    h = x
    for l in range(DEPTH):
        u = rmsnorm(h, pre1_g[l])
        proj = u @ w_in[l]
        p_fox, p_rwkv, p_memq, p_gate = _split(proj, (FOX_COLS, RWKV_COLS, MEM_W, GATE_COLS))

        fq, fk, fv, ff = _split(p_fox, (FOX_W, FOX_W, FOX_W, FOX_HEADS))
        fox_out = forgetting_attention(
            fq.reshape(B, S, FOX_HEADS, HEAD_DIM), fk.reshape(B, S, FOX_HEADS, HEAD_DIM),
            fv.reshape(B, S, FOX_HEADS, HEAD_DIM), ff + fox_f_bias[l])

        rwkv_out = rwkv7_time_mix(p_rwkv, rwkv_mu[l], rwkv_w0[l], rwkv_w_up[l], rwkv_a0[l],
                                  rwkv_a_up[l], rwkv_g_up[l], rwkv_k_k[l], rwkv_k_a[l],
                                  rwkv_r_k[l], rwkv_gn_g[l], rwkv_gn_b[l])

        mem_kv = rmsnorm(mem, mem_norm_g[l]) @ w_mem_kv[l]
        mem_out = memory_cross_attention(p_memq, mem_kv)

        g_fox, g_rwkv, g_mem = _split(jax.nn.sigmoid(p_gate.astype(jnp.float32)), (D, D, D))
        merged = (g_fox * (fox_out @ w_fox_out[l])
                  + g_rwkv * (rwkv_out @ w_rwkv_out[l])
                  + g_mem * (mem_out @ w_mem_out[l]))
        y = merged @ w_o[l]
        h = h + rmsnorm(y, post1_g[l])

        u2 = rmsnorm(h, pre2_g[l])
        ffn = (jax.nn.silu(u2 @ w_ffn_gate[l]) * (u2 @ w_ffn_up[l])) @ w_ffn_down[l]
        h = h + rmsnorm(ffn, post2_g[l])
    return h.astype(x.dtype)
```

```python
import functools

import jax
import jax.numpy as jnp
from jax import lax
from jax.experimental import pallas as pl
from jax.experimental.pallas import tpu as pltpu

F32 = jnp.float32
BF16 = jnp.bfloat16

NORM_EPS = 1e-6
GN_EPS = 64e-5
HEAD_DIM = 64
LANES = 128
MEM_HEADS = 4
CHUNK = 64
PAIR = 2 * CHUNK
NEG_BIG = -1e30
VMEM_LIMIT = 56 * 1024 * 1024


def _dot(a, b):
    return jnp.dot(a, b, preferred_element_type=F32)


def _dot_nt(a, b):
    return lax.dot_general(a, b, (((1,), (1,)), ((), ())), preferred_element_type=F32)


def _dot_tn(a, b):
    return lax.dot_general(a, b, (((0,), (0,)), ((), ())), preferred_element_type=F32)


def _split3(x):
    hi = x.astype(BF16)
    r1 = x - hi.astype(F32)
    mid = r1.astype(BF16)
    lo = (r1 - mid.astype(F32)).astype(BF16)
    return hi, mid, lo


def _dot_exact_lhs(m_bf16, x):
    hi, mid, lo = _split3(x)
    return _dot(m_bf16, hi) + _dot(m_bf16, mid) + _dot(m_bf16, lo)


def _dot_exact_rhs(x, m_bf16):
    hi, mid, lo = _split3(x)
    return _dot(hi, m_bf16) + _dot(mid, m_bf16) + _dot(lo, m_bf16)


def _rms(xf, g):
    return xf * lax.rsqrt(jnp.mean(xf * xf, axis=-1, keepdims=True) + NORM_EPS) * g


def _sigmoid(x):
    return 1.0 / (1.0 + jnp.exp(-x))


def _log_sigmoid(x):
    return jnp.minimum(x, 0.0) - jnp.log(1.0 + jnp.exp(-jnp.abs(x)))


def _const_spec(shape):
    nd = len(shape)
    return pl.BlockSpec(shape, lambda *_: (0,) * nd, pipeline_mode=pl.Buffered(1))


def _params(n_axes):
    return pltpu.CompilerParams(dimension_semantics=("arbitrary",) * n_axes,
                                vmem_limit_bytes=VMEM_LIMIT)


def _inproj_kernel(x_ref, g_ref, wqkv_ref, wff_ref, wrw_ref, wmq_ref, wgt_ref, fb_ref,
                   qkv_ref, ccol_ref, crow_ref, prw_ref, mq_ref, gate_ref, carry_ref,
                   *, tiles_per_seq, mem_scale):
    i = pl.program_id(0)
    tm = x_ref.shape[0]

    @pl.when(i % tiles_per_seq == 0)
    def _():
        carry_ref[...] = jnp.zeros_like(carry_ref)

    u = _rms(x_ref[...], g_ref[...]).astype(BF16)
    qkv_ref[...] = _dot(u, wqkv_ref[...]).astype(BF16)
    prw_ref[...] = _dot(u, wrw_ref[...])
    mq_ref[...] = (_dot(u, wmq_ref[...]) * mem_scale).astype(BF16)
    d = gate_ref.shape[1] // 3
    for j in range(3):
        gate_ref[:, j * d:(j + 1) * d] = _sigmoid(_dot(u, wgt_ref[:, j * d:(j + 1) * d])).astype(BF16)

    ls = _log_sigmoid(_dot(u, wff_ref[...]) + fb_ref[...])
    row = lax.broadcasted_iota(jnp.int32, (tm, tm), 0)
    col = lax.broadcasted_iota(jnp.int32, (tm, tm), 1)
    tri = jnp.where(row >= col, 1.0, 0.0).astype(BF16)
    c = _dot_exact_lhs(tri, ls) + carry_ref[...]
    carry_ref[...] = c[tm - 1:tm, :]
    ccol_ref[...] = c
    crow_ref[...] = jnp.transpose(c)[:crow_ref.shape[0], :]


def _inproj(x2, g, wqkv, wff, wrw, wmq, wgt, fbias, *, seq, mem_scale, tm=512):
    t, d = x2.shape
    n_heads_pad = 8
    outs = (
        jax.ShapeDtypeStruct((t, wqkv.shape[1]), BF16),
        jax.ShapeDtypeStruct((t, LANES), F32),
        jax.ShapeDtypeStruct((n_heads_pad, t), F32),
        jax.ShapeDtypeStruct((t, wrw.shape[1]), F32),
        jax.ShapeDtypeStruct((t, wmq.shape[1]), BF16),
        jax.ShapeDtypeStruct((t, wgt.shape[1]), BF16),
    )
    row_spec = lambda w: pl.BlockSpec((tm, w), lambda i: (i, 0))
    return pl.pallas_call(
        functools.partial(_inproj_kernel, tiles_per_seq=seq // tm, mem_scale=mem_scale),
        grid=(t // tm,),
        in_specs=[row_spec(d), _const_spec(g.shape), _const_spec(wqkv.shape), _const_spec(wff.shape),
                  _const_spec(wrw.shape), _const_spec(wmq.shape), _const_spec(wgt.shape),
                  _const_spec(fbias.shape)],
        out_specs=[row_spec(wqkv.shape[1]), row_spec(LANES),
                   pl.BlockSpec((n_heads_pad, tm), lambda i: (0, i)),
                   row_spec(wrw.shape[1]), row_spec(wmq.shape[1]), row_spec(wgt.shape[1])],
        out_shape=outs,
        scratch_shapes=[pltpu.VMEM((1, LANES), F32)],
        compiler_params=_params(1),
        name="inproj",
    )(x2, g, wqkv, wff, wrw, wmq, wgt, fbias)


def _fox_kernel(q_ref, k_ref, v_ref, ccol_ref, crow_ref, o_ref, *, bk):
    hp = pl.program_id(1)
    qi = pl.program_id(2)
    tq = q_ref.shape[0]
    lane = lax.broadcasted_iota(jnp.int32, (tq, LANES), 1)
    first = lane < HEAD_DIM
    q2 = q_ref[...]
    zero = jnp.zeros_like(q2)
    qs = (jnp.where(first, q2, zero), jnp.where(first, zero, q2))
    ccol_tile = ccol_ref[...]
    ccols = tuple(jnp.sum(jnp.where(lane == 2 * hp + e, ccol_tile, 0.0), axis=-1, keepdims=True)
                  for e in range(2))
    sub = lax.broadcasted_iota(jnp.int32, (crow_ref.shape[0], bk), 0)
    rowi = lax.broadcasted_iota(jnp.int32, (tq, bk), 0) + qi * tq
    coli = lax.broadcasted_iota(jnp.int32, (tq, bk), 1)

    def body(j, carry):
        off = pl.multiple_of(j * bk, bk)
        kt = k_ref[pl.ds(off, bk), :]
        vt = v_ref[pl.ds(off, bk), :]
        crow_tile = crow_ref[:, pl.ds(off, bk)]
        causal = rowi >= coli + off
        new = []
        for e in range(2):
            m, l, acc = carry[e]
            crow = jnp.sum(jnp.where(sub == 2 * hp + e, crow_tile, 0.0), axis=0, keepdims=True)
            s = _dot_nt(qs[e], kt) + (ccols[e] - crow)
            s = jnp.where(causal, s, NEG_BIG)
            m_new = jnp.maximum(m, jnp.max(s, axis=-1, keepdims=True))
            alpha = jnp.exp(m - m_new)
            p = jnp.exp(s - m_new)
            l = alpha * l + jnp.sum(p, axis=-1, keepdims=True)
            acc = alpha * acc + _dot(p.astype(BF16), vt)
            new.append((m_new, l, acc))
        return tuple(new)

    init = tuple((jnp.full((tq, 1), NEG_BIG, F32), jnp.zeros((tq, 1), F32), jnp.zeros((tq, LANES), F32))
                 for _ in range(2))
    n_k = (qi * tq + tq + bk - 1) // bk
    (m0, l0, a0), (m1, l1, a1) = lax.fori_loop(0, n_k, body, init)
    o_ref[...] = jnp.where(first, a0 / l0, a1 / l1).astype(o_ref.dtype)


def _fox(qkv, ccol, crow, *, batch, seq, tq=256, bk=256):
    t = qkv.shape[0]
    n_pairs = qkv.shape[1] // (3 * LANES)
    nq = seq // tq
    return pl.pallas_call(
        functools.partial(_fox_kernel, bk=bk),
        grid=(batch, n_pairs, nq),
        in_specs=[
            pl.BlockSpec((tq, LANES), lambda b, h, i: (b * nq + i, h)),
            pl.BlockSpec((seq, LANES), lambda b, h, i: (b, n_pairs + h)),
            pl.BlockSpec((seq, LANES), lambda b, h, i: (b, 2 * n_pairs + h)),
            pl.BlockSpec((tq, LANES), lambda b, h, i: (b * nq + i, 0)),
            pl.BlockSpec((crow.shape[0], seq), lambda b, h, i: (0, b)),
        ],
        out_specs=pl.BlockSpec((tq, LANES), lambda b, h, i: (b * nq + i, h)),
        out_shape=jax.ShapeDtypeStruct((t, n_pairs * LANES), BF16),
        compiler_params=_params(3),
        name="fox_attention",
    )(qkv, qkv, qkv, ccol, crow)


def _memkv_kernel(m_ref, g_ref, w_ref, o_ref):
    o_ref[...] = _dot(_rms(m_ref[...], g_ref[...]).astype(BF16), w_ref[...]).astype(BF16)


def _memkv(mem2, g, w, tm=512):
    t, d = mem2.shape
    return pl.pallas_call(
        _memkv_kernel,
        grid=(t // tm,),
        in_specs=[pl.BlockSpec((tm, d), lambda i: (i, 0)), _const_spec(g.shape), _const_spec(w.shape)],
        out_specs=pl.BlockSpec((tm, w.shape[1]), lambda i: (i, 0)),
        out_shape=jax.ShapeDtypeStruct((t, w.shape[1]), BF16),
        compiler_params=_params(1),
        name="mem_kv",
    )(mem2, g, w)


def _memattn_kernel(q_ref, kv_ref, o_ref):
    w = q_ref.shape[1]
    hd = w // MEM_HEADS
    for h in range(MEM_HEADS):
        q = q_ref[:, h * hd:(h + 1) * hd]
        k = kv_ref[:, h * hd:(h + 1) * hd]
        v = kv_ref[:, w + h * hd:w + (h + 1) * hd]
        s = _dot_nt(q, k)
        p = jnp.exp(s - jnp.max(s, axis=-1, keepdims=True))
        l = jnp.sum(p, axis=-1, keepdims=True)
        o_ref[:, h * hd:(h + 1) * hd] = (_dot(p.astype(BF16), v) / l).astype(o_ref.dtype)


def _memattn(mq, memkv, *, batch, seq, mem_len, tq=512):
    t, w = mq.shape
    nq = seq // tq
    return pl.pallas_call(
        _memattn_kernel,
        grid=(batch, nq),
        in_specs=[pl.BlockSpec((tq, w), lambda b, i: (b * nq + i, 0)),
                  pl.BlockSpec((mem_len, 2 * w), lambda b, i: (b, 0))],
        out_specs=pl.BlockSpec((tq, w), lambda b, i: (b * nq + i, 0)),
        out_shape=jax.ShapeDtypeStruct((t, w), BF16),
        compiler_params=_params(2),
        name="mem_attention",
    )(mq, memkv)


def _rwkv_prep_kernel(p_ref, pprev_ref, mu_ref, w0_ref, a0_ref, kk_ref, ka_ref, rk_ref,
                      wlora_ref, gup_ref, bd_ref,
                      at_ref, rt_ref, bt_ref, kt_ref, bh_ref, kh_ref, v_ref, gam_ref, bonus_ref, g_ref,
                      *, tiles_per_seq):
    i = pl.program_id(0)
    tm = p_ref.shape[0]
    w = v_ref.shape[1]
    p = p_ref[...]
    prev_rows = pprev_ref[...]
    prev_last = jnp.where(i % tiles_per_seq == 0, 0.0, prev_rows[prev_rows.shape[0] - 1:, :])
    rowid = lax.broadcasted_iota(jnp.int32, p.shape, 0)
    shifted = jnp.where(rowid == 0, prev_last, pltpu.roll(p, 1, axis=0))
    xs = p + (shifted - p) * mu_ref[...]
    r = xs[:, 0:w]
    k = xs[:, w:2 * w]
    v = xs[:, 2 * w:3 * w]
    lora_in = xs[:, 3 * w:3 * w + LANES]
    gd = xs[:, 3 * w + LANES:]
    lane = lax.broadcasted_iota(jnp.int32, lora_in.shape, 1)
    lora_in = jnp.where(lane < lora_in.shape[1] // 2, jnp.tanh(lora_in), lora_in)
    lora = _dot(lora_in.astype(BF16), wlora_ref[...])
    w_log = -jnp.exp(_log_sigmoid(w0_ref[...] + lora[:, :w]) - 0.5)
    a = _sigmoid(a0_ref[...] + lora[:, w:])
    g_ref[...] = _dot(_sigmoid(gd).astype(BF16), gup_ref[...])

    bd = bd_ref[...]
    kk = k * kk_ref[...]
    kk = kk * lax.rsqrt(jnp.maximum(_dot_exact_rhs(kk * kk, bd), 1e-24))
    k2 = k * (1.0 + (a - 1.0) * ka_ref[...])
    bonus_ref[...] = _dot_exact_rhs(r * k2 * rk_ref[...], bd) * v

    ti = lax.broadcasted_iota(jnp.int32, (tm, tm), 0)
    si = lax.broadcasted_iota(jnp.int32, (tm, tm), 1)
    same = (ti // CHUNK) == (si // CHUNK)
    lower = jnp.where(same & (si <= ti), 1.0, 0.0).astype(BF16)
    upper = jnp.where(same & (si > ti), 1.0, 0.0).astype(BF16)
    hi, mid, lo = _split3(w_log)
    cum = _dot(lower, hi) + _dot(lower, mid) + _dot(lower, lo)
    rem = _dot(upper, hi) + _dot(upper, mid) + _dot(upper, lo)
    n_sel = gam_ref.shape[1]
    ci = lax.broadcasted_iota(jnp.int32, (n_sel, tm), 0)
    sj = lax.broadcasted_iota(jnp.int32, (n_sel, tm), 1)
    sel = jnp.where(ci == sj // CHUNK, 1.0, 0.0).astype(BF16)
    gam_ref[0] = jnp.exp(_dot(sel, hi) + _dot(sel, mid) + _dot(sel, lo))

    e_prev = jnp.exp(cum - w_log)
    e_cum = jnp.exp(cum)
    e_neg = jnp.exp(-cum)
    e_rem = jnp.exp(rem)
    b = kk * a
    at_ref[...] = (-kk * e_prev).astype(BF16)
    rt_ref[...] = (r * e_cum).astype(BF16)
    bt_ref[...] = (b * e_neg).astype(BF16)
    kt_ref[...] = (k2 * e_neg).astype(BF16)
    bh_ref[...] = (b * e_rem).astype(BF16)
    kh_ref[...] = (k2 * e_rem).astype(BF16)
    v_ref[...] = v.astype(BF16)


def _rwkv_prep(prw, mu, w0, a0, k_k, k_a, r_k, wlora, gup, bd, *, seq, tm=256):
    t, cols = prw.shape
    w = w0.shape[1]
    n_tiles = t // tm
    n_sel = 8
    row = lambda width: pl.BlockSpec((tm, width), lambda i: (i, 0))
    tok_bf = jax.ShapeDtypeStruct((t, w), BF16)
    tok_f = jax.ShapeDtypeStruct((t, w), F32)
    prev_rows = 8
    return pl.pallas_call(
        functools.partial(_rwkv_prep_kernel, tiles_per_seq=seq // tm),
        grid=(n_tiles,),
        in_specs=[row(cols),
                  pl.BlockSpec((prev_rows, cols), lambda i: (jnp.maximum(i * (tm // prev_rows) - 1, 0), 0)),
                  _const_spec(mu.shape), _const_spec(w0.shape), _const_spec(a0.shape),
                  _const_spec(k_k.shape), _const_spec(k_a.shape), _const_spec(r_k.shape),
                  _const_spec(wlora.shape), _const_spec(gup.shape), _const_spec(bd.shape)],
        out_specs=[row(w)] * 7 + [pl.BlockSpec((1, n_sel, w), lambda i: (i, 0, 0)), row(w), row(w)],
        out_shape=[tok_bf] * 7 + [jax.ShapeDtypeStruct((n_tiles, n_sel, w), F32), tok_f, tok_f],
        compiler_params=_params(1),
        name="rwkv_prep",
    )(prw, prw, mu, w0, a0, k_k, k_a, r_k, wlora, gup, bd)


def _rwkv_scan_kernel(at_ref, rt_ref, bt_ref, kt_ref, bh_ref, kh_ref, v_ref, gam_ref, bonus_ref, g_ref,
                      gng_ref, gnb_ref, bdavg_ref, o_ref, state_ref, y_ref):
    @pl.when(pl.program_id(1) == 0)
    def _():
        state_ref[...] = jnp.zeros_like(state_ref)

    n_t, w = v_ref.shape
    n_pairs = w // LANES
    n_chunks = n_t // CHUNK
    lane = lax.broadcasted_iota(jnp.int32, (CHUNK, LANES), 1)
    first = lane < HEAD_DIM
    ri = lax.broadcasted_iota(jnp.int32, (PAIR, PAIR), 0)
    ci = lax.broadcasted_iota(jnp.int32, (PAIR, PAIR), 1)
    strict = ri > ci
    incl = ri >= ci
    eye = jnp.where(ri == ci, 1.0, 0.0)

    def stacked(ref, c, p):
        x = ref[c * CHUNK:(c + 1) * CHUNK, p * LANES:(p + 1) * LANES]
        z = jnp.zeros_like(x)
        return jnp.concatenate([jnp.where(first, x, z), jnp.where(first, z, x)], axis=0)

    states = [state_ref[p] for p in range(n_pairs)]
    for c in range(n_chunks):
        for p in range(n_pairs):
            a_s, r_s, b_s, k_s = (stacked(ref, c, p) for ref in (at_ref, rt_ref, bt_ref, kt_ref))
            bh_s, kh_s, v_s = (stacked(ref, c, p) for ref in (bh_ref, kh_ref, v_ref))
            g_all = _dot_nt(jnp.concatenate([a_s, r_s], axis=0), jnp.concatenate([b_s, k_s], axis=0))
            n_ab = jnp.where(strict, g_all[:PAIR, :PAIR], 0.0)
            a_ak = jnp.where(strict, g_all[:PAIR, PAIR:], 0.0).astype(BF16)
            a_rb = jnp.where(incl, g_all[PAIR:, :PAIR], 0.0).astype(BF16)
            a_rk = jnp.where(incl, g_all[PAIR:, PAIR:], 0.0).astype(BF16)
            s_inv = eye + n_ab
            q_pow = n_ab.astype(BF16)
            span = 1
            while 2 * span < CHUNK:
                q_pow = _dot(q_pow, q_pow).astype(BF16)
                s_inv = s_inv + _dot(q_pow, s_inv.astype(BF16))
                span *= 2
            h_t = states[p]
            h_bf = h_t.astype(BF16)
            w_rhs = _dot_nt(a_s, h_bf) + _dot(a_ak, v_s)
            u = _dot(s_inv.astype(BF16), w_rhs.astype(BF16)).astype(BF16)
            y_s = _dot_nt(r_s, h_bf) + _dot(a_rb, u) + _dot(a_rk, v_s)
            gam = gam_ref[0, c:c + 1, p * LANES:(p + 1) * LANES]
            states[p] = h_t * gam + _dot_tn(u, bh_s) + _dot_tn(v_s, kh_s)
            y_ref[c * CHUNK:(c + 1) * CHUNK, p * LANES:(p + 1) * LANES] = y_s[:CHUNK] + y_s[CHUNK:]
    for p in range(n_pairs):
        state_ref[p] = states[p]

    y = y_ref[...]
    bdavg = bdavg_ref[...]
    dev = y - _dot_exact_rhs(y, bdavg)
    var = _dot_exact_rhs(dev * dev, bdavg)
    yn = dev * lax.rsqrt(var + GN_EPS) * gng_ref[...] + gnb_ref[...]
    o_ref[...] = ((yn + bonus_ref[...]) * g_ref[...]).astype(o_ref.dtype)


def _rwkv_scan(at, rt, bt, kt, bh, kh, v, gam, bonus, g, gn_g, gn_b, bdavg, *, batch, seq):
    t, w = v.shape
    n_sel = gam.shape[1]
    tl = t // gam.shape[0]
    nt = seq // tl
    tok = pl.BlockSpec((tl, w), lambda b, i: (b * nt + i, 0))
    return pl.pallas_call(
        _rwkv_scan_kernel,
        grid=(batch, nt),
        in_specs=[tok] * 7 + [pl.BlockSpec((1, n_sel, w), lambda b, i: (b * nt + i, 0, 0)), tok, tok,
                              _const_spec(gn_g.shape), _const_spec(gn_b.shape), _const_spec(bdavg.shape)],
        out_specs=tok,
        out_shape=jax.ShapeDtypeStruct((t, w), BF16),
        scratch_shapes=[pltpu.VMEM((w // LANES, LANES, LANES), F32), pltpu.VMEM((tl, w), F32)],
        compiler_params=_params(2),
        name="rwkv_scan",
    )(at, rt, bt, kt, bh, kh, v, gam, bonus, g, gn_g, gn_b, bdavg)


def _merge_kernel(x_ref, fox_ref, rw_ref, mem_ref, gate_ref, wf_ref, wr_ref, wm_ref, wo_ref, g_ref, o_ref):
    d = x_ref.shape[1]
    merged = (gate_ref[:, 0:d].astype(F32) * _dot(fox_ref[...], wf_ref[...])
              + gate_ref[:, d:2 * d].astype(F32) * _dot(rw_ref[...], wr_ref[...])
              + gate_ref[:, 2 * d:3 * d].astype(F32) * _dot(mem_ref[...], wm_ref[...]))
    y = _dot(merged.astype(BF16), wo_ref[...])
    o_ref[...] = x_ref[...] + _rms(y, g_ref[...])


def _merge(x2, fox, rw, mem, gates, wf, wr, wm, wo, g, tm=512):
    t, d = x2.shape
    row = lambda width: pl.BlockSpec((tm, width), lambda i: (i, 0))
    return pl.pallas_call(
        _merge_kernel,
        grid=(t // tm,),
        in_specs=[row(d), row(fox.shape[1]), row(rw.shape[1]), row(mem.shape[1]), row(gates.shape[1]),
                  _const_spec(wf.shape), _const_spec(wr.shape), _const_spec(wm.shape), _const_spec(wo.shape),
                  _const_spec(g.shape)],
        out_specs=row(d),
        out_shape=jax.ShapeDtypeStruct((t, d), F32),
        compiler_params=_params(1),
        name="merge_out",
    )(x2, fox, rw, mem, gates, wf, wr, wm, wo, g)


def _ffn_kernel(h_ref, g1_ref, wg_ref, wu_ref, wd_ref, g2_ref, o_ref, *, ff_chunk):
    h = h_ref[...]
    u = _rms(h, g1_ref[...]).astype(BF16)
    d_ff = wg_ref.shape[1]
    acc = jnp.zeros(h.shape, F32)
    for lo in range(0, d_ff, ff_chunk):
        hi = min(lo + ff_chunk, d_ff)
        gt = _dot(u, wg_ref[:, lo:hi])
        up = _dot(u, wu_ref[:, lo:hi])
        act = (gt * _sigmoid(gt) * up).astype(BF16)
        acc = acc + _dot(act, wd_ref[lo:hi, :])
    o_ref[...] = h + _rms(acc, g2_ref[...])


def _ffn(h2, g1, wg, wu, wd, g2, tm=512, ff_chunk=1024):
    t, d = h2.shape
    row = pl.BlockSpec((tm, d), lambda i: (i, 0))
    return pl.pallas_call(
        functools.partial(_ffn_kernel, ff_chunk=ff_chunk),
        grid=(t // tm,),
        in_specs=[row, _const_spec(g1.shape), _const_spec(wg.shape), _const_spec(wu.shape),
                  _const_spec(wd.shape), _const_spec(g2.shape)],
        out_specs=row,
        out_shape=jax.ShapeDtypeStruct((t, d), F32),
        compiler_params=_params(1),
        name="ffn",
    )(h2, g1, wg, wu, wd, g2)


def _block_diag_ones(width, block, value):
    idx = jnp.arange(width) // block
    return jnp.where(idx[:, None] == idx[None, :], value, 0.0).astype(BF16)


def kernel(x, mem, pre1_g, post1_g, pre2_g, post2_g, mem_norm_g, w_in, fox_f_bias, rwkv_mu, rwkv_w0,
           rwkv_w_up, rwkv_a0, rwkv_a_up, rwkv_g_up, rwkv_k_k, rwkv_k_a, rwkv_r_k, rwkv_gn_g, rwkv_gn_b,
           w_mem_kv, w_fox_out, w_rwkv_out, w_mem_out, w_o, w_ffn_gate, w_ffn_up, w_ffn_down):
    batch, seq, d = x.shape
    mem_len = mem.shape[1]
    depth = w_in.shape[0]
    fox_heads = fox_f_bias.shape[1]
    fox_w = fox_heads * HEAD_DIM
    rw_w = rwkv_w0.shape[1]
    rw_cols = rwkv_mu.shape[1]
    mem_w = w_mem_kv.shape[2] // 2
    dec_lora = rwkv_w_up.shape[1]
    aaa_lora = rwkv_a_up.shape[1]
    assert dec_lora + aaa_lora == LANES and dec_lora == aaa_lora
    assert fox_heads <= 8 and fox_w % LANES == 0 and rw_w % LANES == 0
    row2 = lambda v: v.reshape(1, -1)

    h = x.reshape(batch * seq, d)
    mem2 = mem.reshape(batch * mem_len, d)
    bd_ones = _block_diag_ones(rw_w, HEAD_DIM, 1.0)
    bd_avg = _block_diag_ones(rw_w, HEAD_DIM, 1.0 / HEAD_DIM)
    for l in range(depth):
        wi = w_in[l]
        o_ff = 3 * fox_w
        o_rw = o_ff + fox_heads
        o_mq = o_rw + rw_cols
        o_gt = o_mq + mem_w
        scale = HEAD_DIM ** -0.5
        wqkv = jnp.concatenate([wi[:, :fox_w] * scale, wi[:, fox_w:o_ff]], axis=1).astype(BF16)
        wff = jnp.pad(wi[:, o_ff:o_rw], ((0, 0), (0, LANES - fox_heads))).astype(BF16)
        fbias = jnp.pad(row2(fox_f_bias[l]), ((0, 0), (0, LANES - fox_heads)))
        wrw = wi[:, o_rw:o_mq].astype(BF16)
        wmq = wi[:, o_mq:o_gt].astype(BF16)
        wgt = wi[:, o_gt:].astype(BF16)
        wlora = jnp.zeros((LANES, 2 * rw_w), F32)
        wlora = wlora.at[:dec_lora, :rw_w].set(rwkv_w_up[l]).at[dec_lora:, rw_w:].set(rwkv_a_up[l]).astype(BF16)

        qkv, ccol, crow, prw, mq, gates = _inproj(
            h, row2(pre1_g[l]), wqkv, wff, wrw, wmq, wgt, fbias,
            seq=seq, mem_scale=(mem_w // MEM_HEADS) ** -0.5)

        fox_out = _fox(qkv, ccol, crow, batch=batch, seq=seq)

        memkv = _memkv(mem2, row2(mem_norm_g[l]), w_mem_kv[l].astype(BF16))
        mem_out = _memattn(mq, memkv, batch=batch, seq=seq, mem_len=mem_len)

        prep = _rwkv_prep(prw, row2(rwkv_mu[l]), row2(rwkv_w0[l]), row2(rwkv_a0[l]), row2(rwkv_k_k[l]),
                          row2(rwkv_k_a[l]), row2(rwkv_r_k[l]), wlora, rwkv_g_up[l].astype(BF16), bd_ones,
                          seq=seq)
        rwkv_out = _rwkv_scan(*prep, row2(rwkv_gn_g[l]), row2(rwkv_gn_b[l]), bd_avg, batch=batch, seq=seq)

        h = _merge(h, fox_out, rwkv_out, mem_out, gates, w_fox_out[l].astype(BF16),
                   w_rwkv_out[l].astype(BF16), w_mem_out[l].astype(BF16), w_o[l].astype(BF16),
                   row2(post1_g[l]))
        h = _ffn(h, row2(pre2_g[l]), w_ffn_gate[l].astype(BF16), w_ffn_up[l].astype(BF16),
                 w_ffn_down[l].astype(BF16), row2(post2_g[l]))
    return h.reshape(batch, seq, d).astype(x.dtype)
```

```python
import functools

import jax
import jax.numpy as jnp
from jax import lax
from jax.experimental import pallas as pl
from jax.experimental.pallas import tpu as pltpu

F32 = jnp.float32
BF16 = jnp.bfloat16

NORM_EPS = 1e-6
GN_EPS = 64e-5
HEAD_DIM = 64
LANES = 128
SUBLANES = 8
LOG2E = 1.4426950408889634
MEM_HEADS = 4
CHUNK = 64
PAIR = 2 * CHUNK
NEG_BIG = -1e30
VMEM_LIMIT = 56 * 1024 * 1024


def _dot(a, b):
    return jnp.dot(a, b, preferred_element_type=F32)


def _dot_nt(a, b):
    return lax.dot_general(a, b, (((1,), (1,)), ((), ())), preferred_element_type=F32)


def _dot_tn(a, b):
    return lax.dot_general(a, b, (((0,), (0,)), ((), ())), preferred_element_type=F32)


def _split3(x):
    hi = x.astype(BF16)
    r1 = x - hi.astype(F32)
    mid = r1.astype(BF16)
    lo = (r1 - mid.astype(F32)).astype(BF16)
    return hi, mid, lo


def _dot_exact_lhs(m_bf16, x):
    hi, mid, lo = _split3(x)
    return _dot(m_bf16, hi) + _dot(m_bf16, mid) + _dot(m_bf16, lo)


def _dot_exact_rhs(x, m_bf16):
    hi, mid, lo = _split3(x)
    return _dot(hi, m_bf16) + _dot(mid, m_bf16) + _dot(lo, m_bf16)


def _rms(xf, g):
    return xf * lax.rsqrt(jnp.mean(xf * xf, axis=-1, keepdims=True) + NORM_EPS) * g


def _sigmoid(x):
    return 1.0 / (1.0 + jnp.exp(-x))


def _log_sigmoid(x):
    return jnp.minimum(x, 0.0) - jnp.log(1.0 + jnp.exp(-jnp.abs(x)))


def _const_spec(shape):
    nd = len(shape)
    return pl.BlockSpec(shape, lambda *_: (0,) * nd, pipeline_mode=pl.Buffered(1))


def _params(n_axes):
    return pltpu.CompilerParams(dimension_semantics=("arbitrary",) * n_axes,
                                vmem_limit_bytes=VMEM_LIMIT)


def _inproj_kernel(x_ref, g_ref, wqkv_ref, wff_ref, wrw_ref, wmq_ref, wgt_ref, fb_ref,
                   qkv_ref, crow_ref, prw_ref, mq_ref, gate_ref, carry_ref,
                   *, tiles_per_seq, mem_scale):
    i = pl.program_id(0)
    tm = x_ref.shape[0]

    @pl.when(i % tiles_per_seq == 0)
    def _():
        carry_ref[...] = jnp.zeros_like(carry_ref)

    u = _rms(x_ref[...], g_ref[...]).astype(BF16)
    qkv_ref[...] = _dot(u, wqkv_ref[...]).astype(BF16)
    prw_ref[...] = _dot(u, wrw_ref[...])
    mq_ref[...] = (_dot(u, wmq_ref[...]) * mem_scale).astype(BF16)
    d = gate_ref.shape[1] // 3
    for j in range(3):
        gate_ref[:, j * d:(j + 1) * d] = _sigmoid(_dot(u, wgt_ref[:, j * d:(j + 1) * d])).astype(BF16)

    ls = _log_sigmoid(_dot(u, wff_ref[...]) + fb_ref[...])
    row = lax.broadcasted_iota(jnp.int32, (tm, tm), 0)
    col = lax.broadcasted_iota(jnp.int32, (tm, tm), 1)
    tri = jnp.where(row >= col, 1.0, 0.0).astype(BF16)
    c = _dot_exact_lhs(tri, ls) + carry_ref[...]
    carry_ref[...] = c[tm - 1:tm, :]
    crow_ref[...] = jnp.transpose(c * LOG2E)[:crow_ref.shape[0], :]


def _inproj(x2, g, wqkv, wff, wrw, wmq, wgt, fbias, *, seq, mem_scale, tm=512):
    t, d = x2.shape
    outs = (
        jax.ShapeDtypeStruct((t, wqkv.shape[1]), BF16),
        jax.ShapeDtypeStruct((SUBLANES, t), F32),
        jax.ShapeDtypeStruct((t, wrw.shape[1]), F32),
        jax.ShapeDtypeStruct((t, wmq.shape[1]), BF16),
        jax.ShapeDtypeStruct((t, wgt.shape[1]), BF16),
    )
    row_spec = lambda w: pl.BlockSpec((tm, w), lambda i: (i, 0))
    return pl.pallas_call(
        functools.partial(_inproj_kernel, tiles_per_seq=seq // tm, mem_scale=mem_scale),
        grid=(t // tm,),
        in_specs=[row_spec(d), _const_spec(g.shape), _const_spec(wqkv.shape), _const_spec(wff.shape),
                  _const_spec(wrw.shape), _const_spec(wmq.shape), _const_spec(wgt.shape),
                  _const_spec(fbias.shape)],
        out_specs=[row_spec(wqkv.shape[1]), pl.BlockSpec((SUBLANES, tm), lambda i: (0, i)),
                   row_spec(wrw.shape[1]), row_spec(wmq.shape[1]), row_spec(wgt.shape[1])],
        out_shape=outs,
        scratch_shapes=[pltpu.VMEM((1, LANES), F32)],
        compiler_params=_params(1),
        name="inproj",
    )(x2, g, wqkv, wff, wrw, wmq, wgt, fbias)


def _fox_kernel(q_ref, k_ref, v_ref, crow_ref, o_ref, *, rb):
    hp = pl.program_id(1)
    qi = pl.program_id(2)
    tq = q_ref.shape[0]
    bk = tq
    n_r = tq // rb
    first_q = lax.broadcasted_iota(jnp.int32, (rb, LANES), 1) < HEAD_DIM
    first_k = lax.broadcasted_iota(jnp.int32, (bk, LANES), 1) < HEAD_DIM
    sub = lax.broadcasted_iota(jnp.int32, (crow_ref.shape[0], bk), 0)
    col = lax.broadcasted_iota(jnp.int32, (rb, bk), 1)
    row = lax.broadcasted_iota(jnp.int32, (rb, bk), 0)
    blocks = [(e, r) for r in range(n_r) for e in range(2)]
    q_blk = {}
    for e, r in blocks:
        q2 = q_ref[r * rb:(r + 1) * rb, :]
        zero = jnp.zeros_like(q2)
        q_blk[e, r] = jnp.where(first_q, q2, zero) if e == 0 else jnp.where(first_q, zero, q2)

    def tile(j, carry, on_diagonal):
        off = pl.multiple_of(j * bk, bk)
        kt = k_ref[pl.ds(off, bk), :]
        vt = v_ref[pl.ds(off, bk), :]
        one = jnp.ones_like(vt)
        v_aug = (jnp.where(first_k, vt, one), jnp.where(first_k, one, vt))
        crow_tile = crow_ref[:, pl.ds(off, bk)]
        crow = [jnp.sum(jnp.where(sub == 2 * hp + e, crow_tile, 0.0), axis=0, keepdims=True) for e in range(2)]
        new = dict(carry)
        scores = {}

        def issue_scores(r):
            for e in range(2):
                scores[e, r] = _dot_nt(q_blk[e, r], kt)

        def finish(r):
            for e in range(2):
                m, acc = carry[e, r]
                s = scores.pop((e, r)) - crow[e]
                if on_diagonal:
                    s = jnp.where(row + r * rb >= col, s, NEG_BIG)
                m_new = jnp.maximum(m, jnp.max(s, axis=-1, keepdims=True))
                p = jnp.exp2(s - m_new).astype(BF16)
                new[e, r] = (m_new, jnp.exp2(m - m_new) * acc + _dot(p, v_aug[e]))

        issue_scores(0)
        for r in range(n_r):
            if r + 1 < n_r:
                issue_scores(r + 1)
            finish(r)
        return new

    init = {b: (jnp.full((rb, 1), NEG_BIG, F32), jnp.zeros((rb, LANES), F32)) for b in blocks}
    carry = lax.fori_loop(0, qi, lambda j, c: tile(j, c, False), init)
    final = tile(qi, carry, True)
    for r in range(n_r):
        a0, a1 = final[0, r][1], final[1, r][1]
        den = jnp.where(first_q, pltpu.roll(a0, HEAD_DIM, axis=1), pltpu.roll(a1, HEAD_DIM, axis=1))
        o_ref[r * rb:(r + 1) * rb, :] = (jnp.where(first_q, a0, a1) / den).astype(o_ref.dtype)


def _fox(qkv, crow, *, batch, seq, tq=512, rb=128):
    t = qkv.shape[0]
    n_pairs = qkv.shape[1] // (3 * LANES)
    nq = seq // tq
    return pl.pallas_call(
        functools.partial(_fox_kernel, rb=rb),
        grid=(batch, n_pairs, nq),
        in_specs=[
            pl.BlockSpec((tq, LANES), lambda b, h, i: (b * nq + i, h)),
            pl.BlockSpec((seq, LANES), lambda b, h, i: (b, n_pairs + h)),
            pl.BlockSpec((seq, LANES), lambda b, h, i: (b, 2 * n_pairs + h)),
            pl.BlockSpec((crow.shape[0], seq), lambda b, h, i: (0, b)),
        ],
        out_specs=pl.BlockSpec((tq, LANES), lambda b, h, i: (b * nq + i, h)),
        out_shape=jax.ShapeDtypeStruct((t, n_pairs * LANES), BF16),
        compiler_params=_params(3),
        name="fox_attention",
    )(qkv, qkv, qkv, crow)


def _memkv_kernel(m_ref, g_ref, w_ref, o_ref):
    o_ref[...] = _dot(_rms(m_ref[...], g_ref[...]).astype(BF16), w_ref[...]).astype(BF16)


def _memkv(mem2, g, w, tm=512):
    t, d = mem2.shape
    return pl.pallas_call(
        _memkv_kernel,
        grid=(t // tm,),
        in_specs=[pl.BlockSpec((tm, d), lambda i: (i, 0)), _const_spec(g.shape), _const_spec(w.shape)],
        out_specs=pl.BlockSpec((tm, w.shape[1]), lambda i: (i, 0)),
        out_shape=jax.ShapeDtypeStruct((t, w.shape[1]), BF16),
        compiler_params=_params(1),
        name="mem_kv",
    )(mem2, g, w)


def _memattn_kernel(q_ref, kv_ref, o_ref):
    w = q_ref.shape[1]
    hd = w // MEM_HEADS
    for h in range(MEM_HEADS):
        q = q_ref[:, h * hd:(h + 1) * hd]
        k = kv_ref[:, h * hd:(h + 1) * hd]
        v = kv_ref[:, w + h * hd:w + (h + 1) * hd]
        s = _dot_nt(q, k)
        p = jnp.exp(s - jnp.max(s, axis=-1, keepdims=True))
        l = jnp.sum(p, axis=-1, keepdims=True)
        o_ref[:, h * hd:(h + 1) * hd] = (_dot(p.astype(BF16), v) / l).astype(o_ref.dtype)


def _memattn(mq, memkv, *, batch, seq, mem_len, tq=512):
    t, w = mq.shape
    nq = seq // tq
    return pl.pallas_call(
        _memattn_kernel,
        grid=(batch, nq),
        in_specs=[pl.BlockSpec((tq, w), lambda b, i: (b * nq + i, 0)),
                  pl.BlockSpec((mem_len, 2 * w), lambda b, i: (b, 0))],
        out_specs=pl.BlockSpec((tq, w), lambda b, i: (b * nq + i, 0)),
        out_shape=jax.ShapeDtypeStruct((t, w), BF16),
        compiler_params=_params(2),
        name="mem_attention",
    )(mq, memkv)


def _rwkv_prep_kernel(p_ref, pprev_ref, mu_ref, w0_ref, a0_ref, kk_ref, ka_ref, rk_ref,
                      wlora_ref, gup_ref, bd_ref,
                      at_ref, rt_ref, bt_ref, kt_ref, bh_ref, kh_ref, v_ref, gam_ref, bonus_ref, g_ref,
                      *, tiles_per_seq):
    i = pl.program_id(0)
    tm = p_ref.shape[0]
    w = v_ref.shape[1]
    p = p_ref[...]
    prev_rows = pprev_ref[...]
    prev_last = jnp.where(i % tiles_per_seq == 0, 0.0, prev_rows[prev_rows.shape[0] - 1:, :])
    rowid = lax.broadcasted_iota(jnp.int32, p.shape, 0)
    shifted = jnp.where(rowid == 0, prev_last, pltpu.roll(p, 1, axis=0))
    xs = p + (shifted - p) * mu_ref[...]
    r = xs[:, 0:w]
    k = xs[:, w:2 * w]
    v = xs[:, 2 * w:3 * w]
    lora_in = xs[:, 3 * w:3 * w + LANES]
    gd = xs[:, 3 * w + LANES:]
    lane = lax.broadcasted_iota(jnp.int32, lora_in.shape, 1)
    lora_in = jnp.where(lane < lora_in.shape[1] // 2, jnp.tanh(lora_in), lora_in)
    lora = _dot(lora_in.astype(BF16), wlora_ref[...])
    w_log = -jnp.exp(_log_sigmoid(w0_ref[...] + lora[:, :w]) - 0.5)
    a = _sigmoid(a0_ref[...] + lora[:, w:])
    g_ref[...] = _dot(_sigmoid(gd).astype(BF16), gup_ref[...])

    bd = bd_ref[...]
    kk = k * kk_ref[...]
    kk = kk * lax.rsqrt(jnp.maximum(_dot_exact_rhs(kk * kk, bd), 1e-24))
    k2 = k * (1.0 + (a - 1.0) * ka_ref[...])
    bonus_ref[...] = _dot_exact_rhs(r * k2 * rk_ref[...], bd) * v

    ti = lax.broadcasted_iota(jnp.int32, (tm, tm), 0)
    si = lax.broadcasted_iota(jnp.int32, (tm, tm), 1)
    same = (ti // CHUNK) == (si // CHUNK)
    lower = jnp.where(same & (si <= ti), 1.0, 0.0).astype(BF16)
    upper = jnp.where(same & (si > ti), 1.0, 0.0).astype(BF16)
    hi, mid, lo = _split3(w_log)
    cum = _dot(lower, hi) + _dot(lower, mid) + _dot(lower, lo)
    rem = _dot(upper, hi) + _dot(upper, mid) + _dot(upper, lo)
    n_sel = gam_ref.shape[1]
    ci = lax.broadcasted_iota(jnp.int32, (n_sel, tm), 0)
    sj = lax.broadcasted_iota(jnp.int32, (n_sel, tm), 1)
    sel = jnp.where(ci == sj // CHUNK, 1.0, 0.0).astype(BF16)
    gam_ref[0] = jnp.exp(_dot(sel, hi) + _dot(sel, mid) + _dot(sel, lo))

    e_prev = jnp.exp(cum - w_log)
    e_cum = jnp.exp(cum)
    e_neg = jnp.exp(-cum)
    e_rem = jnp.exp(rem)
    b = kk * a
    at_ref[...] = (-kk * e_prev).astype(BF16)
    rt_ref[...] = (r * e_cum).astype(BF16)
    bt_ref[...] = (b * e_neg).astype(BF16)
    kt_ref[...] = (k2 * e_neg).astype(BF16)
    bh_ref[...] = (b * e_rem).astype(BF16)
    kh_ref[...] = (k2 * e_rem).astype(BF16)
    v_ref[...] = v.astype(BF16)


def _rwkv_prep(prw, mu, w0, a0, k_k, k_a, r_k, wlora, gup, bd, *, seq, tm=256):
    t, cols = prw.shape
    w = w0.shape[1]
    n_tiles = t // tm
    n_sel = 8
    row = lambda width: pl.BlockSpec((tm, width), lambda i: (i, 0))
    tok_bf = jax.ShapeDtypeStruct((t, w), BF16)
    tok_f = jax.ShapeDtypeStruct((t, w), F32)
    prev_rows = 8
    return pl.pallas_call(
        functools.partial(_rwkv_prep_kernel, tiles_per_seq=seq // tm),
        grid=(n_tiles,),
        in_specs=[row(cols),
                  pl.BlockSpec((prev_rows, cols), lambda i: (jnp.maximum(i * (tm // prev_rows) - 1, 0), 0)),
                  _const_spec(mu.shape), _const_spec(w0.shape), _const_spec(a0.shape),
                  _const_spec(k_k.shape), _const_spec(k_a.shape), _const_spec(r_k.shape),
                  _const_spec(wlora.shape), _const_spec(gup.shape), _const_spec(bd.shape)],
        out_specs=[row(w)] * 7 + [pl.BlockSpec((1, n_sel, w), lambda i: (i, 0, 0)), row(w), row(w)],
        out_shape=[tok_bf] * 7 + [jax.ShapeDtypeStruct((n_tiles, n_sel, w), F32), tok_f, tok_f],
        compiler_params=_params(1),
        name="rwkv_prep",
    )(prw, prw, mu, w0, a0, k_k, k_a, r_k, wlora, gup, bd)


def _rwkv_scan_kernel(at_ref, rt_ref, bt_ref, kt_ref, bh_ref, kh_ref, v_ref, gam_ref, bonus_ref, g_ref,
                      gng_ref, gnb_ref, bdavg_ref, o_ref, state_ref, y_ref):
    @pl.when(pl.program_id(1) == 0)
    def _():
        state_ref[...] = jnp.zeros_like(state_ref)

    n_t, w = v_ref.shape
    n_pairs = w // LANES
    n_chunks = n_t // CHUNK
    lane = lax.broadcasted_iota(jnp.int32, (CHUNK, LANES), 1)
    first = lane < HEAD_DIM
    ri = lax.broadcasted_iota(jnp.int32, (PAIR, PAIR), 0)
    ci = lax.broadcasted_iota(jnp.int32, (PAIR, PAIR), 1)
    strict = ri > ci
    incl = ri >= ci
    eye = jnp.where(ri == ci, 1.0, 0.0)

    def stacked(ref, c, p):
        x = ref[c * CHUNK:(c + 1) * CHUNK, p * LANES:(p + 1) * LANES]
        z = jnp.zeros_like(x)
        return jnp.concatenate([jnp.where(first, x, z), jnp.where(first, z, x)], axis=0)

    cps = [(c, p) for c in range(n_chunks) for p in range(n_pairs)]

    gram = {}
    for cp in cps:
        lhs = jnp.concatenate([stacked(at_ref, *cp), stacked(rt_ref, *cp)], axis=0)
        rhs = jnp.concatenate([stacked(bt_ref, *cp), stacked(kt_ref, *cp)], axis=0)
        gram[cp] = _dot_nt(lhs, rhs)
    s_inv, q_pow, a_ak, a_rb, a_rk = {}, {}, {}, {}, {}
    for cp in cps:
        g_all = gram.pop(cp)
        n_ab = jnp.where(strict, g_all[:PAIR, :PAIR], 0.0)
        a_ak[cp] = jnp.where(strict, g_all[:PAIR, PAIR:], 0.0).astype(BF16)
        a_rb[cp] = jnp.where(incl, g_all[PAIR:, :PAIR], 0.0).astype(BF16)
        a_rk[cp] = jnp.where(incl, g_all[PAIR:, PAIR:], 0.0).astype(BF16)
        s_inv[cp] = eye + n_ab
        q_pow[cp] = n_ab.astype(BF16)
    for cp in cps:
        q_pow[cp] = _dot(q_pow[cp], q_pow[cp]).astype(BF16)
    span = 2
    while 2 * span < CHUNK:
        for cp in cps:
            both = _dot(q_pow[cp], jnp.concatenate([s_inv[cp].astype(BF16), q_pow[cp]], axis=1))
            s_inv[cp] = s_inv[cp] + both[:, :PAIR]
            q_pow[cp] = both[:, PAIR:].astype(BF16)
        span *= 2
    for cp in cps:
        s_inv[cp] = (s_inv[cp] + _dot(q_pow[cp], s_inv[cp].astype(BF16))).astype(BF16)

    pairs = range(n_pairs)
    states = [state_ref[p] for p in pairs]
    for c in range(n_chunks):
        h_bf = [states[p].astype(BF16) for p in pairs]
        v_s = [stacked(v_ref, c, p) for p in pairs]
        w_rhs = [_dot_nt(stacked(at_ref, c, p), h_bf[p]) + _dot(a_ak[c, p], v_s[p]) for p in pairs]
        u = [_dot(s_inv[c, p], w_rhs[p].astype(BF16)).astype(BF16) for p in pairs]
        y_s = [_dot_nt(stacked(rt_ref, c, p), h_bf[p]) + _dot(a_rb[c, p], u[p]) + _dot(a_rk[c, p], v_s[p])
               for p in pairs]
        for p in pairs:
            gam = gam_ref[0, c:c + 1, p * LANES:(p + 1) * LANES]
            states[p] = (states[p] * gam + _dot_tn(u[p], stacked(bh_ref, c, p))
                         + _dot_tn(v_s[p], stacked(kh_ref, c, p)))
            y_ref[c * CHUNK:(c + 1) * CHUNK, p * LANES:(p + 1) * LANES] = y_s[p][:CHUNK] + y_s[p][CHUNK:]
    for p in pairs:
        state_ref[p] = states[p]

    y = y_ref[...]
    bdavg = bdavg_ref[...]
    dev = y - _dot_exact_rhs(y, bdavg)
    var = _dot_exact_rhs(dev * dev, bdavg)
    yn = dev * lax.rsqrt(var + GN_EPS) * gng_ref[...] + gnb_ref[...]
    o_ref[...] = ((yn + bonus_ref[...]) * g_ref[...]).astype(o_ref.dtype)


def _rwkv_scan(at, rt, bt, kt, bh, kh, v, gam, bonus, g, gn_g, gn_b, bdavg, *, batch, seq):
    t, w = v.shape
    n_sel = gam.shape[1]
    tl = t // gam.shape[0]
    nt = seq // tl
    tok = pl.BlockSpec((tl, w), lambda b, i: (b * nt + i, 0))
    return pl.pallas_call(
        _rwkv_scan_kernel,
        grid=(batch, nt),
        in_specs=[tok] * 7 + [pl.BlockSpec((1, n_sel, w), lambda b, i: (b * nt + i, 0, 0)), tok, tok,
                              _const_spec(gn_g.shape), _const_spec(gn_b.shape), _const_spec(bdavg.shape)],
        out_specs=tok,
        out_shape=jax.ShapeDtypeStruct((t, w), BF16),
        scratch_shapes=[pltpu.VMEM((w // LANES, LANES, LANES), F32), pltpu.VMEM((tl, w), F32)],
        compiler_params=_params(2),
        name="rwkv_scan",
    )(at, rt, bt, kt, bh, kh, v, gam, bonus, g, gn_g, gn_b, bdavg)


def _merge_kernel(x_ref, fox_ref, rw_ref, mem_ref, gate_ref, wf_ref, wr_ref, wm_ref, wo_ref, g_ref, o_ref):
    d = x_ref.shape[1]
    merged = (gate_ref[:, 0:d].astype(F32) * _dot(fox_ref[...], wf_ref[...])
              + gate_ref[:, d:2 * d].astype(F32) * _dot(rw_ref[...], wr_ref[...])
              + gate_ref[:, 2 * d:3 * d].astype(F32) * _dot(mem_ref[...], wm_ref[...]))
    y = _dot(merged.astype(BF16), wo_ref[...])
    o_ref[...] = x_ref[...] + _rms(y, g_ref[...])


def _merge(x2, fox, rw, mem, gates, wf, wr, wm, wo, g, tm=512):
    t, d = x2.shape
    row = lambda width: pl.BlockSpec((tm, width), lambda i: (i, 0))
    return pl.pallas_call(
        _merge_kernel,
        grid=(t // tm,),
        in_specs=[row(d), row(fox.shape[1]), row(rw.shape[1]), row(mem.shape[1]), row(gates.shape[1]),
                  _const_spec(wf.shape), _const_spec(wr.shape), _const_spec(wm.shape), _const_spec(wo.shape),
                  _const_spec(g.shape)],
        out_specs=row(d),
        out_shape=jax.ShapeDtypeStruct((t, d), F32),
        compiler_params=_params(1),
        name="merge_out",
    )(x2, fox, rw, mem, gates, wf, wr, wm, wo, g)


def _ffn_kernel(h_ref, g1_ref, wg_ref, wu_ref, wd_ref, g2_ref, o_ref, *, ff_chunk):
    h = h_ref[...]
    u = _rms(h, g1_ref[...]).astype(BF16)
    d_ff = wg_ref.shape[1]
    acc = jnp.zeros(h.shape, F32)
    for lo in range(0, d_ff, ff_chunk):
        hi = min(lo + ff_chunk, d_ff)
        gt = _dot(u, wg_ref[:, lo:hi])
        up = _dot(u, wu_ref[:, lo:hi])
        act = (gt * _sigmoid(gt) * up).astype(BF16)
        acc = acc + _dot(act, wd_ref[lo:hi, :])
    o_ref[...] = h + _rms(acc, g2_ref[...])


def _ffn(h2, g1, wg, wu, wd, g2, tm=512, ff_chunk=1024):
    t, d = h2.shape
    row = pl.BlockSpec((tm, d), lambda i: (i, 0))
    return pl.pallas_call(
        functools.partial(_ffn_kernel, ff_chunk=ff_chunk),
        grid=(t // tm,),
        in_specs=[row, _const_spec(g1.shape), _const_spec(wg.shape), _const_spec(wu.shape),
                  _const_spec(wd.shape), _const_spec(g2.shape)],
        out_specs=row,
        out_shape=jax.ShapeDtypeStruct((t, d), F32),
        compiler_params=_params(1),
        name="ffn",
    )(h2, g1, wg, wu, wd, g2)


def _block_diag_ones(width, block, value):
    idx = jnp.arange(width) // block
    return jnp.where(idx[:, None] == idx[None, :], value, 0.0).astype(BF16)


def kernel(x, mem, pre1_g, post1_g, pre2_g, post2_g, mem_norm_g, w_in, fox_f_bias, rwkv_mu, rwkv_w0,
           rwkv_w_up, rwkv_a0, rwkv_a_up, rwkv_g_up, rwkv_k_k, rwkv_k_a, rwkv_r_k, rwkv_gn_g, rwkv_gn_b,
           w_mem_kv, w_fox_out, w_rwkv_out, w_mem_out, w_o, w_ffn_gate, w_ffn_up, w_ffn_down):
    batch, seq, d = x.shape
    mem_len = mem.shape[1]
    depth = w_in.shape[0]
    fox_heads = fox_f_bias.shape[1]
    fox_w = fox_heads * HEAD_DIM
    rw_w = rwkv_w0.shape[1]
    rw_cols = rwkv_mu.shape[1]
    mem_w = w_mem_kv.shape[2] // 2
    dec_lora = rwkv_w_up.shape[1]
    aaa_lora = rwkv_a_up.shape[1]
    assert dec_lora + aaa_lora == LANES and dec_lora == aaa_lora
    assert fox_heads <= SUBLANES and fox_w % LANES == 0 and rw_w % LANES == 0
    row2 = lambda v: v.reshape(1, -1)

    h = x.reshape(batch * seq, d)
    mem2 = mem.reshape(batch * mem_len, d)
    bd_ones = _block_diag_ones(rw_w, HEAD_DIM, 1.0)
    bd_avg = _block_diag_ones(rw_w, HEAD_DIM, 1.0 / HEAD_DIM)
    for l in range(depth):
        wi = w_in[l]
        o_ff = 3 * fox_w
        o_rw = o_ff + fox_heads
        o_mq = o_rw + rw_cols
        o_gt = o_mq + mem_w
        scale = HEAD_DIM ** -0.5 * LOG2E
        wqkv = jnp.concatenate([wi[:, :fox_w] * scale, wi[:, fox_w:o_ff]], axis=1).astype(BF16)
        wff = jnp.pad(wi[:, o_ff:o_rw], ((0, 0), (0, LANES - fox_heads))).astype(BF16)
        fbias = jnp.pad(row2(fox_f_bias[l]), ((0, 0), (0, LANES - fox_heads)))
        wrw = wi[:, o_rw:o_mq].astype(BF16)
        wmq = wi[:, o_mq:o_gt].astype(BF16)
        wgt = wi[:, o_gt:].astype(BF16)
        wlora = jnp.zeros((LANES, 2 * rw_w), F32)
        wlora = wlora.at[:dec_lora, :rw_w].set(rwkv_w_up[l]).at[dec_lora:, rw_w:].set(rwkv_a_up[l]).astype(BF16)

        qkv, crow, prw, mq, gates = _inproj(
            h, row2(pre1_g[l]), wqkv, wff, wrw, wmq, wgt, fbias,
            seq=seq, mem_scale=(mem_w // MEM_HEADS) ** -0.5)

        fox_out = _fox(qkv, crow, batch=batch, seq=seq)

        memkv = _memkv(mem2, row2(mem_norm_g[l]), w_mem_kv[l].astype(BF16))
        mem_out = _memattn(mq, memkv, batch=batch, seq=seq, mem_len=mem_len)

        prep = _rwkv_prep(prw, row2(rwkv_mu[l]), row2(rwkv_w0[l]), row2(rwkv_a0[l]), row2(rwkv_k_k[l]),
                          row2(rwkv_k_a[l]), row2(rwkv_r_k[l]), wlora, rwkv_g_up[l].astype(BF16), bd_ones,
                          seq=seq)
        rwkv_out = _rwkv_scan(*prep, row2(rwkv_gn_g[l]), row2(rwkv_gn_b[l]), bd_avg, batch=batch, seq=seq)

        h = _merge(h, fox_out, rwkv_out, mem_out, gates, w_fox_out[l].astype(BF16),
                   w_rwkv_out[l].astype(BF16), w_mem_out[l].astype(BF16), w_o[l].astype(BF16),
                   row2(post1_g[l]))
        h = _ffn(h, row2(pre2_g[l]), w_ffn_gate[l].astype(BF16), w_ffn_up[l].astype(BF16),
                 w_ffn_down[l].astype(BF16), row2(post2_g[l]))
    return h.reshape(batch, seq, d).astype(x.dtype)
```

```python
import functools

import jax
import jax.numpy as jnp
from jax import lax
from jax.experimental import pallas as pl
from jax.experimental.pallas import tpu as pltpu

F32 = jnp.float32
BF16 = jnp.bfloat16

NORM_EPS = 1e-6
GN_EPS = 64e-5
HEAD_DIM = 64
LANES = 128
SUBLANES = 8
LOG2E = 1.4426950408889634
MEM_HEADS = 4
CHUNK = 64
PAIR = 2 * CHUNK
F32_TERMS = 3
DECAY_SUM_TERMS = 2
HEAD_SUM_TERMS = 1
NEG_BIG = -1e30
VMEM_LIMIT = 56 * 1024 * 1024


def _dot(a, b):
    return jnp.dot(a, b, preferred_element_type=F32)


def _dot_nt(a, b):
    return lax.dot_general(a, b, (((1,), (1,)), ((), ())), preferred_element_type=F32)


def _dot_tn(a, b):
    return lax.dot_general(a, b, (((0,), (0,)), ((), ())), preferred_element_type=F32)


def _bf16_terms(x, n):
    terms = []
    for _ in range(n - 1):
        t = x.astype(BF16)
        terms.append(t)
        x = x - t.astype(F32)
    return terms + [x.astype(BF16)]


def _dot_terms_lhs(m_bf16, x, n):
    return sum(_dot(m_bf16, t) for t in _bf16_terms(x, n))


def _dot_terms_rhs(x, m_bf16, n):
    return sum(_dot(t, m_bf16) for t in _bf16_terms(x, n))


def _rms(xf, g):
    return xf * lax.rsqrt(jnp.mean(xf * xf, axis=-1, keepdims=True) + NORM_EPS) * g


def _sigmoid(x):
    return 1.0 / (1.0 + jnp.exp(-x))


def _log_sigmoid(x):
    return jnp.minimum(x, 0.0) - jnp.log(1.0 + jnp.exp(-jnp.abs(x)))


def _const_spec(shape):
    nd = len(shape)
    return pl.BlockSpec(shape, lambda *_: (0,) * nd, pipeline_mode=pl.Buffered(1))


def _params(n_axes):
    return pltpu.CompilerParams(dimension_semantics=("arbitrary",) * n_axes,
                                vmem_limit_bytes=VMEM_LIMIT)


def _inproj_kernel(x_ref, g_ref, wqkv_ref, wff_ref, wrw_ref, wmq_ref, wgt_ref, fb_ref,
                   qkv_ref, crow_ref, prw_ref, mq_ref, gate_ref, carry_ref,
                   *, tiles_per_seq, mem_scale):
    i = pl.program_id(0)
    tm = x_ref.shape[0]

    @pl.when(i % tiles_per_seq == 0)
    def _():
        carry_ref[...] = jnp.zeros_like(carry_ref)

    u = _rms(x_ref[...], g_ref[...]).astype(BF16)
    qkv_ref[...] = _dot(u, wqkv_ref[...]).astype(BF16)
    prw_ref[...] = _dot(u, wrw_ref[...])
    mq_ref[...] = (_dot(u, wmq_ref[...]) * mem_scale).astype(BF16)
    d = gate_ref.shape[1] // 3
    for j in range(3):
        gate_ref[:, j * d:(j + 1) * d] = _sigmoid(_dot(u, wgt_ref[:, j * d:(j + 1) * d])).astype(BF16)

    ls = _log_sigmoid(_dot(u, wff_ref[...]) + fb_ref[...])
    row = lax.broadcasted_iota(jnp.int32, (tm, tm), 0)
    col = lax.broadcasted_iota(jnp.int32, (tm, tm), 1)
    tri = jnp.where(row >= col, 1.0, 0.0).astype(BF16)
    c = _dot_terms_lhs(tri, ls, F32_TERMS) + carry_ref[...]
    carry_ref[...] = c[tm - 1:tm, :]
    crow_ref[...] = jnp.transpose(c * LOG2E)[:crow_ref.shape[0], :]


def _inproj(x2, g, wqkv, wff, wrw, wmq, wgt, fbias, *, seq, mem_scale, tm=512):
    t, d = x2.shape
    outs = (
        jax.ShapeDtypeStruct((t, wqkv.shape[1]), BF16),
        jax.ShapeDtypeStruct((SUBLANES, t), F32),
        jax.ShapeDtypeStruct((t, wrw.shape[1]), F32),
        jax.ShapeDtypeStruct((t, wmq.shape[1]), BF16),
        jax.ShapeDtypeStruct((t, wgt.shape[1]), BF16),
    )
    row_spec = lambda w: pl.BlockSpec((tm, w), lambda i: (i, 0))
    return pl.pallas_call(
        functools.partial(_inproj_kernel, tiles_per_seq=seq // tm, mem_scale=mem_scale),
        grid=(t // tm,),
        in_specs=[row_spec(d), _const_spec(g.shape), _const_spec(wqkv.shape), _const_spec(wff.shape),
                  _const_spec(wrw.shape), _const_spec(wmq.shape), _const_spec(wgt.shape),
                  _const_spec(fbias.shape)],
        out_specs=[row_spec(wqkv.shape[1]), pl.BlockSpec((SUBLANES, tm), lambda i: (0, i)),
                   row_spec(wrw.shape[1]), row_spec(wmq.shape[1]), row_spec(wgt.shape[1])],
        out_shape=outs,
        scratch_shapes=[pltpu.VMEM((1, LANES), F32)],
        compiler_params=_params(1),
        name="inproj",
    )(x2, g, wqkv, wff, wrw, wmq, wgt, fbias)


def _fox_kernel(q_ref, k_ref, v_ref, crow_ref, o_ref, *, bk, rb, ahead):
    hp = pl.program_id(1)
    seq = q_ref.shape[0]
    first_q = lax.broadcasted_iota(jnp.int32, (rb, LANES), 1) < HEAD_DIM
    lower = (lax.broadcasted_iota(jnp.int32, (rb, rb), 0) >= lax.broadcasted_iota(jnp.int32, (rb, rb), 1))
    sub = lax.broadcasted_iota(jnp.int32, crow_ref.shape, 0)
    crow_all = crow_ref[...]
    crow = [jnp.sum(jnp.where(sub == 2 * hp + e, crow_all, 0.0), axis=0, keepdims=True) for e in range(2)]

    units = []
    for r in range(seq // rb):
        stop = (r + 1) * rb
        units += [(r, k0, min(bk, stop - k0)) for k0 in range(0, stop, bk)]
    units.sort(key=lambda u: (u[0] * rb // bk, u[1], u[0]))

    scores, state = {}, {}

    def issue_scores(u):
        r, k0, width = u
        q2 = q_ref[r * rb:(r + 1) * rb, :]
        zero = jnp.zeros_like(q2)
        kt = k_ref[k0:k0 + width, :]
        scores[u, 0] = _dot_nt(jnp.where(first_q, q2, zero), kt)
        scores[u, 1] = _dot_nt(jnp.where(first_q, zero, q2), kt)

    def finish(u):
        r, k0, width = u
        vt = v_ref[k0:k0 + width, :]
        one = jnp.ones_like(vt)
        first_k = lax.broadcasted_iota(jnp.int32, vt.shape, 1) < HEAD_DIM
        on_diagonal = k0 + width == (r + 1) * rb
        for e in range(2):
            v_aug = jnp.where(first_k, vt, one) if e == 0 else jnp.where(first_k, one, vt)
            s = scores.pop((u, e)) - crow[e][:, k0:k0 + width]
            if on_diagonal:
                tail = jnp.where(lower, s[:, width - rb:], NEG_BIG)
                s = tail if width == rb else jnp.concatenate([s[:, :width - rb], tail], axis=1)
            m_tile = jnp.max(s, axis=-1, keepdims=True)
            if k0 == 0:
                m_new = m_tile
                acc = _dot(jnp.exp2(s - m_new).astype(BF16), v_aug)
            else:
                m_old, acc_old = state[r, e]
                m_new = jnp.maximum(m_old, m_tile)
                acc = jnp.exp2(m_old - m_new) * acc_old + _dot(jnp.exp2(s - m_new).astype(BF16), v_aug)
            state[r, e] = (m_new, acc)
        if on_diagonal:
            a0, a1 = state.pop((r, 0))[1], state.pop((r, 1))[1]
            den = jnp.where(first_q, pltpu.roll(a0, HEAD_DIM, axis=1), pltpu.roll(a1, HEAD_DIM, axis=1))
            o_ref[r * rb:(r + 1) * rb, :] = (jnp.where(first_q, a0, a1) / den).astype(o_ref.dtype)

    for u in units[:ahead]:
        issue_scores(u)
    for n, u in enumerate(units):
        if n + ahead < len(units):
            issue_scores(units[n + ahead])
        finish(u)


def _fox(qkv, crow, *, batch, seq, bk=512, rb=128, ahead=2):
    t = qkv.shape[0]
    n_pairs = qkv.shape[1] // (3 * LANES)
    return pl.pallas_call(
        functools.partial(_fox_kernel, bk=bk, rb=rb, ahead=ahead),
        grid=(batch, n_pairs),
        in_specs=[
            pl.BlockSpec((seq, LANES), lambda b, h: (b, h)),
            pl.BlockSpec((seq, LANES), lambda b, h: (b, n_pairs + h)),
            pl.BlockSpec((seq, LANES), lambda b, h: (b, 2 * n_pairs + h)),
            pl.BlockSpec((crow.shape[0], seq), lambda b, h: (0, b)),
        ],
        out_specs=pl.BlockSpec((seq, LANES), lambda b, h: (b, h)),
        out_shape=jax.ShapeDtypeStruct((t, n_pairs * LANES), BF16),
        compiler_params=_params(2),
        name="fox_attention",
    )(qkv, qkv, qkv, crow)


def _memkv_kernel(m_ref, g_ref, w_ref, o_ref):
    o_ref[...] = _dot(_rms(m_ref[...], g_ref[...]).astype(BF16), w_ref[...]).astype(BF16)


def _memkv(mem2, g, w, tm=512):
    t, d = mem2.shape
    return pl.pallas_call(
        _memkv_kernel,
        grid=(t // tm,),
        in_specs=[pl.BlockSpec((tm, d), lambda i: (i, 0)), _const_spec(g.shape), _const_spec(w.shape)],
        out_specs=pl.BlockSpec((tm, w.shape[1]), lambda i: (i, 0)),
        out_shape=jax.ShapeDtypeStruct((t, w.shape[1]), BF16),
        compiler_params=_params(1),
        name="mem_kv",
    )(mem2, g, w)


def _memattn_kernel(q_ref, kv_ref, o_ref):
    w = q_ref.shape[1]
    hd = w // MEM_HEADS
    for h in range(MEM_HEADS):
        q = q_ref[:, h * hd:(h + 1) * hd]
        k = kv_ref[:, h * hd:(h + 1) * hd]
        v = kv_ref[:, w + h * hd:w + (h + 1) * hd]
        s = _dot_nt(q, k)
        p = jnp.exp(s - jnp.max(s, axis=-1, keepdims=True))
        l = jnp.sum(p, axis=-1, keepdims=True)
        o_ref[:, h * hd:(h + 1) * hd] = (_dot(p.astype(BF16), v) / l).astype(o_ref.dtype)


def _memattn(mq, memkv, *, batch, seq, mem_len, tq=512):
    t, w = mq.shape
    nq = seq // tq
    return pl.pallas_call(
        _memattn_kernel,
        grid=(batch, nq),
        in_specs=[pl.BlockSpec((tq, w), lambda b, i: (b * nq + i, 0)),
                  pl.BlockSpec((mem_len, 2 * w), lambda b, i: (b, 0))],
        out_specs=pl.BlockSpec((tq, w), lambda b, i: (b * nq + i, 0)),
        out_shape=jax.ShapeDtypeStruct((t, w), BF16),
        compiler_params=_params(2),
        name="mem_attention",
    )(mq, memkv)


def _rwkv_prep_kernel(p_ref, pprev_ref, mu_ref, w0_ref, a0_ref, kk_ref, ka_ref, rk_ref,
                      wlora_ref, gup_ref, bd_ref,
                      at_ref, rt_ref, bt_ref, kt_ref, bh_ref, kh_ref, v_ref, gam_ref, bonus_ref, g_ref,
                      *, tiles_per_seq):
    i = pl.program_id(0)
    tm = p_ref.shape[0]
    w = v_ref.shape[1]
    p = p_ref[...]
    prev_rows = pprev_ref[...]
    prev_last = jnp.where(i % tiles_per_seq == 0, 0.0, prev_rows[prev_rows.shape[0] - 1:, :])
    rowid = lax.broadcasted_iota(jnp.int32, p.shape, 0)
    shifted = jnp.where(rowid == 0, prev_last, pltpu.roll(p, 1, axis=0))
    xs = p + (shifted - p) * mu_ref[...]
    r = xs[:, 0:w]
    k = xs[:, w:2 * w]
    v = xs[:, 2 * w:3 * w]
    lora_in = xs[:, 3 * w:3 * w + LANES]
    gd = xs[:, 3 * w + LANES:]
    lane = lax.broadcasted_iota(jnp.int32, lora_in.shape, 1)
    lora_in = jnp.where(lane < lora_in.shape[1] // 2, jnp.tanh(lora_in), lora_in)
    lora = _dot(lora_in.astype(BF16), wlora_ref[...])
    w_log = -jnp.exp(_log_sigmoid(w0_ref[...] + lora[:, :w]) - 0.5)
    a = _sigmoid(a0_ref[...] + lora[:, w:])
    g_ref[...] = _dot(_sigmoid(gd).astype(BF16), gup_ref[...])

    bd = bd_ref[...]
    kk = k * kk_ref[...]
    kk = kk * lax.rsqrt(jnp.maximum(_dot_terms_rhs(kk * kk, bd, HEAD_SUM_TERMS), 1e-24))
    k2 = k * (1.0 + (a - 1.0) * ka_ref[...])
    bonus_ref[...] = _dot_terms_rhs(r * k2 * rk_ref[...], bd, HEAD_SUM_TERMS) * v

    ti = lax.broadcasted_iota(jnp.int32, (tm, tm), 0)
    si = lax.broadcasted_iota(jnp.int32, (tm, tm), 1)
    same = (ti // CHUNK) == (si // CHUNK)
    lower = jnp.where(same & (si <= ti), 1.0, 0.0).astype(BF16)
    upper = jnp.where(same & (si > ti), 1.0, 0.0).astype(BF16)
    w_terms = _bf16_terms(w_log, DECAY_SUM_TERMS)
    cum = sum(_dot(lower, t) for t in w_terms)
    rem = sum(_dot(upper, t) for t in w_terms)
    n_sel = gam_ref.shape[1]
    ci = lax.broadcasted_iota(jnp.int32, (n_sel, tm), 0)
    sj = lax.broadcasted_iota(jnp.int32, (n_sel, tm), 1)
    sel = jnp.where(ci == sj // CHUNK, 1.0, 0.0).astype(BF16)
    gam_ref[0] = jnp.exp(sum(_dot(sel, t) for t in w_terms))

    e_prev = jnp.exp(cum - w_log)
    e_cum = jnp.exp(cum)
    e_neg = jnp.exp(-cum)
    e_rem = jnp.exp(rem)
    b = kk * a
    at_ref[...] = (-kk * e_prev).astype(BF16)
    rt_ref[...] = (r * e_cum).astype(BF16)
    bt_ref[...] = (b * e_neg).astype(BF16)
    kt_ref[...] = (k2 * e_neg).astype(BF16)
    bh_ref[...] = (b * e_rem).astype(BF16)
    kh_ref[...] = (k2 * e_rem).astype(BF16)
    v_ref[...] = v.astype(BF16)


def _rwkv_prep(prw, mu, w0, a0, k_k, k_a, r_k, wlora, gup, bd, *, seq, tm=256):
    t, cols = prw.shape
    w = w0.shape[1]
    n_tiles = t // tm
    n_sel = 8
    row = lambda width: pl.BlockSpec((tm, width), lambda i: (i, 0))
    tok_bf = jax.ShapeDtypeStruct((t, w), BF16)
    tok_f = jax.ShapeDtypeStruct((t, w), F32)
    prev_rows = 8
    return pl.pallas_call(
        functools.partial(_rwkv_prep_kernel, tiles_per_seq=seq // tm),
        grid=(n_tiles,),
        in_specs=[row(cols),
                  pl.BlockSpec((prev_rows, cols), lambda i: (jnp.maximum(i * (tm // prev_rows) - 1, 0), 0)),
                  _const_spec(mu.shape), _const_spec(w0.shape), _const_spec(a0.shape),
                  _const_spec(k_k.shape), _const_spec(k_a.shape), _const_spec(r_k.shape),
                  _const_spec(wlora.shape), _const_spec(gup.shape), _const_spec(bd.shape)],
        out_specs=[row(w)] * 7 + [pl.BlockSpec((1, n_sel, w), lambda i: (i, 0, 0)), row(w), row(w)],
        out_shape=[tok_bf] * 7 + [jax.ShapeDtypeStruct((n_tiles, n_sel, w), F32), tok_f, tok_f],
        compiler_params=_params(1),
        name="rwkv_prep",
    )(prw, prw, mu, w0, a0, k_k, k_a, r_k, wlora, gup, bd)


def _rwkv_scan_kernel(at_ref, rt_ref, bt_ref, kt_ref, bh_ref, kh_ref, v_ref, gam_ref, bonus_ref, g_ref,
                      gng_ref, gnb_ref, bdavg_ref, o_ref, state_ref, y_ref):
    @pl.when(pl.program_id(1) == 0)
    def _():
        state_ref[...] = jnp.zeros_like(state_ref)

    n_t, w = v_ref.shape
    n_pairs = w // LANES
    n_chunks = n_t // CHUNK
    lane = lax.broadcasted_iota(jnp.int32, (CHUNK, LANES), 1)
    first = lane < HEAD_DIM
    ri = lax.broadcasted_iota(jnp.int32, (PAIR, PAIR), 0)
    ci = lax.broadcasted_iota(jnp.int32, (PAIR, PAIR), 1)
    strict = ri > ci
    eye = jnp.where(ri == ci, 1.0, 0.0)
    incl2 = (lax.broadcasted_iota(jnp.int32, (PAIR, 2 * PAIR), 0)
             >= (lax.broadcasted_iota(jnp.int32, (PAIR, 2 * PAIR), 1) & (PAIR - 1)))

    def stacked(ref, c, p):
        x = ref[c * CHUNK:(c + 1) * CHUNK, p * LANES:(p + 1) * LANES]
        z = jnp.zeros_like(x)
        return jnp.concatenate([jnp.where(first, x, z), jnp.where(first, z, x)], axis=0)

    pairs = range(n_pairs)
    states = [state_ref[p] for p in pairs]
    p_mat, d_mat, m_t, c_t = {}, {}, {}, {}

    def chain_step(c):
        h_bf = [states[p].astype(BF16) for p in pairs]
        for p in pairs:
            y_s = _dot_nt(p_mat.pop((c, p)), h_bf[p]) + d_mat.pop((c, p))
            gam = gam_ref[0, c:c + 1, p * LANES:(p + 1) * LANES]
            states[p] = states[p] * gam + _dot(h_bf[p], m_t.pop((c, p))) + c_t.pop((c, p))
            y_ref[c * CHUNK:(c + 1) * CHUNK, p * LANES:(p + 1) * LANES] = y_s[:CHUNK] + y_s[CHUNK:]

    def chunk_stages(cps):
        gram, s_inv, q_pow, a_ak, a_rbk, x_rhs, sx = {}, {}, {}, {}, {}, {}, {}
        for cp in cps:
            lhs = jnp.concatenate([stacked(at_ref, *cp), stacked(rt_ref, *cp)], axis=0)
            rhs = jnp.concatenate([stacked(bt_ref, *cp), stacked(kt_ref, *cp)], axis=0)
            gram[cp] = _dot_nt(lhs, rhs)
        yield
        for cp in cps:
            g_all = gram.pop(cp)
            n_ab = jnp.where(strict, g_all[:PAIR, :PAIR], 0.0)
            a_ak[cp] = jnp.where(strict, g_all[:PAIR, PAIR:], 0.0).astype(BF16)
            a_rbk[cp] = jnp.where(incl2, g_all[PAIR:, :], 0.0).astype(BF16)
            s_inv[cp] = eye + n_ab
            q_pow[cp] = n_ab.astype(BF16)
        for cp in cps:
            q_pow[cp] = _dot(q_pow[cp], q_pow[cp]).astype(BF16)
            x_rhs[cp] = _dot(a_ak.pop(cp), stacked(v_ref, *cp))
        yield
        span = 2
        while 2 * span < CHUNK:
            for cp in cps:
                both = _dot(q_pow[cp], jnp.concatenate([s_inv[cp].astype(BF16), q_pow[cp]], axis=1))
                s_inv[cp] = s_inv[cp] + both[:, :PAIR]
                q_pow[cp] = both[:, PAIR:].astype(BF16)
            span *= 2
            yield
        for cp in cps:
            s_inv[cp] = (s_inv[cp] + _dot(q_pow.pop(cp), s_inv[cp].astype(BF16))).astype(BF16)
        yield
        for cp in cps:
            x = jnp.concatenate([stacked(at_ref, *cp), x_rhs.pop(cp).astype(BF16)], axis=1)
            sx[cp] = _dot(s_inv.pop(cp), x).astype(BF16)
        yield
        for cp in cps:
            zero = jnp.zeros((PAIR, PAIR), BF16)
            lower_rows = jnp.concatenate([zero, stacked(v_ref, *cp)], axis=1)
            full = jnp.concatenate([sx.pop(cp), lower_rows], axis=0)
            pd = _dot(a_rbk.pop(cp), full)
            p_mat[cp] = (pd[:, :PAIR] + stacked(rt_ref, *cp).astype(F32)).astype(BF16)
            d_mat[cp] = pd[:, PAIR:]
            mc = _dot_tn(full, jnp.concatenate([stacked(bh_ref, *cp), stacked(kh_ref, *cp)], axis=0))
            m_t[cp] = mc[:PAIR].astype(BF16)
            c_t[cp] = mc[PAIR:]
        yield

    group = max(1, n_chunks // 2)
    pending = []
    for c0 in range(0, n_chunks, group):
        chunks = range(c0, min(c0 + group, n_chunks))
        for _ in chunk_stages([(c, p) for c in chunks for p in pairs]):
            if pending:
                chain_step(pending.pop(0))
        pending += list(chunks)
    for c in pending:
        chain_step(c)
    for p in pairs:
        state_ref[p] = states[p]

    y = y_ref[...]
    bdavg = bdavg_ref[...]
    dev = y - _dot_terms_rhs(y, bdavg, HEAD_SUM_TERMS + 1)
    var = _dot_terms_rhs(dev * dev, bdavg, HEAD_SUM_TERMS)
    yn = dev * lax.rsqrt(var + GN_EPS) * gng_ref[...] + gnb_ref[...]
    o_ref[...] = ((yn + bonus_ref[...]) * g_ref[...]).astype(o_ref.dtype)


def _rwkv_scan(at, rt, bt, kt, bh, kh, v, gam, bonus, g, gn_g, gn_b, bdavg, *, batch, seq):
    t, w = v.shape
    n_sel = gam.shape[1]
    tl = t // gam.shape[0]
    nt = seq // tl
    tok = pl.BlockSpec((tl, w), lambda b, i: (b * nt + i, 0))
    return pl.pallas_call(
        _rwkv_scan_kernel,
        grid=(batch, nt),
        in_specs=[tok] * 7 + [pl.BlockSpec((1, n_sel, w), lambda b, i: (b * nt + i, 0, 0)), tok, tok,
                              _const_spec(gn_g.shape), _const_spec(gn_b.shape), _const_spec(bdavg.shape)],
        out_specs=tok,
        out_shape=jax.ShapeDtypeStruct((t, w), BF16),
        scratch_shapes=[pltpu.VMEM((w // LANES, LANES, LANES), F32), pltpu.VMEM((tl, w), F32)],
        compiler_params=_params(2),
        name="rwkv_scan",
    )(at, rt, bt, kt, bh, kh, v, gam, bonus, g, gn_g, gn_b, bdavg)


def _merge_kernel(x_ref, fox_ref, rw_ref, mem_ref, gate_ref, wf_ref, wr_ref, wm_ref, wo_ref, g_ref, o_ref):
    d = x_ref.shape[1]
    merged = (gate_ref[:, 0:d].astype(F32) * _dot(fox_ref[...], wf_ref[...])
              + gate_ref[:, d:2 * d].astype(F32) * _dot(rw_ref[...], wr_ref[...])
              + gate_ref[:, 2 * d:3 * d].astype(F32) * _dot(mem_ref[...], wm_ref[...]))
    y = _dot(merged.astype(BF16), wo_ref[...])
    o_ref[...] = x_ref[...] + _rms(y, g_ref[...])


def _merge(x2, fox, rw, mem, gates, wf, wr, wm, wo, g, tm=512):
    t, d = x2.shape
    row = lambda width: pl.BlockSpec((tm, width), lambda i: (i, 0))
    return pl.pallas_call(
        _merge_kernel,
        grid=(t // tm,),
        in_specs=[row(d), row(fox.shape[1]), row(rw.shape[1]), row(mem.shape[1]), row(gates.shape[1]),
                  _const_spec(wf.shape), _const_spec(wr.shape), _const_spec(wm.shape), _const_spec(wo.shape),
                  _const_spec(g.shape)],
        out_specs=row(d),
        out_shape=jax.ShapeDtypeStruct((t, d), F32),
        compiler_params=_params(1),
        name="merge_out",
    )(x2, fox, rw, mem, gates, wf, wr, wm, wo, g)


def _ffn_kernel(h_ref, g1_ref, wg_ref, wu_ref, wd_ref, g2_ref, o_ref, *, ff_chunk):
    h = h_ref[...]
    u = _rms(h, g1_ref[...]).astype(BF16)
    d_ff = wg_ref.shape[1]
    acc = jnp.zeros(h.shape, F32)
    for lo in range(0, d_ff, ff_chunk):
        hi = min(lo + ff_chunk, d_ff)
        gt = _dot(u, wg_ref[:, lo:hi])
        up = _dot(u, wu_ref[:, lo:hi])
        act = (gt * _sigmoid(gt) * up).astype(BF16)
        acc = acc + _dot(act, wd_ref[lo:hi, :])
    o_ref[...] = h + _rms(acc, g2_ref[...])


def _ffn(h2, g1, wg, wu, wd, g2, tm=512, ff_chunk=1024):
    t, d = h2.shape
    row = pl.BlockSpec((tm, d), lambda i: (i, 0))
    return pl.pallas_call(
        functools.partial(_ffn_kernel, ff_chunk=ff_chunk),
        grid=(t // tm,),
        in_specs=[row, _const_spec(g1.shape), _const_spec(wg.shape), _const_spec(wu.shape),
                  _const_spec(wd.shape), _const_spec(g2.shape)],
        out_specs=row,
        out_shape=jax.ShapeDtypeStruct((t, d), F32),
        compiler_params=_params(1),
        name="ffn",
    )(h2, g1, wg, wu, wd, g2)


def _block_diag_ones(width, block, value):
    idx = jnp.arange(width) // block
    return jnp.where(idx[:, None] == idx[None, :], value, 0.0).astype(BF16)


def kernel(x, mem, pre1_g, post1_g, pre2_g, post2_g, mem_norm_g, w_in, fox_f_bias, rwkv_mu, rwkv_w0,
           rwkv_w_up, rwkv_a0, rwkv_a_up, rwkv_g_up, rwkv_k_k, rwkv_k_a, rwkv_r_k, rwkv_gn_g, rwkv_gn_b,
           w_mem_kv, w_fox_out, w_rwkv_out, w_mem_out, w_o, w_ffn_gate, w_ffn_up, w_ffn_down):
    batch, seq, d = x.shape
    mem_len = mem.shape[1]
    depth = w_in.shape[0]
    fox_heads = fox_f_bias.shape[1]
    fox_w = fox_heads * HEAD_DIM
    rw_w = rwkv_w0.shape[1]
    rw_cols = rwkv_mu.shape[1]
    mem_w = w_mem_kv.shape[2] // 2
    dec_lora = rwkv_w_up.shape[1]
    aaa_lora = rwkv_a_up.shape[1]
    assert dec_lora + aaa_lora == LANES and dec_lora == aaa_lora
    assert fox_heads <= SUBLANES and fox_w % LANES == 0 and rw_w % LANES == 0
    row2 = lambda v: v.reshape(1, -1)

    h = x.reshape(batch * seq, d)
    mem2 = mem.reshape(batch * mem_len, d)
    bd_ones = _block_diag_ones(rw_w, HEAD_DIM, 1.0)
    bd_avg = _block_diag_ones(rw_w, HEAD_DIM, 1.0 / HEAD_DIM)
    for l in range(depth):
        wi = w_in[l]
        o_ff = 3 * fox_w
        o_rw = o_ff + fox_heads
        o_mq = o_rw + rw_cols
        o_gt = o_mq + mem_w
        scale = HEAD_DIM ** -0.5 * LOG2E
        wqkv = jnp.concatenate([wi[:, :fox_w] * scale, wi[:, fox_w:o_ff]], axis=1).astype(BF16)
        wff = jnp.pad(wi[:, o_ff:o_rw], ((0, 0), (0, LANES - fox_heads))).astype(BF16)
        fbias = jnp.pad(row2(fox_f_bias[l]), ((0, 0), (0, LANES - fox_heads)))
        wrw = wi[:, o_rw:o_mq].astype(BF16)
        wmq = wi[:, o_mq:o_gt].astype(BF16)
        wgt = wi[:, o_gt:].astype(BF16)
        wlora = jnp.zeros((LANES, 2 * rw_w), F32)
        wlora = wlora.at[:dec_lora, :rw_w].set(rwkv_w_up[l]).at[dec_lora:, rw_w:].set(rwkv_a_up[l]).astype(BF16)

        qkv, crow, prw, mq, gates = _inproj(
            h, row2(pre1_g[l]), wqkv, wff, wrw, wmq, wgt, fbias,
            seq=seq, mem_scale=(mem_w // MEM_HEADS) ** -0.5)

        fox_out = _fox(qkv, crow, batch=batch, seq=seq)

        memkv = _memkv(mem2, row2(mem_norm_g[l]), w_mem_kv[l].astype(BF16))
        mem_out = _memattn(mq, memkv, batch=batch, seq=seq, mem_len=mem_len)

        prep = _rwkv_prep(prw, row2(rwkv_mu[l]), row2(rwkv_w0[l]), row2(rwkv_a0[l]), row2(rwkv_k_k[l]),
                          row2(rwkv_k_a[l]), row2(rwkv_r_k[l]), wlora, rwkv_g_up[l].astype(BF16), bd_ones,
                          seq=seq)
        rwkv_out = _rwkv_scan(*prep, row2(rwkv_gn_g[l]), row2(rwkv_gn_b[l]), bd_avg, batch=batch, seq=seq)

        h = _merge(h, fox_out, rwkv_out, mem_out, gates, w_fox_out[l].astype(BF16),
                   w_rwkv_out[l].astype(BF16), w_mem_out[l].astype(BF16), w_o[l].astype(BF16),
                   row2(post1_g[l]))
        h = _ffn(h, row2(pre2_g[l]), w_ffn_gate[l].astype(BF16), w_ffn_up[l].astype(BF16),
                 w_ffn_down[l].astype(BF16), row2(post2_g[l]))
    return h.reshape(batch, seq, d).astype(x.dtype)
```

```python
import functools

import jax
import jax.numpy as jnp
from jax import lax
from jax.experimental import pallas as pl
from jax.experimental.pallas import tpu as pltpu

F32 = jnp.float32
BF16 = jnp.bfloat16

NORM_EPS = 1e-6
GN_EPS = 64e-5
HEAD_DIM = 64
LANES = 128
SUBLANES = 8
LOG2E = 1.4426950408889634
MEM_HEADS = 4
CHUNK = 64
PAIR = 2 * CHUNK
F32_TERMS = 3
DECAY_SUM_TERMS = 2
HEAD_SUM_TERMS = 1
SUM_BLOCK = 256
INPROJ_ROWS = 256
SCAN_ROWS = 256
NEG_BIG = -1e30
VMEM_LIMIT = 56 * 1024 * 1024


def _dot(a, b):
    return jnp.dot(a, b, preferred_element_type=F32)


def _dot_nt(a, b):
    return lax.dot_general(a, b, (((1,), (1,)), ((), ())), preferred_element_type=F32)


def _dot_tn(a, b):
    return lax.dot_general(a, b, (((0,), (0,)), ((), ())), preferred_element_type=F32)


def _bf16_terms(x, n):
    terms = []
    for _ in range(n - 1):
        t = x.astype(BF16)
        terms.append(t)
        x = x - t.astype(F32)
    return terms + [x.astype(BF16)]


def _dot_terms_lhs(m_bf16, x, n):
    return sum(_dot(m_bf16, t) for t in _bf16_terms(x, n))


def _dot_terms_rhs(x, m_bf16, n):
    return sum(_dot(t, m_bf16) for t in _bf16_terms(x, n))


def _rms(xf, g):
    return xf * lax.rsqrt(jnp.mean(xf * xf, axis=-1, keepdims=True) + NORM_EPS) * g


def _sigmoid(x):
    return 1.0 / (1.0 + jnp.exp(-x))


def _log_sigmoid(x):
    return jnp.minimum(x, 0.0) - jnp.log(1.0 + jnp.exp(-jnp.abs(x)))


def _const_spec(shape):
    nd = len(shape)
    return pl.BlockSpec(shape, lambda *_: (0,) * nd, pipeline_mode=pl.Buffered(1))


def _params(n_axes):
    return pltpu.CompilerParams(dimension_semantics=("arbitrary",) * n_axes,
                                vmem_limit_bytes=VMEM_LIMIT)


def _inproj_kernel(x_ref, g_ref, w_ref, fb_ref, mu_ref, w0_ref, a0_ref, kk_ref, ka_ref, rk_ref,
                   wlora_ref, gup_ref, bd_ref,
                   qkv_ref, crow_ref, mq_ref, gate_ref,
                   at_ref, rt_ref, bt_ref, kt_ref, bh_ref, kh_ref, v_ref, gam_ref, bonus_ref, g_out_ref,
                   carry_ref, prev_ref, *, tiles_per_seq, mem_scale, cols):
    i = pl.program_id(0)
    tm = x_ref.shape[0]
    w = v_ref.shape[1]

    @pl.when(i % tiles_per_seq == 0)
    def _():
        carry_ref[...] = jnp.zeros_like(carry_ref)
        prev_ref[...] = jnp.zeros_like(prev_ref)

    def proj(name, lo=0, hi=None):
        start, stop = cols[name]
        return _dot(u, w_ref[:, start + lo:(stop if hi is None else start + hi)])

    u = _rms(x_ref[...], g_ref[...]).astype(BF16)

    p = proj("rwkv")
    rowid = lax.broadcasted_iota(jnp.int32, p.shape, 0)
    shifted = jnp.where(rowid == 0, prev_ref[...], pltpu.roll(p, 1, axis=0))
    prev_ref[...] = p[tm - 1:tm, :]
    xs = p + (shifted - p) * mu_ref[...]
    r = xs[:, 0:w]
    k = xs[:, w:2 * w]
    v = xs[:, 2 * w:3 * w]
    lora_in = xs[:, 3 * w:3 * w + LANES]
    gd = xs[:, 3 * w + LANES:]
    lane = lax.broadcasted_iota(jnp.int32, lora_in.shape, 1)
    lora_in = jnp.where(lane < lora_in.shape[1] // 2, jnp.tanh(lora_in), lora_in)
    qkv_ref[...] = proj("qkv").astype(BF16)
    lora = _dot(lora_in.astype(BF16), wlora_ref[...])
    g_out_ref[...] = _dot(_sigmoid(gd).astype(BF16), gup_ref[...])
    w_log = -jnp.exp(_log_sigmoid(w0_ref[...] + lora[:, :w]) - 0.5)
    a = _sigmoid(a0_ref[...] + lora[:, w:])

    d = gate_ref.shape[1] // 3
    gate_ref[:, 0:d] = _sigmoid(proj("gate", 0, d)).astype(BF16)

    bd = bd_ref[...]
    kk = k * kk_ref[...]
    kk = kk * lax.rsqrt(jnp.maximum(_dot_terms_rhs(kk * kk, bd, HEAD_SUM_TERMS), 1e-24))
    k2 = k * (1.0 + (a - 1.0) * ka_ref[...])
    bonus_ref[...] = _dot_terms_rhs(r * k2 * rk_ref[...], bd, HEAD_SUM_TERMS) * v

    sb = min(tm, SUM_BLOCK)
    ti = lax.broadcasted_iota(jnp.int32, (sb, sb), 0)
    si = lax.broadcasted_iota(jnp.int32, (sb, sb), 1)
    same = (ti // CHUNK) == (si // CHUNK)
    lower = jnp.where(same & (si <= ti), 1.0, 0.0).astype(BF16)
    upper = jnp.where(same & (si > ti), 1.0, 0.0).astype(BF16)
    w_terms = _bf16_terms(w_log, DECAY_SUM_TERMS)
    blocks = [slice(s0, s0 + sb) for s0 in range(0, tm, sb)]
    cum = jnp.concatenate([sum(_dot(lower, t[blk]) for t in w_terms) for blk in blocks], axis=0)
    rem = jnp.concatenate([sum(_dot(upper, t[blk]) for t in w_terms) for blk in blocks], axis=0)
    n_sel = gam_ref.shape[1]
    ci = lax.broadcasted_iota(jnp.int32, (n_sel, tm), 0)
    sj = lax.broadcasted_iota(jnp.int32, (n_sel, tm), 1)
    sel = jnp.where(ci == sj // CHUNK, 1.0, 0.0).astype(BF16)
    gam_ref[0] = jnp.exp(sum(_dot(sel, t) for t in w_terms))

    gate_ref[:, d:2 * d] = _sigmoid(proj("gate", d, 2 * d)).astype(BF16)

    e_prev = jnp.exp(cum - w_log)
    e_cum = jnp.exp(cum)
    e_neg = jnp.exp(-cum)
    e_rem = jnp.exp(rem)
    b = kk * a
    at_ref[...] = (-kk * e_prev).astype(BF16)
    rt_ref[...] = (r * e_cum).astype(BF16)
    bt_ref[...] = (b * e_neg).astype(BF16)
    kt_ref[...] = (k2 * e_neg).astype(BF16)
    bh_ref[...] = (b * e_rem).astype(BF16)
    kh_ref[...] = (k2 * e_rem).astype(BF16)
    v_ref[...] = v.astype(BF16)

    gate_ref[:, 2 * d:3 * d] = _sigmoid(proj("gate", 2 * d, 3 * d)).astype(BF16)
    mq_ref[...] = (proj("memq") * mem_scale).astype(BF16)

    ls = _log_sigmoid(proj("ff") + fb_ref[...])
    tri = jnp.where(lax.broadcasted_iota(jnp.int32, (tm, tm), 0) >= lax.broadcasted_iota(jnp.int32, (tm, tm), 1),
                    1.0, 0.0).astype(BF16)
    c = _dot_terms_lhs(tri, ls, F32_TERMS) + carry_ref[...]
    carry_ref[...] = c[tm - 1:tm, :]
    crow_ref[...] = jnp.transpose(c * LOG2E)[:crow_ref.shape[0], :]


def _inproj(x2, g, w_all, cols, fbias, mu, w0, a0, k_k, k_a, r_k, wlora, gup, bd, *, seq, mem_scale,
            tm=INPROJ_ROWS):
    t, d = x2.shape
    w = w0.shape[1]
    n_tiles = t // tm
    width = lambda name: cols[name][1] - cols[name][0]
    tok = lambda n, dt: jax.ShapeDtypeStruct((t, n), dt)
    n_sel = max(SUBLANES, tm // CHUNK)
    outs = ([tok(width("qkv"), BF16), jax.ShapeDtypeStruct((SUBLANES, t), F32), tok(width("memq"), BF16),
             tok(width("gate"), BF16)] + [tok(w, BF16)] * 7
            + [jax.ShapeDtypeStruct((n_tiles, n_sel, w), F32), tok(w, F32), tok(w, F32)])
    row = lambda n: pl.BlockSpec((tm, n), lambda i: (i, 0))
    consts = (g, w_all, fbias, mu, w0, a0, k_k, k_a, r_k, wlora, gup, bd)
    gam_index = 11
    results = list(pl.pallas_call(
        functools.partial(_inproj_kernel, tiles_per_seq=seq // tm, mem_scale=mem_scale, cols=cols),
        grid=(n_tiles,),
        in_specs=[row(d)] + [_const_spec(c.shape) for c in consts],
        out_specs=([row(width("qkv")), pl.BlockSpec((SUBLANES, tm), lambda i: (0, i)), row(width("memq")),
                    row(width("gate"))] + [row(w)] * 7
                   + [pl.BlockSpec((1, n_sel, w), lambda i: (i, 0, 0)), row(w), row(w)]),
        out_shape=outs,
        scratch_shapes=[pltpu.VMEM((1, LANES), F32), pltpu.VMEM((1, cols["rwkv"][1] - cols["rwkv"][0]), F32)],
        compiler_params=_params(1),
        name="inproj",
    )(x2, *consts))
    results[gam_index] = results[gam_index][:, :tm // CHUNK]
    return results


def _fox_kernel(q_ref, k_ref, v_ref, crow_ref, o_ref, *, bk, rb, ahead):
    hp = pl.program_id(1)
    seq = q_ref.shape[0]
    first_q = lax.broadcasted_iota(jnp.int32, (rb, LANES), 1) < HEAD_DIM
    lower = (lax.broadcasted_iota(jnp.int32, (rb, rb), 0) >= lax.broadcasted_iota(jnp.int32, (rb, rb), 1))
    sub = lax.broadcasted_iota(jnp.int32, crow_ref.shape, 0)
    crow_all = crow_ref[...]
    crow = [jnp.sum(jnp.where(sub == 2 * hp + e, crow_all, 0.0), axis=0, keepdims=True) for e in range(2)]

    units = []
    for r in range(seq // rb):
        stop = (r + 1) * rb
        units += [(r, k0, min(bk, stop - k0)) for k0 in range(0, stop, bk)]
    units.sort(key=lambda u: (u[0] * rb // bk, u[1], u[0]))

    scores, state = {}, {}

    def issue_scores(u):
        r, k0, width = u
        q2 = q_ref[r * rb:(r + 1) * rb, :]
        zero = jnp.zeros_like(q2)
        kt = k_ref[k0:k0 + width, :]
        scores[u, 0] = _dot_nt(jnp.where(first_q, q2, zero), kt)
        scores[u, 1] = _dot_nt(jnp.where(first_q, zero, q2), kt)

    def finish(u):
        r, k0, width = u
        vt = v_ref[k0:k0 + width, :]
        one = jnp.ones_like(vt)
        first_k = lax.broadcasted_iota(jnp.int32, vt.shape, 1) < HEAD_DIM
        on_diagonal = k0 + width == (r + 1) * rb
        for e in range(2):
            v_aug = jnp.where(first_k, vt, one) if e == 0 else jnp.where(first_k, one, vt)
            s = scores.pop((u, e)) - crow[e][:, k0:k0 + width]
            if on_diagonal:
                tail = jnp.where(lower, s[:, width - rb:], NEG_BIG)
                s = tail if width == rb else jnp.concatenate([s[:, :width - rb], tail], axis=1)
            m_tile = jnp.max(s, axis=-1, keepdims=True)
            if k0 == 0:
                m_new = m_tile
                acc = _dot(jnp.exp2(s - m_new).astype(BF16), v_aug)
            else:
                m_old, acc_old = state[r, e]
                m_new = jnp.maximum(m_old, m_tile)
                acc = jnp.exp2(m_old - m_new) * acc_old + _dot(jnp.exp2(s - m_new).astype(BF16), v_aug)
            state[r, e] = (m_new, acc)
        if on_diagonal:
            a0, a1 = state.pop((r, 0))[1], state.pop((r, 1))[1]
            den = jnp.where(first_q, pltpu.roll(a0, HEAD_DIM, axis=1), pltpu.roll(a1, HEAD_DIM, axis=1))
            o_ref[r * rb:(r + 1) * rb, :] = (jnp.where(first_q, a0, a1) / den).astype(o_ref.dtype)

    for u in units[:ahead]:
        issue_scores(u)
    for n, u in enumerate(units):
        if n + ahead < len(units):
            issue_scores(units[n + ahead])
        finish(u)


def _fox(qkv, crow, *, batch, seq, bk=512, rb=128, ahead=2):
    t = qkv.shape[0]
    n_pairs = qkv.shape[1] // (3 * LANES)
    return pl.pallas_call(
        functools.partial(_fox_kernel, bk=bk, rb=rb, ahead=ahead),
        grid=(batch, n_pairs),
        in_specs=[
            pl.BlockSpec((seq, LANES), lambda b, h: (b, h)),
            pl.BlockSpec((seq, LANES), lambda b, h: (b, n_pairs + h)),
            pl.BlockSpec((seq, LANES), lambda b, h: (b, 2 * n_pairs + h)),
            pl.BlockSpec((crow.shape[0], seq), lambda b, h: (0, b)),
        ],
        out_specs=pl.BlockSpec((seq, LANES), lambda b, h: (b, h)),
        out_shape=jax.ShapeDtypeStruct((t, n_pairs * LANES), BF16),
        compiler_params=_params(2),
        name="fox_attention",
    )(qkv, qkv, qkv, crow)


def _memkv_kernel(m_ref, g_ref, w_ref, o_ref):
    o_ref[...] = _dot(_rms(m_ref[...], g_ref[...]).astype(BF16), w_ref[...]).astype(BF16)


def _memkv(mem2, g, w, tm=512):
    t, d = mem2.shape
    return pl.pallas_call(
        _memkv_kernel,
        grid=(t // tm,),
        in_specs=[pl.BlockSpec((tm, d), lambda i: (i, 0)), _const_spec(g.shape), _const_spec(w.shape)],
        out_specs=pl.BlockSpec((tm, w.shape[1]), lambda i: (i, 0)),
        out_shape=jax.ShapeDtypeStruct((t, w.shape[1]), BF16),
        compiler_params=_params(1),
        name="mem_kv",
    )(mem2, g, w)


def _memattn_kernel(q_ref, kv_ref, o_ref):
    w = q_ref.shape[1]
    hd = w // MEM_HEADS
    for h in range(MEM_HEADS):
        q = q_ref[:, h * hd:(h + 1) * hd]
        k = kv_ref[:, h * hd:(h + 1) * hd]
        v = kv_ref[:, w + h * hd:w + (h + 1) * hd]
        s = _dot_nt(q, k)
        p = jnp.exp(s - jnp.max(s, axis=-1, keepdims=True))
        l = jnp.sum(p, axis=-1, keepdims=True)
        o_ref[:, h * hd:(h + 1) * hd] = (_dot(p.astype(BF16), v) / l).astype(o_ref.dtype)


def _memattn(mq, memkv, *, batch, seq, mem_len, tq=512):
    t, w = mq.shape
    nq = seq // tq
    return pl.pallas_call(
        _memattn_kernel,
        grid=(batch, nq),
        in_specs=[pl.BlockSpec((tq, w), lambda b, i: (b * nq + i, 0)),
                  pl.BlockSpec((mem_len, 2 * w), lambda b, i: (b, 0))],
        out_specs=pl.BlockSpec((tq, w), lambda b, i: (b * nq + i, 0)),
        out_shape=jax.ShapeDtypeStruct((t, w), BF16),
        compiler_params=_params(2),
        name="mem_attention",
    )(mq, memkv)


def _rwkv_scan_kernel(at_ref, rt_ref, bt_ref, kt_ref, bh_ref, kh_ref, v_ref, gam_ref, bonus_ref, g_ref,
                      gng_ref, gnb_ref, bdavg_ref, o_ref, state_ref, y_ref):
    @pl.when(pl.program_id(1) == 0)
    def _():
        state_ref[...] = jnp.zeros_like(state_ref)

    n_t, w = v_ref.shape
    n_pairs = w // LANES
    n_chunks = n_t // CHUNK
    lane = lax.broadcasted_iota(jnp.int32, (CHUNK, LANES), 1)
    first = lane < HEAD_DIM
    ri = lax.broadcasted_iota(jnp.int32, (PAIR, PAIR), 0)
    ci = lax.broadcasted_iota(jnp.int32, (PAIR, PAIR), 1)
    strict = ri > ci
    eye = jnp.where(ri == ci, 1.0, 0.0)
    incl2 = (lax.broadcasted_iota(jnp.int32, (PAIR, 2 * PAIR), 0)
             >= (lax.broadcasted_iota(jnp.int32, (PAIR, 2 * PAIR), 1) & (PAIR - 1)))

    def stacked(ref, c, p):
        x = ref[c * CHUNK:(c + 1) * CHUNK, p * LANES:(p + 1) * LANES]
        z = jnp.zeros_like(x)
        return jnp.concatenate([jnp.where(first, x, z), jnp.where(first, z, x)], axis=0)

    pairs = range(n_pairs)
    states = [state_ref[p] for p in pairs]
    p_mat, d_mat, m_t, c_t = {}, {}, {}, {}

    def chain_step(c):
        h_bf = [states[p].astype(BF16) for p in pairs]
        for p in pairs:
            y_s = _dot_nt(p_mat.pop((c, p)), h_bf[p]) + d_mat.pop((c, p))
            gam = gam_ref[0, c:c + 1, p * LANES:(p + 1) * LANES]
            states[p] = states[p] * gam + _dot(h_bf[p], m_t.pop((c, p))) + c_t.pop((c, p))
            y_ref[c * CHUNK:(c + 1) * CHUNK, p * LANES:(p + 1) * LANES] = y_s[:CHUNK] + y_s[CHUNK:]

    def chunk_stages(cps):
        gram, s_inv, q_pow, a_ak, a_rbk, x_rhs, sx = {}, {}, {}, {}, {}, {}, {}
        for cp in cps:
            lhs = jnp.concatenate([stacked(at_ref, *cp), stacked(rt_ref, *cp)], axis=0)
            rhs = jnp.concatenate([stacked(bt_ref, *cp), stacked(kt_ref, *cp)], axis=0)
            gram[cp] = _dot_nt(lhs, rhs)
        yield
        for cp in cps:
            g_all = gram.pop(cp)
            n_ab = jnp.where(strict, g_all[:PAIR, :PAIR], 0.0)
            a_ak[cp] = jnp.where(strict, g_all[:PAIR, PAIR:], 0.0).astype(BF16)
            a_rbk[cp] = jnp.where(incl2, g_all[PAIR:, :], 0.0).astype(BF16)
            s_inv[cp] = eye + n_ab
            q_pow[cp] = n_ab.astype(BF16)
        for cp in cps:
            q_pow[cp] = _dot(q_pow[cp], q_pow[cp]).astype(BF16)
            x_rhs[cp] = _dot(a_ak.pop(cp), stacked(v_ref, *cp))
        yield
        span = 2
        while 2 * span < CHUNK:
            for cp in cps:
                both = _dot(q_pow[cp], jnp.concatenate([s_inv[cp].astype(BF16), q_pow[cp]], axis=1))
                s_inv[cp] = s_inv[cp] + both[:, :PAIR]
                q_pow[cp] = both[:, PAIR:].astype(BF16)
            span *= 2
            yield
        for cp in cps:
            s_inv[cp] = (s_inv[cp] + _dot(q_pow.pop(cp), s_inv[cp].astype(BF16))).astype(BF16)
        yield
        for cp in cps:
            x = jnp.concatenate([stacked(at_ref, *cp), x_rhs.pop(cp).astype(BF16)], axis=1)
            sx[cp] = _dot(s_inv.pop(cp), x).astype(BF16)
        yield
        for cp in cps:
            zero = jnp.zeros((PAIR, PAIR), BF16)
            lower_rows = jnp.concatenate([zero, stacked(v_ref, *cp)], axis=1)
            full = jnp.concatenate([sx.pop(cp), lower_rows], axis=0)
            pd = _dot(a_rbk.pop(cp), full)
            p_mat[cp] = (pd[:, :PAIR] + stacked(rt_ref, *cp).astype(F32)).astype(BF16)
            d_mat[cp] = pd[:, PAIR:]
            mc = _dot_tn(full, jnp.concatenate([stacked(bh_ref, *cp), stacked(kh_ref, *cp)], axis=0))
            m_t[cp] = mc[:PAIR].astype(BF16)
            c_t[cp] = mc[PAIR:]
        yield

    group = max(1, n_chunks // 2)
    pending = []
    for c0 in range(0, n_chunks, group):
        chunks = range(c0, min(c0 + group, n_chunks))
        for _ in chunk_stages([(c, p) for c in chunks for p in pairs]):
            if pending:
                chain_step(pending.pop(0))
        pending += list(chunks)
    for c in pending:
        chain_step(c)
    for p in pairs:
        state_ref[p] = states[p]

    y = y_ref[...]
    bdavg = bdavg_ref[...]
    dev = y - _dot_terms_rhs(y, bdavg, HEAD_SUM_TERMS + 1)
    var = _dot_terms_rhs(dev * dev, bdavg, HEAD_SUM_TERMS)
    yn = dev * lax.rsqrt(var + GN_EPS) * gng_ref[...] + gnb_ref[...]
    o_ref[...] = ((yn + bonus_ref[...]) * g_ref[...]).astype(o_ref.dtype)


def _rwkv_scan(at, rt, bt, kt, bh, kh, v, gam, bonus, g, gn_g, gn_b, bdavg, *, batch, seq, tl=SCAN_ROWS):
    t, w = v.shape
    n_sel = tl // CHUNK
    gam = gam.reshape(t // tl, n_sel, w)
    nt = seq // tl
    tok = pl.BlockSpec((tl, w), lambda b, i: (b * nt + i, 0))
    return pl.pallas_call(
        _rwkv_scan_kernel,
        grid=(batch, nt),
        in_specs=[tok] * 7 + [pl.BlockSpec((1, n_sel, w), lambda b, i: (b * nt + i, 0, 0)), tok, tok,
                              _const_spec(gn_g.shape), _const_spec(gn_b.shape), _const_spec(bdavg.shape)],
        out_specs=tok,
        out_shape=jax.ShapeDtypeStruct((t, w), BF16),
        scratch_shapes=[pltpu.VMEM((w // LANES, LANES, LANES), F32), pltpu.VMEM((tl, w), F32)],
        compiler_params=_params(2),
        name="rwkv_scan",
    )(at, rt, bt, kt, bh, kh, v, gam, bonus, g, gn_g, gn_b, bdavg)


def _merge_kernel(x_ref, fox_ref, rw_ref, mem_ref, gate_ref, wf_ref, wr_ref, wm_ref, wo_ref, g_ref, o_ref):
    d = x_ref.shape[1]
    merged = (gate_ref[:, 0:d].astype(F32) * _dot(fox_ref[...], wf_ref[...])
              + gate_ref[:, d:2 * d].astype(F32) * _dot(rw_ref[...], wr_ref[...])
              + gate_ref[:, 2 * d:3 * d].astype(F32) * _dot(mem_ref[...], wm_ref[...]))
    y = _dot(merged.astype(BF16), wo_ref[...])
    o_ref[...] = x_ref[...] + _rms(y, g_ref[...])


def _merge(x2, fox, rw, mem, gates, wf, wr, wm, wo, g, tm=1024):
    t, d = x2.shape
    row = lambda width: pl.BlockSpec((tm, width), lambda i: (i, 0))
    return pl.pallas_call(
        _merge_kernel,
        grid=(t // tm,),
        in_specs=[row(d), row(fox.shape[1]), row(rw.shape[1]), row(mem.shape[1]), row(gates.shape[1]),
                  _const_spec(wf.shape), _const_spec(wr.shape), _const_spec(wm.shape), _const_spec(wo.shape),
                  _const_spec(g.shape)],
        out_specs=row(d),
        out_shape=jax.ShapeDtypeStruct((t, d), F32),
        compiler_params=_params(1),
        name="merge_out",
    )(x2, fox, rw, mem, gates, wf, wr, wm, wo, g)


def _ffn_kernel(h_ref, g1_ref, wg_ref, wu_ref, wd_ref, g2_ref, o_ref, *, ff_chunk):
    h = h_ref[...]
    u = _rms(h, g1_ref[...]).astype(BF16)
    d_ff = wg_ref.shape[1]
    acc = jnp.zeros(h.shape, F32)
    for lo in range(0, d_ff, ff_chunk):
        hi = min(lo + ff_chunk, d_ff)
        gt = _dot(u, wg_ref[:, lo:hi])
        up = _dot(u, wu_ref[:, lo:hi])
        act = (gt * _sigmoid(gt) * up).astype(BF16)
        acc = acc + _dot(act, wd_ref[lo:hi, :])
    o_ref[...] = h + _rms(acc, g2_ref[...])


def _ffn(h2, g1, wg, wu, wd, g2, tm=1024, ff_chunk=1024):
    t, d = h2.shape
    row = pl.BlockSpec((tm, d), lambda i: (i, 0))
    return pl.pallas_call(
        functools.partial(_ffn_kernel, ff_chunk=ff_chunk),
        grid=(t // tm,),
        in_specs=[row, _const_spec(g1.shape), _const_spec(wg.shape), _const_spec(wu.shape),
                  _const_spec(wd.shape), _const_spec(g2.shape)],
        out_specs=row,
        out_shape=jax.ShapeDtypeStruct((t, d), F32),
        compiler_params=_params(1),
        name="ffn",
    )(h2, g1, wg, wu, wd, g2)


def _block_diag_ones(width, block, value):
    idx = jnp.arange(width) // block
    return jnp.where(idx[:, None] == idx[None, :], value, 0.0).astype(BF16)


def kernel(x, mem, pre1_g, post1_g, pre2_g, post2_g, mem_norm_g, w_in, fox_f_bias, rwkv_mu, rwkv_w0,
           rwkv_w_up, rwkv_a0, rwkv_a_up, rwkv_g_up, rwkv_k_k, rwkv_k_a, rwkv_r_k, rwkv_gn_g, rwkv_gn_b,
           w_mem_kv, w_fox_out, w_rwkv_out, w_mem_out, w_o, w_ffn_gate, w_ffn_up, w_ffn_down):
    batch, seq, d = x.shape
    mem_len = mem.shape[1]
    depth = w_in.shape[0]
    fox_heads = fox_f_bias.shape[1]
    fox_w = fox_heads * HEAD_DIM
    rw_w = rwkv_w0.shape[1]
    rw_cols = rwkv_mu.shape[1]
    mem_w = w_mem_kv.shape[2] // 2
    dec_lora = rwkv_w_up.shape[1]
    aaa_lora = rwkv_a_up.shape[1]
    assert dec_lora + aaa_lora == LANES and dec_lora == aaa_lora
    assert fox_heads <= SUBLANES and fox_w % LANES == 0 and rw_w % LANES == 0
    row2 = lambda v: v.reshape(1, -1)

    h = x.reshape(batch * seq, d)
    mem2 = mem.reshape(batch * mem_len, d)
    bd_ones = _block_diag_ones(rw_w, HEAD_DIM, 1.0)
    bd_avg = _block_diag_ones(rw_w, HEAD_DIM, 1.0 / HEAD_DIM)
    for l in range(depth):
        wi = w_in[l]
        o_ff = 3 * fox_w
        o_rw = o_ff + fox_heads
        o_mq = o_rw + rw_cols
        o_gt = o_mq + mem_w
        scale = HEAD_DIM ** -0.5 * LOG2E
        sections = [("qkv", jnp.concatenate([wi[:, :fox_w] * scale, wi[:, fox_w:o_ff]], axis=1)),
                    ("ff", jnp.pad(wi[:, o_ff:o_rw], ((0, 0), (0, LANES - fox_heads)))),
                    ("rwkv", wi[:, o_rw:o_mq]), ("memq", wi[:, o_mq:o_gt]), ("gate", wi[:, o_gt:])]
        w_all = jnp.concatenate([sec for _, sec in sections], axis=1).astype(BF16)
        cols, start = {}, 0
        for name, sec in sections:
            assert sec.shape[1] % LANES == 0
            cols[name] = (start, start + sec.shape[1])
            start += sec.shape[1]
        fbias = jnp.pad(row2(fox_f_bias[l]), ((0, 0), (0, LANES - fox_heads)))
        wlora = jnp.zeros((LANES, 2 * rw_w), F32)
        wlora = wlora.at[:dec_lora, :rw_w].set(rwkv_w_up[l]).at[dec_lora:, rw_w:].set(rwkv_a_up[l]).astype(BF16)

        qkv, crow, mq, gates, *scan_in = _inproj(
            h, row2(pre1_g[l]), w_all, cols, fbias, row2(rwkv_mu[l]), row2(rwkv_w0[l]), row2(rwkv_a0[l]),
            row2(rwkv_k_k[l]), row2(rwkv_k_a[l]), row2(rwkv_r_k[l]), wlora, rwkv_g_up[l].astype(BF16), bd_ones,
            seq=seq, mem_scale=(mem_w // MEM_HEADS) ** -0.5)

        fox_out = _fox(qkv, crow, batch=batch, seq=seq)

        memkv = _memkv(mem2, row2(mem_norm_g[l]), w_mem_kv[l].astype(BF16))
        mem_out = _memattn(mq, memkv, batch=batch, seq=seq, mem_len=mem_len)

        rwkv_out = _rwkv_scan(*scan_in, row2(rwkv_gn_g[l]), row2(rwkv_gn_b[l]), bd_avg, batch=batch, seq=seq)

        h = _merge(h, fox_out, rwkv_out, mem_out, gates, w_fox_out[l].astype(BF16),
                   w_rwkv_out[l].astype(BF16), w_mem_out[l].astype(BF16), w_o[l].astype(BF16),
                   row2(post1_g[l]))
        h = _ffn(h, row2(pre2_g[l]), w_ffn_gate[l].astype(BF16), w_ffn_up[l].astype(BF16),
                 w_ffn_down[l].astype(BF16), row2(post2_g[l]))
    return h.reshape(batch, seq, d).astype(x.dtype)
```

```python
import functools

import jax
import jax.numpy as jnp
from jax import lax
from jax.experimental import pallas as pl
from jax.experimental.pallas import tpu as pltpu

F32 = jnp.float32
BF16 = jnp.bfloat16

NORM_EPS = 1e-6
GN_EPS = 64e-5
HEAD_DIM = 64
LANES = 128
SUBLANES = 8
LOG2E = 1.4426950408889634
MEM_HEADS = 4
CHUNK = 64
PAIR = 2 * CHUNK
F32_TERMS = 3
DECAY_SUM_TERMS = 2
HEAD_SUM_TERMS = 1
SUM_BLOCK = 256
INPROJ_ROWS = 256
SCAN_ROWS = 512
NEG_BIG = -1e30
VMEM_LIMIT = 56 * 1024 * 1024


def _dot(a, b):
    return jnp.dot(a, b, preferred_element_type=F32)


def _dot_nt(a, b):
    return lax.dot_general(a, b, (((1,), (1,)), ((), ())), preferred_element_type=F32)


def _dot_tn(a, b):
    return lax.dot_general(a, b, (((0,), (0,)), ((), ())), preferred_element_type=F32)


def _bf16_terms(x, n):
    terms = []
    for _ in range(n - 1):
        t = x.astype(BF16)
        terms.append(t)
        x = x - t.astype(F32)
    return terms + [x.astype(BF16)]


def _dot_terms_lhs(m_bf16, x, n):
    return sum(_dot(m_bf16, t) for t in _bf16_terms(x, n))


def _dot_terms_rhs(x, m_bf16, n):
    return sum(_dot(t, m_bf16) for t in _bf16_terms(x, n))


def _rms(xf, g):
    return xf * lax.rsqrt(jnp.mean(xf * xf, axis=-1, keepdims=True) + NORM_EPS) * g


def _sigmoid(x):
    return 1.0 / (1.0 + jnp.exp(-x))


def _log_sigmoid(x):
    return jnp.minimum(x, 0.0) - jnp.log(1.0 + jnp.exp(-jnp.abs(x)))


def _const_spec(shape):
    nd = len(shape)
    return pl.BlockSpec(shape, lambda *_: (0,) * nd, pipeline_mode=pl.Buffered(1))


def _params(n_axes):
    return pltpu.CompilerParams(dimension_semantics=("arbitrary",) * n_axes,
                                vmem_limit_bytes=VMEM_LIMIT)


def _inproj_kernel(x_ref, xnext_ref, g_ref, w_ref, fb_ref, mu_ref, w0_ref, a0_ref, kk_ref, ka_ref, rk_ref,
                   wlora_ref, gup_ref, bd_ref,
                   qkv_ref, crow_ref, mq_ref, gate_ref,
                   at_ref, rt_ref, bt_ref, kt_ref, bh_ref, kh_ref, v_ref, gam_ref, bonus_ref, g_out_ref,
                   carry_ref, prev_ref, u_ref, *, tiles_per_seq, mem_scale, cols):
    i = pl.program_id(0)
    tm = x_ref.shape[0]
    w = v_ref.shape[1]

    @pl.when(i % tiles_per_seq == 0)
    def _():
        carry_ref[...] = jnp.zeros_like(carry_ref)
        prev_ref[...] = jnp.zeros_like(prev_ref)

    @pl.when(i == 0)
    def _():
        u_ref[...] = _rms(x_ref[...], g_ref[...]).astype(BF16)

    def proj(name, lo=0, hi=None):
        start, stop = cols[name]
        return _dot(u, w_ref[:, start + lo:(stop if hi is None else start + hi)])

    u = u_ref[...]

    p = proj("rwkv")
    u_next = _rms(xnext_ref[...], g_ref[...]).astype(BF16)
    rowid = lax.broadcasted_iota(jnp.int32, p.shape, 0)
    shifted = jnp.where(rowid == 0, prev_ref[...], pltpu.roll(p, 1, axis=0))
    prev_ref[...] = p[tm - 1:tm, :]
    xs = p + (shifted - p) * mu_ref[...]
    r = xs[:, 0:w]
    k = xs[:, w:2 * w]
    v = xs[:, 2 * w:3 * w]
    lora_in = xs[:, 3 * w:3 * w + LANES]
    gd = xs[:, 3 * w + LANES:]
    lane = lax.broadcasted_iota(jnp.int32, lora_in.shape, 1)
    lora_in = jnp.where(lane < lora_in.shape[1] // 2, jnp.tanh(lora_in), lora_in)
    qkv_ref[...] = proj("qkv").astype(BF16)
    lora = _dot(lora_in.astype(BF16), wlora_ref[...])
    g_out_ref[...] = _dot(_sigmoid(gd).astype(BF16), gup_ref[...])
    w_log = -jnp.exp(_log_sigmoid(w0_ref[...] + lora[:, :w]) - 0.5)
    a = _sigmoid(a0_ref[...] + lora[:, w:])

    d = gate_ref.shape[1] // 3
    gate_ref[:, 0:d] = _sigmoid(proj("gate", 0, d)).astype(BF16)

    bd = bd_ref[...]
    kk = k * kk_ref[...]
    kk = kk * lax.rsqrt(jnp.maximum(_dot_terms_rhs(kk * kk, bd, HEAD_SUM_TERMS), 1e-24))
    k2 = k * (1.0 + (a - 1.0) * ka_ref[...])
    bonus_ref[...] = _dot_terms_rhs(r * k2 * rk_ref[...], bd, HEAD_SUM_TERMS) * v

    sb = min(tm, SUM_BLOCK)
    ti = lax.broadcasted_iota(jnp.int32, (sb, sb), 0)
    si = lax.broadcasted_iota(jnp.int32, (sb, sb), 1)
    same = (ti // CHUNK) == (si // CHUNK)
    lower = jnp.where(same & (si <= ti), 1.0, 0.0).astype(BF16)
    upper = jnp.where(same & (si > ti), 1.0, 0.0).astype(BF16)
    w_terms = _bf16_terms(w_log, DECAY_SUM_TERMS)
    blocks = [slice(s0, s0 + sb) for s0 in range(0, tm, sb)]
    cum = jnp.concatenate([sum(_dot(lower, t[blk]) for t in w_terms) for blk in blocks], axis=0)
    rem = jnp.concatenate([sum(_dot(upper, t[blk]) for t in w_terms) for blk in blocks], axis=0)
    n_sel = gam_ref.shape[1]
    ci = lax.broadcasted_iota(jnp.int32, (n_sel, tm), 0)
    sj = lax.broadcasted_iota(jnp.int32, (n_sel, tm), 1)
    sel = jnp.where(ci == sj // CHUNK, 1.0, 0.0).astype(BF16)
    gam_ref[0] = jnp.exp(sum(_dot(sel, t) for t in w_terms))

    gate_ref[:, d:2 * d] = _sigmoid(proj("gate", d, 2 * d)).astype(BF16)

    e_prev = jnp.exp(cum - w_log)
    e_cum = jnp.exp(cum)
    e_neg = jnp.exp(-cum)
    e_rem = jnp.exp(rem)
    b = kk * a
    at_ref[...] = (-kk * e_prev).astype(BF16)
    rt_ref[...] = (r * e_cum).astype(BF16)
    bt_ref[...] = (b * e_neg).astype(BF16)
    kt_ref[...] = (k2 * e_neg).astype(BF16)
    bh_ref[...] = (b * e_rem).astype(BF16)
    kh_ref[...] = (k2 * e_rem).astype(BF16)
    v_ref[...] = v.astype(BF16)

    ff = proj("ff")
    gate_ref[:, 2 * d:3 * d] = _sigmoid(proj("gate", 2 * d, 3 * d)).astype(BF16)
    ls = _log_sigmoid(ff + fb_ref[...])
    tri = jnp.where(lax.broadcasted_iota(jnp.int32, (tm, tm), 0) >= lax.broadcasted_iota(jnp.int32, (tm, tm), 1),
                    1.0, 0.0).astype(BF16)
    c = _dot_terms_lhs(tri, ls, F32_TERMS) + carry_ref[...]
    carry_ref[...] = c[tm - 1:tm, :]
    crow_ref[...] = jnp.transpose(c * LOG2E)[:crow_ref.shape[0], :]
    mq_ref[...] = (proj("memq") * mem_scale).astype(BF16)
    u_ref[...] = u_next


def _inproj(x2, g, w_all, cols, fbias, mu, w0, a0, k_k, k_a, r_k, wlora, gup, bd, *, seq, mem_scale,
            tm=INPROJ_ROWS):
    t, d = x2.shape
    w = w0.shape[1]
    n_tiles = t // tm
    width = lambda name: cols[name][1] - cols[name][0]
    tok = lambda n, dt: jax.ShapeDtypeStruct((t, n), dt)
    n_sel = max(SUBLANES, tm // CHUNK)
    outs = ([tok(width("qkv"), BF16), jax.ShapeDtypeStruct((SUBLANES, t), F32), tok(width("memq"), BF16),
             tok(width("gate"), BF16)] + [tok(w, BF16)] * 7
            + [jax.ShapeDtypeStruct((n_tiles, n_sel, w), F32), tok(w, F32), tok(w, F32)])
    row = lambda n: pl.BlockSpec((tm, n), lambda i: (i, 0))
    consts = (g, w_all, fbias, mu, w0, a0, k_k, k_a, r_k, wlora, gup, bd)
    gam_index = 11
    results = list(pl.pallas_call(
        functools.partial(_inproj_kernel, tiles_per_seq=seq // tm, mem_scale=mem_scale, cols=cols),
        grid=(n_tiles,),
        in_specs=[row(d), pl.BlockSpec((tm, d), lambda i: (jnp.minimum(i + 1, n_tiles - 1), 0))]
        + [_const_spec(c.shape) for c in consts],
        out_specs=([row(width("qkv")), pl.BlockSpec((SUBLANES, tm), lambda i: (0, i)), row(width("memq")),
                    row(width("gate"))] + [row(w)] * 7
                   + [pl.BlockSpec((1, n_sel, w), lambda i: (i, 0, 0)), row(w), row(w)]),
        out_shape=outs,
        scratch_shapes=[pltpu.VMEM((1, LANES), F32), pltpu.VMEM((1, cols["rwkv"][1] - cols["rwkv"][0]), F32),
                        pltpu.VMEM((tm, d), BF16)],
        compiler_params=_params(1),
        name="inproj",
    )(x2, x2, *consts))
    results[gam_index] = results[gam_index][:, :tm // CHUNK]
    return results


def _fox_kernel(q_ref, k_ref, v_ref, crow_ref, o_ref, *, bk, rb, ahead):
    hp = pl.program_id(1)
    seq = q_ref.shape[0]
    first_q = lax.broadcasted_iota(jnp.int32, (rb, LANES), 1) < HEAD_DIM
    lower = (lax.broadcasted_iota(jnp.int32, (rb, rb), 0) >= lax.broadcasted_iota(jnp.int32, (rb, rb), 1))
    sub = lax.broadcasted_iota(jnp.int32, crow_ref.shape, 0)
    crow_all = crow_ref[...]
    crow = [jnp.sum(jnp.where(sub == 2 * hp + e, crow_all, 0.0), axis=0, keepdims=True) for e in range(2)]

    units = []
    for r in range(seq // rb):
        stop = (r + 1) * rb
        units += [(r, k0, min(bk, stop - k0)) for k0 in range(0, stop, bk)]
    units.sort(key=lambda u: (u[0] * rb // bk, u[1], u[0]))

    scores, state = {}, {}

    def issue_scores(u):
        r, k0, width = u
        q2 = q_ref[r * rb:(r + 1) * rb, :]
        zero = jnp.zeros_like(q2)
        kt = k_ref[k0:k0 + width, :]
        scores[u, 0] = _dot_nt(jnp.where(first_q, q2, zero), kt)
        scores[u, 1] = _dot_nt(jnp.where(first_q, zero, q2), kt)

    def finish(u):
        r, k0, width = u
        vt = v_ref[k0:k0 + width, :]
        one = jnp.ones_like(vt)
        first_k = lax.broadcasted_iota(jnp.int32, vt.shape, 1) < HEAD_DIM
        on_diagonal = k0 + width == (r + 1) * rb
        for e in range(2):
            v_aug = jnp.where(first_k, vt, one) if e == 0 else jnp.where(first_k, one, vt)
            s = scores.pop((u, e)) - crow[e][:, k0:k0 + width]
            if on_diagonal:
                tail = jnp.where(lower, s[:, width - rb:], NEG_BIG)
                s = tail if width == rb else jnp.concatenate([s[:, :width - rb], tail], axis=1)
            m_tile = jnp.max(s, axis=-1, keepdims=True)
            if k0 == 0:
                m_new = m_tile
                acc = _dot(jnp.exp2(s - m_new).astype(BF16), v_aug)
            else:
                m_old, acc_old = state[r, e]
                m_new = jnp.maximum(m_old, m_tile)
                acc = jnp.exp2(m_old - m_new) * acc_old + _dot(jnp.exp2(s - m_new).astype(BF16), v_aug)
            state[r, e] = (m_new, acc)
        if on_diagonal:
            a0, a1 = state.pop((r, 0))[1], state.pop((r, 1))[1]
            den = jnp.where(first_q, pltpu.roll(a0, HEAD_DIM, axis=1), pltpu.roll(a1, HEAD_DIM, axis=1))
            o_ref[r * rb:(r + 1) * rb, :] = (jnp.where(first_q, a0, a1) / den).astype(o_ref.dtype)

    for u in units[:ahead]:
        issue_scores(u)
    for n, u in enumerate(units):
        if n + ahead < len(units):
            issue_scores(units[n + ahead])
        finish(u)


def _fox(qkv, crow, *, batch, seq, bk=512, rb=128, ahead=2):
    t = qkv.shape[0]
    n_pairs = qkv.shape[1] // (3 * LANES)
    return pl.pallas_call(
        functools.partial(_fox_kernel, bk=bk, rb=rb, ahead=ahead),
        grid=(batch, n_pairs),
        in_specs=[
            pl.BlockSpec((seq, LANES), lambda b, h: (b, h)),
            pl.BlockSpec((seq, LANES), lambda b, h: (b, n_pairs + h)),
            pl.BlockSpec((seq, LANES), lambda b, h: (b, 2 * n_pairs + h)),
            pl.BlockSpec((crow.shape[0], seq), lambda b, h: (0, b)),
        ],
        out_specs=pl.BlockSpec((seq, LANES), lambda b, h: (b, h)),
        out_shape=jax.ShapeDtypeStruct((t, n_pairs * LANES), BF16),
        compiler_params=_params(2),
        name="fox_attention",
    )(qkv, qkv, qkv, crow)


def _memkv_kernel(m_ref, g_ref, w_ref, o_ref):
    o_ref[...] = _dot(_rms(m_ref[...], g_ref[...]).astype(BF16), w_ref[...]).astype(BF16)


def _memkv(mem2, g, w, tm=512):
    t, d = mem2.shape
    return pl.pallas_call(
        _memkv_kernel,
        grid=(t // tm,),
        in_specs=[pl.BlockSpec((tm, d), lambda i: (i, 0)), _const_spec(g.shape), _const_spec(w.shape)],
        out_specs=pl.BlockSpec((tm, w.shape[1]), lambda i: (i, 0)),
        out_shape=jax.ShapeDtypeStruct((t, w.shape[1]), BF16),
        compiler_params=_params(1),
        name="mem_kv",
    )(mem2, g, w)


def _rwkv_scan_kernel(at_ref, rt_ref, bt_ref, kt_ref, bh_ref, kh_ref, v_ref, gam_ref, bonus_ref, g_ref,
                      gng_ref, gnb_ref, bdavg_ref, o_ref, state_ref, y_ref):
    @pl.when(pl.program_id(1) == 0)
    def _():
        state_ref[...] = jnp.zeros_like(state_ref)

    n_t, w = v_ref.shape
    n_pairs = w // LANES
    n_chunks = n_t // CHUNK
    lane = lax.broadcasted_iota(jnp.int32, (CHUNK, LANES), 1)
    first = lane < HEAD_DIM
    ri = lax.broadcasted_iota(jnp.int32, (PAIR, PAIR), 0)
    ci = lax.broadcasted_iota(jnp.int32, (PAIR, PAIR), 1)
    strict = ri > ci
    eye = jnp.where(ri == ci, 1.0, 0.0)
    incl2 = (lax.broadcasted_iota(jnp.int32, (PAIR, 2 * PAIR), 0)
             >= (lax.broadcasted_iota(jnp.int32, (PAIR, 2 * PAIR), 1) & (PAIR - 1)))

    def stacked(ref, c, p):
        x = ref[c * CHUNK:(c + 1) * CHUNK, p * LANES:(p + 1) * LANES]
        z = jnp.zeros_like(x)
        return jnp.concatenate([jnp.where(first, x, z), jnp.where(first, z, x)], axis=0)

    pairs = range(n_pairs)
    states = [state_ref[p] for p in pairs]
    p_mat, d_mat, m_t, c_t = {}, {}, {}, {}

    def chain_step(c):
        h_bf = [states[p].astype(BF16) for p in pairs]
        for p in pairs:
            y_s = _dot_nt(p_mat.pop((c, p)), h_bf[p]) + d_mat.pop((c, p))
            gam = gam_ref[0, c:c + 1, p * LANES:(p + 1) * LANES]
            states[p] = states[p] * gam + _dot(h_bf[p], m_t.pop((c, p))) + c_t.pop((c, p))
            y_ref[c * CHUNK:(c + 1) * CHUNK, p * LANES:(p + 1) * LANES] = y_s[:CHUNK] + y_s[CHUNK:]

    def chunk_stages(cps):
        gram, s_inv, q_pow, a_ak, a_rbk, x_rhs, sx = {}, {}, {}, {}, {}, {}, {}
        for cp in cps:
            lhs = jnp.concatenate([stacked(at_ref, *cp), stacked(rt_ref, *cp)], axis=0)
            rhs = jnp.concatenate([stacked(bt_ref, *cp), stacked(kt_ref, *cp)], axis=0)
            gram[cp] = _dot_nt(lhs, rhs)
        yield
        for cp in cps:
            g_all = gram.pop(cp)
            n_ab = jnp.where(strict, g_all[:PAIR, :PAIR], 0.0)
            a_ak[cp] = jnp.where(strict, g_all[:PAIR, PAIR:], 0.0).astype(BF16)
            a_rbk[cp] = jnp.where(incl2, g_all[PAIR:, :], 0.0).astype(BF16)
            s_inv[cp] = eye + n_ab
            q_pow[cp] = n_ab.astype(BF16)
        for cp in cps:
            q_pow[cp] = _dot(q_pow[cp], q_pow[cp]).astype(BF16)
            x_rhs[cp] = _dot(a_ak.pop(cp), stacked(v_ref, *cp))
        yield
        span = 2
        while 2 * span < CHUNK:
            for cp in cps:
                both = _dot(q_pow[cp], jnp.concatenate([s_inv[cp].astype(BF16), q_pow[cp]], axis=1))
                s_inv[cp] = s_inv[cp] + both[:, :PAIR]
                q_pow[cp] = both[:, PAIR:].astype(BF16)
            span *= 2
            yield
        for cp in cps:
            s_inv[cp] = (s_inv[cp] + _dot(q_pow.pop(cp), s_inv[cp].astype(BF16))).astype(BF16)
        yield
        for cp in cps:
            x = jnp.concatenate([stacked(at_ref, *cp), x_rhs.pop(cp).astype(BF16)], axis=1)
            sx[cp] = _dot(s_inv.pop(cp), x).astype(BF16)
        yield
        for cp in cps:
            zero = jnp.zeros((PAIR, PAIR), BF16)
            lower_rows = jnp.concatenate([zero, stacked(v_ref, *cp)], axis=1)
            full = jnp.concatenate([sx.pop(cp), lower_rows], axis=0)
            pd = _dot(a_rbk.pop(cp), full)
            p_mat[cp] = (pd[:, :PAIR] + stacked(rt_ref, *cp).astype(F32)).astype(BF16)
            d_mat[cp] = pd[:, PAIR:]
            mc = _dot_tn(full, jnp.concatenate([stacked(bh_ref, *cp), stacked(kh_ref, *cp)], axis=0))
            m_t[cp] = mc[:PAIR].astype(BF16)
            c_t[cp] = mc[PAIR:]
        yield

    group = max(1, n_chunks // 2)
    pending = []
    for c0 in range(0, n_chunks, group):
        chunks = range(c0, min(c0 + group, n_chunks))
        for _ in chunk_stages([(c, p) for c in chunks for p in pairs]):
            if pending:
                chain_step(pending.pop(0))
        pending += list(chunks)
    for c in pending:
        chain_step(c)
    for p in pairs:
        state_ref[p] = states[p]

    y = y_ref[...]
    bdavg = bdavg_ref[...]
    dev = y - _dot_terms_rhs(y, bdavg, HEAD_SUM_TERMS + 1)
    var = _dot_terms_rhs(dev * dev, bdavg, HEAD_SUM_TERMS)
    yn = dev * lax.rsqrt(var + GN_EPS) * gng_ref[...] + gnb_ref[...]
    o_ref[...] = ((yn + bonus_ref[...]) * g_ref[...]).astype(o_ref.dtype)


def _rwkv_scan(at, rt, bt, kt, bh, kh, v, gam, bonus, g, gn_g, gn_b, bdavg, *, batch, seq, tl=SCAN_ROWS):
    t, w = v.shape
    n_sel = tl // CHUNK
    gam = gam.reshape(t // tl, n_sel, w)
    nt = seq // tl
    tok = pl.BlockSpec((tl, w), lambda b, i: (b * nt + i, 0))
    return pl.pallas_call(
        _rwkv_scan_kernel,
        grid=(batch, nt),
        in_specs=[tok] * 7 + [pl.BlockSpec((1, n_sel, w), lambda b, i: (b * nt + i, 0, 0)), tok, tok,
                              _const_spec(gn_g.shape), _const_spec(gn_b.shape), _const_spec(bdavg.shape)],
        out_specs=tok,
        out_shape=jax.ShapeDtypeStruct((t, w), BF16),
        scratch_shapes=[pltpu.VMEM((w // LANES, LANES, LANES), F32), pltpu.VMEM((tl, w), F32)],
        compiler_params=_params(2),
        name="rwkv_scan",
    )(at, rt, bt, kt, bh, kh, v, gam, bonus, g, gn_g, gn_b, bdavg)


def _merge_kernel(x_ref, fox_ref, rw_ref, mq_ref, kv_ref, gate_ref, wf_ref, wr_ref, wm_ref, wo_ref, g_ref, o_ref):
    d = x_ref.shape[1]
    w = mq_ref.shape[1]
    hd = w // MEM_HEADS
    heads = range(MEM_HEADS)
    scores = [_dot_nt(mq_ref[:, h * hd:(h + 1) * hd], kv_ref[:, h * hd:(h + 1) * hd]) for h in heads]
    merged = (gate_ref[:, 0:d].astype(F32) * _dot(fox_ref[...], wf_ref[...])
              + gate_ref[:, d:2 * d].astype(F32) * _dot(rw_ref[...], wr_ref[...]))
    probs, norms = [], []
    for h in heads:
        p = jnp.exp(scores[h] - jnp.max(scores[h], axis=-1, keepdims=True))
        norms.append(jnp.sum(p, axis=-1, keepdims=True))
        probs.append(p.astype(BF16))
    mem_out = jnp.concatenate(
        [(_dot(probs[h], kv_ref[:, w + h * hd:w + (h + 1) * hd]) / norms[h]).astype(BF16) for h in heads], axis=1)
    merged = merged + gate_ref[:, 2 * d:3 * d].astype(F32) * _dot(mem_out, wm_ref[...])
    y = _dot(merged.astype(BF16), wo_ref[...])
    o_ref[...] = x_ref[...] + _rms(y, g_ref[...])


def _merge(x2, fox, rw, mq, memkv, gates, wf, wr, wm, wo, g, *, seq, mem_len, tm=1024):
    t, d = x2.shape
    assert seq % tm == 0
    tiles_per_seq = seq // tm
    row = lambda width: pl.BlockSpec((tm, width), lambda i: (i, 0))
    return pl.pallas_call(
        _merge_kernel,
        grid=(t // tm,),
        in_specs=[row(d), row(fox.shape[1]), row(rw.shape[1]), row(mq.shape[1]),
                  pl.BlockSpec((mem_len, memkv.shape[1]), lambda i: (i // tiles_per_seq, 0)), row(gates.shape[1]),
                  _const_spec(wf.shape), _const_spec(wr.shape), _const_spec(wm.shape), _const_spec(wo.shape),
                  _const_spec(g.shape)],
        out_specs=row(d),
        out_shape=jax.ShapeDtypeStruct((t, d), F32),
        compiler_params=_params(1),
        name="merge_out",
    )(x2, fox, rw, mq, memkv, gates, wf, wr, wm, wo, g)


def _ffn_kernel(h_ref, g1_ref, wg_ref, wu_ref, wd_ref, g2_ref, o_ref, *, ff_chunk):
    h = h_ref[...]
    u = _rms(h, g1_ref[...]).astype(BF16)
    d_ff = wg_ref.shape[1]
    acc = jnp.zeros(h.shape, F32)
    for lo in range(0, d_ff, ff_chunk):
        hi = min(lo + ff_chunk, d_ff)
        gt = _dot(u, wg_ref[:, lo:hi])
        up = _dot(u, wu_ref[:, lo:hi])
        act = (gt * _sigmoid(gt) * up).astype(BF16)
        acc = acc + _dot(act, wd_ref[lo:hi, :])
    o_ref[...] = h + _rms(acc, g2_ref[...])


def _ffn(h2, g1, wg, wu, wd, g2, tm=1024, ff_chunk=1024):
    t, d = h2.shape
    row = pl.BlockSpec((tm, d), lambda i: (i, 0))
    return pl.pallas_call(
        functools.partial(_ffn_kernel, ff_chunk=ff_chunk),
        grid=(t // tm,),
        in_specs=[row, _const_spec(g1.shape), _const_spec(wg.shape), _const_spec(wu.shape),
                  _const_spec(wd.shape), _const_spec(g2.shape)],
        out_specs=row,
        out_shape=jax.ShapeDtypeStruct((t, d), F32),
        compiler_params=_params(1),
        name="ffn",
    )(h2, g1, wg, wu, wd, g2)


def _block_diag_ones(width, block, value):
    idx = jnp.arange(width) // block
    return jnp.where(idx[:, None] == idx[None, :], value, 0.0).astype(BF16)


def kernel(x, mem, pre1_g, post1_g, pre2_g, post2_g, mem_norm_g, w_in, fox_f_bias, rwkv_mu, rwkv_w0,
           rwkv_w_up, rwkv_a0, rwkv_a_up, rwkv_g_up, rwkv_k_k, rwkv_k_a, rwkv_r_k, rwkv_gn_g, rwkv_gn_b,
           w_mem_kv, w_fox_out, w_rwkv_out, w_mem_out, w_o, w_ffn_gate, w_ffn_up, w_ffn_down):
    batch, seq, d = x.shape
    mem_len = mem.shape[1]
    depth = w_in.shape[0]
    fox_heads = fox_f_bias.shape[1]
    fox_w = fox_heads * HEAD_DIM
    rw_w = rwkv_w0.shape[1]
    rw_cols = rwkv_mu.shape[1]
    mem_w = w_mem_kv.shape[2] // 2
    dec_lora = rwkv_w_up.shape[1]
    aaa_lora = rwkv_a_up.shape[1]
    assert dec_lora + aaa_lora == LANES and dec_lora == aaa_lora
    assert fox_heads <= SUBLANES and fox_w % LANES == 0 and rw_w % LANES == 0
    row2 = lambda v: v.reshape(1, -1)

    h = x.reshape(batch * seq, d)
    mem2 = mem.reshape(batch * mem_len, d)
    bd_ones = _block_diag_ones(rw_w, HEAD_DIM, 1.0)
    bd_avg = _block_diag_ones(rw_w, HEAD_DIM, 1.0 / HEAD_DIM)
    for l in range(depth):
        wi = w_in[l]
        o_ff = 3 * fox_w
        o_rw = o_ff + fox_heads
        o_mq = o_rw + rw_cols
        o_gt = o_mq + mem_w
        scale = HEAD_DIM ** -0.5 * LOG2E
        sections = [("qkv", jnp.concatenate([wi[:, :fox_w] * scale, wi[:, fox_w:o_ff]], axis=1)),
                    ("ff", jnp.pad(wi[:, o_ff:o_rw], ((0, 0), (0, LANES - fox_heads)))),
                    ("rwkv", wi[:, o_rw:o_mq]), ("memq", wi[:, o_mq:o_gt]), ("gate", wi[:, o_gt:])]
        w_all = jnp.concatenate([sec for _, sec in sections], axis=1).astype(BF16)
        cols, start = {}, 0
        for name, sec in sections:
            assert sec.shape[1] % LANES == 0
            cols[name] = (start, start + sec.shape[1])
            start += sec.shape[1]
        fbias = jnp.pad(row2(fox_f_bias[l]), ((0, 0), (0, LANES - fox_heads)))
        wlora = jnp.zeros((LANES, 2 * rw_w), F32)
        wlora = wlora.at[:dec_lora, :rw_w].set(rwkv_w_up[l]).at[dec_lora:, rw_w:].set(rwkv_a_up[l]).astype(BF16)

        qkv, crow, mq, gates, *scan_in = _inproj(
            h, row2(pre1_g[l]), w_all, cols, fbias, row2(rwkv_mu[l]), row2(rwkv_w0[l]), row2(rwkv_a0[l]),
            row2(rwkv_k_k[l]), row2(rwkv_k_a[l]), row2(rwkv_r_k[l]), wlora, rwkv_g_up[l].astype(BF16), bd_ones,
            seq=seq, mem_scale=(mem_w // MEM_HEADS) ** -0.5)

        fox_out = _fox(qkv, crow, batch=batch, seq=seq)

        memkv = _memkv(mem2, row2(mem_norm_g[l]), w_mem_kv[l].astype(BF16))
        rwkv_out = _rwkv_scan(*scan_in, row2(rwkv_gn_g[l]), row2(rwkv_gn_b[l]), bd_avg, batch=batch, seq=seq)

        h = _merge(h, fox_out, rwkv_out, mq, memkv, gates, w_fox_out[l].astype(BF16),
                   w_rwkv_out[l].astype(BF16), w_mem_out[l].astype(BF16), w_o[l].astype(BF16),
                   row2(post1_g[l]), seq=seq, mem_len=mem_len)
        h = _ffn(h, row2(pre2_g[l]), w_ffn_gate[l].astype(BF16), w_ffn_up[l].astype(BF16),
                 w_ffn_down[l].astype(BF16), row2(post2_g[l]))
    return h.reshape(batch, seq, d).astype(x.dtype)
```

```python
import functools

import jax
import jax.numpy as jnp
from jax import lax
from jax.experimental import pallas as pl
from jax.experimental.pallas import tpu as pltpu

F32 = jnp.float32
BF16 = jnp.bfloat16

NORM_EPS = 1e-6
GN_EPS = 64e-5
HEAD_DIM = 64
LANES = 128
SUBLANES = 8
LOG2E = 1.4426950408889634
MEM_HEADS = 4
CHUNK = 64
PAIR = 2 * CHUNK
F32_TERMS = 3
DECAY_SUM_TERMS = 2
HEAD_SUM_TERMS = 1
SUM_BLOCK = 256
INPROJ_ROWS = 256
SCAN_ROWS = 512
NEG_BIG = -1e30
VMEM_LIMIT = 56 * 1024 * 1024


def _dot(a, b):
    return jnp.dot(a, b, preferred_element_type=F32)


def _dot_nt(a, b):
    return lax.dot_general(a, b, (((1,), (1,)), ((), ())), preferred_element_type=F32)


def _dot_tn(a, b):
    return lax.dot_general(a, b, (((0,), (0,)), ((), ())), preferred_element_type=F32)


def _bf16_terms(x, n):
    terms = []
    for _ in range(n - 1):
        t = x.astype(BF16)
        terms.append(t)
        x = x - t.astype(F32)
    return terms + [x.astype(BF16)]


def _dot_terms_lhs(m_bf16, x, n):
    return sum(_dot(m_bf16, t) for t in _bf16_terms(x, n))


def _dot_terms_rhs(x, m_bf16, n):
    return sum(_dot(t, m_bf16) for t in _bf16_terms(x, n))


def _rms(xf, g):
    return xf * lax.rsqrt(jnp.mean(xf * xf, axis=-1, keepdims=True) + NORM_EPS) * g


def _sigmoid(x):
    return 1.0 / (1.0 + jnp.exp(-x))


def _log_sigmoid(x):
    return jnp.minimum(x, 0.0) - jnp.log(1.0 + jnp.exp(-jnp.abs(x)))


def _const_spec(shape):
    nd = len(shape)
    return pl.BlockSpec(shape, lambda *_: (0,) * nd, pipeline_mode=pl.Buffered(1))


def _params(n_axes):
    return pltpu.CompilerParams(dimension_semantics=("arbitrary",) * n_axes,
                                vmem_limit_bytes=VMEM_LIMIT)


def _inproj_kernel(x_ref, g_ref, w_ref, fb_ref, mu_ref, w0_ref, a0_ref, kk_ref, ka_ref, rk_ref,
                   wlora_ref, gup_ref, bd_ref,
                   qkv_ref, crow_ref, mq_ref, gate_ref,
                   at_ref, rt_ref, bt_ref, kt_ref, bh_ref, kh_ref, v_ref, gam_ref, bonus_ref, g_out_ref,
                   carry_ref, prev_ref, *, tiles_per_seq, mem_scale, cols):
    i = pl.program_id(0)
    tm = x_ref.shape[0]
    w = v_ref.shape[1]

    @pl.when(i % tiles_per_seq == 0)
    def _():
        carry_ref[...] = jnp.zeros_like(carry_ref)
        prev_ref[...] = jnp.zeros_like(prev_ref)

    def proj(name, lo=0, hi=None):
        start, stop = cols[name]
        return _dot(u, w_ref[:, start + lo:(stop if hi is None else start + hi)])

    u = _rms(x_ref[...], g_ref[...]).astype(BF16)

    p = proj("rwkv")
    rowid = lax.broadcasted_iota(jnp.int32, p.shape, 0)
    shifted = jnp.where(rowid == 0, prev_ref[...], pltpu.roll(p, 1, axis=0))
    prev_ref[...] = p[tm - 1:tm, :]
    xs = p + (shifted - p) * mu_ref[...]
    r = xs[:, 0:w]
    k = xs[:, w:2 * w]
    v = xs[:, 2 * w:3 * w]
    lora_in = xs[:, 3 * w:3 * w + LANES]
    gd = xs[:, 3 * w + LANES:]
    lane = lax.broadcasted_iota(jnp.int32, lora_in.shape, 1)
    lora_in = jnp.where(lane < lora_in.shape[1] // 2, jnp.tanh(lora_in), lora_in)
    qkv_ref[...] = proj("qkv").astype(BF16)
    lora = _dot(lora_in.astype(BF16), wlora_ref[...])
    g_out_ref[...] = _dot(_sigmoid(gd).astype(BF16), gup_ref[...])
    w_log = -jnp.exp(_log_sigmoid(w0_ref[...] + lora[:, :w]) - 0.5)
    a = _sigmoid(a0_ref[...] + lora[:, w:])

    d = gate_ref.shape[1] // 3
    gate_ref[:, 0:d] = _sigmoid(proj("gate", 0, d)).astype(BF16)

    bd = bd_ref[...]
    kk = k * kk_ref[...]
    kk = kk * lax.rsqrt(jnp.maximum(_dot_terms_rhs(kk * kk, bd, HEAD_SUM_TERMS), 1e-24))
    gate_ref[:, d:2 * d] = _sigmoid(proj("gate", d, 2 * d)).astype(BF16)
    k2 = k * (1.0 + (a - 1.0) * ka_ref[...])
    bonus_ref[...] = _dot_terms_rhs(r * k2 * rk_ref[...], bd, HEAD_SUM_TERMS) * v

    sb = min(tm, SUM_BLOCK)
    ti = lax.broadcasted_iota(jnp.int32, (sb, sb), 0)
    si = lax.broadcasted_iota(jnp.int32, (sb, sb), 1)
    same = (ti // CHUNK) == (si // CHUNK)
    lower = jnp.where(same & (si <= ti), 1.0, 0.0).astype(BF16)
    upper = jnp.where(same & (si > ti), 1.0, 0.0).astype(BF16)
    w_terms = _bf16_terms(w_log, DECAY_SUM_TERMS)
    blocks = [slice(s0, s0 + sb) for s0 in range(0, tm, sb)]
    cum = jnp.concatenate([sum(_dot(lower, t[blk]) for t in w_terms) for blk in blocks], axis=0)
    rem = jnp.concatenate([sum(_dot(upper, t[blk]) for t in w_terms) for blk in blocks], axis=0)
    n_sel = gam_ref.shape[1]
    ci = lax.broadcasted_iota(jnp.int32, (n_sel, tm), 0)
    sj = lax.broadcasted_iota(jnp.int32, (n_sel, tm), 1)
    sel = jnp.where(ci == sj // CHUNK, 1.0, 0.0).astype(BF16)
    gam_ref[0] = jnp.exp(sum(_dot(sel, t) for t in w_terms))

    e_prev = jnp.exp(cum - w_log)
    e_cum = jnp.exp(cum)
    e_neg = jnp.exp(-cum)
    e_rem = jnp.exp(rem)
    b = kk * a
    at_ref[...] = (-kk * e_prev).astype(BF16)
    rt_ref[...] = (r * e_cum).astype(BF16)
    bt_ref[...] = (b * e_neg).astype(BF16)
    kt_ref[...] = (k2 * e_neg).astype(BF16)
    bh_ref[...] = (b * e_rem).astype(BF16)
    kh_ref[...] = (k2 * e_rem).astype(BF16)
    v_ref[...] = v.astype(BF16)

    ff = proj("ff")
    gate_ref[:, 2 * d:3 * d] = _sigmoid(proj("gate", 2 * d, 3 * d)).astype(BF16)
    ls = _log_sigmoid(ff + fb_ref[...])
    tri = jnp.where(lax.broadcasted_iota(jnp.int32, (tm, tm), 0) >= lax.broadcasted_iota(jnp.int32, (tm, tm), 1),
                    1.0, 0.0).astype(BF16)
    c = _dot_terms_lhs(tri, ls, F32_TERMS) + carry_ref[...]
    carry_ref[...] = c[tm - 1:tm, :]
    crow_ref[...] = jnp.transpose(c * LOG2E)[:crow_ref.shape[0], :]
    mq_ref[...] = (proj("memq") * mem_scale).astype(BF16)


def _inproj(x2, g, w_all, cols, fbias, mu, w0, a0, k_k, k_a, r_k, wlora, gup, bd, *, seq, mem_scale,
            tm=INPROJ_ROWS):
    t, d = x2.shape
    w = w0.shape[1]
    n_tiles = t // tm
    width = lambda name: cols[name][1] - cols[name][0]
    tok = lambda n, dt: jax.ShapeDtypeStruct((t, n), dt)
    n_sel = max(SUBLANES, tm // CHUNK)
    outs = ([tok(width("qkv"), BF16), jax.ShapeDtypeStruct((SUBLANES, t), F32), tok(width("memq"), BF16),
             tok(width("gate"), BF16)] + [tok(w, BF16)] * 7
            + [jax.ShapeDtypeStruct((n_tiles, n_sel, w), F32), tok(w, F32), tok(w, F32)])
    row = lambda n: pl.BlockSpec((tm, n), lambda i: (i, 0))
    consts = (g, w_all, fbias, mu, w0, a0, k_k, k_a, r_k, wlora, gup, bd)
    gam_index = 11
    results = list(pl.pallas_call(
        functools.partial(_inproj_kernel, tiles_per_seq=seq // tm, mem_scale=mem_scale, cols=cols),
        grid=(n_tiles,),
        in_specs=[row(d)] + [_const_spec(c.shape) for c in consts],
        out_specs=([row(width("qkv")), pl.BlockSpec((SUBLANES, tm), lambda i: (0, i)), row(width("memq")),
                    row(width("gate"))] + [row(w)] * 7
                   + [pl.BlockSpec((1, n_sel, w), lambda i: (i, 0, 0)), row(w), row(w)]),
        out_shape=outs,
        scratch_shapes=[pltpu.VMEM((1, LANES), F32), pltpu.VMEM((1, cols["rwkv"][1] - cols["rwkv"][0]), F32)],
        compiler_params=_params(1),
        name="inproj",
    )(x2, *consts))
    results[gam_index] = results[gam_index][:, :tm // CHUNK]
    return results


def _fox_kernel(q_ref, k_ref, v_ref, crow_ref, o_ref, *, bk, rb, ahead):
    hp = pl.program_id(1)
    seq = q_ref.shape[0]
    first_q = lax.broadcasted_iota(jnp.int32, (rb, LANES), 1) < HEAD_DIM
    lower = (lax.broadcasted_iota(jnp.int32, (rb, rb), 0) >= lax.broadcasted_iota(jnp.int32, (rb, rb), 1))
    sub = lax.broadcasted_iota(jnp.int32, crow_ref.shape, 0)
    crow_all = crow_ref[...]
    crow = [jnp.sum(jnp.where(sub == 2 * hp + e, crow_all, 0.0), axis=0, keepdims=True) for e in range(2)]

    units = []
    for r in range(seq // rb):
        stop = (r + 1) * rb
        units += [(r, k0, min(bk, stop - k0)) for k0 in range(0, stop, bk)]
    units.sort(key=lambda u: (u[0] * rb // bk, u[1], u[0]))

    scores, state = {}, {}

    def issue_scores(u):
        r, k0, width = u
        q2 = q_ref[r * rb:(r + 1) * rb, :]
        zero = jnp.zeros_like(q2)
        kt = k_ref[k0:k0 + width, :]
        scores[u, 0] = _dot_nt(jnp.where(first_q, q2, zero), kt)
        scores[u, 1] = _dot_nt(jnp.where(first_q, zero, q2), kt)

    def finish(u):
        r, k0, width = u
        vt = v_ref[k0:k0 + width, :]
        one = jnp.ones_like(vt)
        first_k = lax.broadcasted_iota(jnp.int32, vt.shape, 1) < HEAD_DIM
        on_diagonal = k0 + width == (r + 1) * rb
        for e in range(2):
            v_aug = jnp.where(first_k, vt, one) if e == 0 else jnp.where(first_k, one, vt)
            s = scores.pop((u, e)) - crow[e][:, k0:k0 + width]
            if on_diagonal:
                tail = jnp.where(lower, s[:, width - rb:], NEG_BIG)
                s = tail if width == rb else jnp.concatenate([s[:, :width - rb], tail], axis=1)
            m_tile = jnp.max(s, axis=-1, keepdims=True)
            if k0 == 0:
                m_new = m_tile
                acc = _dot(jnp.exp2(s - m_new).astype(BF16), v_aug)
            else:
                m_old, acc_old = state[r, e]
                m_new = jnp.maximum(m_old, m_tile)
                acc = jnp.exp2(m_old - m_new) * acc_old + _dot(jnp.exp2(s - m_new).astype(BF16), v_aug)
            state[r, e] = (m_new, acc)
        if on_diagonal:
            a0, a1 = state.pop((r, 0))[1], state.pop((r, 1))[1]
            den = jnp.where(first_q, pltpu.roll(a0, HEAD_DIM, axis=1), pltpu.roll(a1, HEAD_DIM, axis=1))
            o_ref[r * rb:(r + 1) * rb, :] = (jnp.where(first_q, a0, a1) / den).astype(o_ref.dtype)

    for u in units[:ahead]:
        issue_scores(u)
    for n, u in enumerate(units):
        if n + ahead < len(units):
            issue_scores(units[n + ahead])
        finish(u)


def _fox(qkv, crow, *, batch, seq, bk=512, rb=128, ahead=2):
    t = qkv.shape[0]
    n_pairs = qkv.shape[1] // (3 * LANES)
    return pl.pallas_call(
        functools.partial(_fox_kernel, bk=bk, rb=rb, ahead=ahead),
        grid=(batch, n_pairs),
        in_specs=[
            pl.BlockSpec((seq, LANES), lambda b, h: (b, h)),
            pl.BlockSpec((seq, LANES), lambda b, h: (b, n_pairs + h)),
            pl.BlockSpec((seq, LANES), lambda b, h: (b, 2 * n_pairs + h)),
            pl.BlockSpec((crow.shape[0], seq), lambda b, h: (0, b)),
        ],
        out_specs=pl.BlockSpec((seq, LANES), lambda b, h: (b, h)),
        out_shape=jax.ShapeDtypeStruct((t, n_pairs * LANES), BF16),
        compiler_params=_params(2),
        name="fox_attention",
    )(qkv, qkv, qkv, crow)


def _memkv_kernel(m_ref, g_ref, w_ref, o_ref):
    o_ref[...] = _dot(_rms(m_ref[...], g_ref[...]).astype(BF16), w_ref[...]).astype(BF16)


def _memkv(mem2, g, w, tm=512):
    t, d = mem2.shape
    return pl.pallas_call(
        _memkv_kernel,
        grid=(t // tm,),
        in_specs=[pl.BlockSpec((tm, d), lambda i: (i, 0)), _const_spec(g.shape), _const_spec(w.shape)],
        out_specs=pl.BlockSpec((tm, w.shape[1]), lambda i: (i, 0)),
        out_shape=jax.ShapeDtypeStruct((t, w.shape[1]), BF16),
        compiler_params=_params(1),
        name="mem_kv",
    )(mem2, g, w)


def _rwkv_scan_kernel(at_ref, rt_ref, bt_ref, kt_ref, bh_ref, kh_ref, v_ref, gam_ref, bonus_ref, g_ref,
                      gng_ref, gnb_ref, bdavg_ref, o_ref, state_ref, y_ref):
    @pl.when(pl.program_id(1) == 0)
    def _():
        state_ref[...] = jnp.zeros_like(state_ref)

    n_t, w = v_ref.shape
    n_pairs = w // LANES
    n_chunks = n_t // CHUNK
    lane = lax.broadcasted_iota(jnp.int32, (CHUNK, LANES), 1)
    first = lane < HEAD_DIM
    ri = lax.broadcasted_iota(jnp.int32, (PAIR, PAIR), 0)
    ci = lax.broadcasted_iota(jnp.int32, (PAIR, PAIR), 1)
    strict = ri > ci
    eye = jnp.where(ri == ci, 1.0, 0.0)
    incl2 = (lax.broadcasted_iota(jnp.int32, (PAIR, 2 * PAIR), 0)
             >= (lax.broadcasted_iota(jnp.int32, (PAIR, 2 * PAIR), 1) & (PAIR - 1)))

    def stacked(ref, c, p):
        x = ref[c * CHUNK:(c + 1) * CHUNK, p * LANES:(p + 1) * LANES]
        z = jnp.zeros_like(x)
        return jnp.concatenate([jnp.where(first, x, z), jnp.where(first, z, x)], axis=0)

    pairs = range(n_pairs)
    states = [state_ref[p] for p in pairs]
    p_mat, d_mat, m_t, c_t = {}, {}, {}, {}

    def chain_step(c):
        h_bf = [states[p].astype(BF16) for p in pairs]
        for p in pairs:
            y_s = _dot_nt(p_mat.pop((c, p)), h_bf[p]) + d_mat.pop((c, p))
            gam = gam_ref[0, c:c + 1, p * LANES:(p + 1) * LANES]
            states[p] = states[p] * gam + _dot(h_bf[p], m_t.pop((c, p))) + c_t.pop((c, p))
            y_ref[c * CHUNK:(c + 1) * CHUNK, p * LANES:(p + 1) * LANES] = y_s[:CHUNK] + y_s[CHUNK:]

    def chunk_stages(cps):
        gram, s_inv, q_pow, a_ak, a_rbk, x_rhs, sx = {}, {}, {}, {}, {}, {}, {}
        for cp in cps:
            lhs = jnp.concatenate([stacked(at_ref, *cp), stacked(rt_ref, *cp)], axis=0)
            rhs = jnp.concatenate([stacked(bt_ref, *cp), stacked(kt_ref, *cp)], axis=0)
            gram[cp] = _dot_nt(lhs, rhs)
        yield
        for cp in cps:
            g_all = gram.pop(cp)
            n_ab = jnp.where(strict, g_all[:PAIR, :PAIR], 0.0)
            a_ak[cp] = jnp.where(strict, g_all[:PAIR, PAIR:], 0.0).astype(BF16)
            a_rbk[cp] = jnp.where(incl2, g_all[PAIR:, :], 0.0).astype(BF16)
            s_inv[cp] = eye + n_ab
            q_pow[cp] = n_ab.astype(BF16)
        for cp in cps:
            q_pow[cp] = _dot(q_pow[cp], q_pow[cp]).astype(BF16)
            x_rhs[cp] = _dot(a_ak.pop(cp), stacked(v_ref, *cp))
        yield
        span = 2
        while 2 * span < CHUNK:
            for cp in cps:
                both = _dot(q_pow[cp], jnp.concatenate([s_inv[cp].astype(BF16), q_pow[cp]], axis=1))
                s_inv[cp] = s_inv[cp] + both[:, :PAIR]
                q_pow[cp] = both[:, PAIR:].astype(BF16)
            span *= 2
            yield
        for cp in cps:
            s_inv[cp] = (s_inv[cp] + _dot(q_pow.pop(cp), s_inv[cp].astype(BF16))).astype(BF16)
        yield
        for cp in cps:
            x = jnp.concatenate([stacked(at_ref, *cp), x_rhs.pop(cp).astype(BF16)], axis=1)
            sx[cp] = _dot(s_inv.pop(cp), x).astype(BF16)
        yield
        for cp in cps:
            zero = jnp.zeros((PAIR, PAIR), BF16)
            lower_rows = jnp.concatenate([zero, stacked(v_ref, *cp)], axis=1)
            full = jnp.concatenate([sx.pop(cp), lower_rows], axis=0)
            pd = _dot(a_rbk.pop(cp), full)
            p_mat[cp] = (pd[:, :PAIR] + stacked(rt_ref, *cp).astype(F32)).astype(BF16)
            d_mat[cp] = pd[:, PAIR:]
            mc = _dot_tn(full, jnp.concatenate([stacked(bh_ref, *cp), stacked(kh_ref, *cp)], axis=0))
            m_t[cp] = mc[:PAIR].astype(BF16)
            c_t[cp] = mc[PAIR:]
        yield

    group = max(1, n_chunks // 2)
    pending = []
    for c0 in range(0, n_chunks, group):
        chunks = range(c0, min(c0 + group, n_chunks))
        for _ in chunk_stages([(c, p) for c in chunks for p in pairs]):
            if pending:
                chain_step(pending.pop(0))
        pending += list(chunks)
    for c in pending:
        chain_step(c)
    for p in pairs:
        state_ref[p] = states[p]

    y = y_ref[...]
    bdavg = bdavg_ref[...]
    dev = y - _dot_terms_rhs(y, bdavg, HEAD_SUM_TERMS + 1)
    var = _dot_terms_rhs(dev * dev, bdavg, HEAD_SUM_TERMS)
    yn = dev * lax.rsqrt(var + GN_EPS) * gng_ref[...] + gnb_ref[...]
    o_ref[...] = ((yn + bonus_ref[...]) * g_ref[...]).astype(o_ref.dtype)


def _rwkv_scan(at, rt, bt, kt, bh, kh, v, gam, bonus, g, gn_g, gn_b, bdavg, *, batch, seq, tl=SCAN_ROWS):
    t, w = v.shape
    n_sel = tl // CHUNK
    gam = gam.reshape(t // tl, n_sel, w)
    nt = seq // tl
    tok = pl.BlockSpec((tl, w), lambda b, i: (b * nt + i, 0))
    return pl.pallas_call(
        _rwkv_scan_kernel,
        grid=(batch, nt),
        in_specs=[tok] * 7 + [pl.BlockSpec((1, n_sel, w), lambda b, i: (b * nt + i, 0, 0)), tok, tok,
                              _const_spec(gn_g.shape), _const_spec(gn_b.shape), _const_spec(bdavg.shape)],
        out_specs=tok,
        out_shape=jax.ShapeDtypeStruct((t, w), BF16),
        scratch_shapes=[pltpu.VMEM((w // LANES, LANES, LANES), F32), pltpu.VMEM((tl, w), F32)],
        compiler_params=_params(2),
        name="rwkv_scan",
    )(at, rt, bt, kt, bh, kh, v, gam, bonus, g, gn_g, gn_b, bdavg)


def _merge_kernel(x_ref, fox_ref, rw_ref, mq_ref, kv_ref, gate_ref, wf_ref, wr_ref, wm_ref, wo_ref, g_ref, o_ref):
    d = x_ref.shape[1]
    w = mq_ref.shape[1]
    hd = w // MEM_HEADS
    heads = range(MEM_HEADS)
    scores = [_dot_nt(mq_ref[:, h * hd:(h + 1) * hd], kv_ref[:, h * hd:(h + 1) * hd]) for h in heads]
    merged = (gate_ref[:, 0:d].astype(F32) * _dot(fox_ref[...], wf_ref[...])
              + gate_ref[:, d:2 * d].astype(F32) * _dot(rw_ref[...], wr_ref[...]))
    probs, norms = [], []
    for h in heads:
        p = jnp.exp(scores[h] - jnp.max(scores[h], axis=-1, keepdims=True))
        norms.append(jnp.sum(p, axis=-1, keepdims=True))
        probs.append(p.astype(BF16))
    mem_out = jnp.concatenate(
        [(_dot(probs[h], kv_ref[:, w + h * hd:w + (h + 1) * hd]) / norms[h]).astype(BF16) for h in heads], axis=1)
    merged = merged + gate_ref[:, 2 * d:3 * d].astype(F32) * _dot(mem_out, wm_ref[...])
    y = _dot(merged.astype(BF16), wo_ref[...])
    o_ref[...] = x_ref[...] + _rms(y, g_ref[...])


def _merge(x2, fox, rw, mq, memkv, gates, wf, wr, wm, wo, g, *, seq, mem_len, tm=1024):
    t, d = x2.shape
    assert seq % tm == 0
    tiles_per_seq = seq // tm
    row = lambda width: pl.BlockSpec((tm, width), lambda i: (i, 0))
    return pl.pallas_call(
        _merge_kernel,
        grid=(t // tm,),
        in_specs=[row(d), row(fox.shape[1]), row(rw.shape[1]), row(mq.shape[1]),
                  pl.BlockSpec((mem_len, memkv.shape[1]), lambda i: (i // tiles_per_seq, 0)), row(gates.shape[1]),
                  _const_spec(wf.shape), _const_spec(wr.shape), _const_spec(wm.shape), _const_spec(wo.shape),
                  _const_spec(g.shape)],
        out_specs=row(d),
        out_shape=jax.ShapeDtypeStruct((t, d), F32),
        compiler_params=_params(1),
        name="merge_out",
    )(x2, fox, rw, mq, memkv, gates, wf, wr, wm, wo, g)


def _ffn_kernel(h_ref, g1_ref, wg_ref, wu_ref, wd_ref, g2_ref, o_ref, *, ff_chunk):
    h = h_ref[...]
    u = _rms(h, g1_ref[...]).astype(BF16)
    d_ff = wg_ref.shape[1]
    acc = jnp.zeros(h.shape, F32)
    for lo in range(0, d_ff, ff_chunk):
        hi = min(lo + ff_chunk, d_ff)
        gt = _dot(u, wg_ref[:, lo:hi])
        up = _dot(u, wu_ref[:, lo:hi])
        act = (gt * _sigmoid(gt) * up).astype(BF16)
        acc = acc + _dot(act, wd_ref[lo:hi, :])
    o_ref[...] = h + _rms(acc, g2_ref[...])


def _ffn(h2, g1, wg, wu, wd, g2, tm=1024, ff_chunk=1024):
    t, d = h2.shape
    row = pl.BlockSpec((tm, d), lambda i: (i, 0))
    return pl.pallas_call(
        functools.partial(_ffn_kernel, ff_chunk=ff_chunk),
        grid=(t // tm,),
        in_specs=[row, _const_spec(g1.shape), _const_spec(wg.shape), _const_spec(wu.shape),
                  _const_spec(wd.shape), _const_spec(g2.shape)],
        out_specs=row,
        out_shape=jax.ShapeDtypeStruct((t, d), F32),
        compiler_params=_params(1),
        name="ffn",
    )(h2, g1, wg, wu, wd, g2)


def _block_diag_ones(width, block, value):
    idx = jnp.arange(width) // block
    return jnp.where(idx[:, None] == idx[None, :], value, 0.0).astype(BF16)


def kernel(x, mem, pre1_g, post1_g, pre2_g, post2_g, mem_norm_g, w_in, fox_f_bias, rwkv_mu, rwkv_w0,
           rwkv_w_up, rwkv_a0, rwkv_a_up, rwkv_g_up, rwkv_k_k, rwkv_k_a, rwkv_r_k, rwkv_gn_g, rwkv_gn_b,
           w_mem_kv, w_fox_out, w_rwkv_out, w_mem_out, w_o, w_ffn_gate, w_ffn_up, w_ffn_down):
    batch, seq, d = x.shape
    mem_len = mem.shape[1]
    depth = w_in.shape[0]
    fox_heads = fox_f_bias.shape[1]
    fox_w = fox_heads * HEAD_DIM
    rw_w = rwkv_w0.shape[1]
    rw_cols = rwkv_mu.shape[1]
    mem_w = w_mem_kv.shape[2] // 2
    dec_lora = rwkv_w_up.shape[1]
    aaa_lora = rwkv_a_up.shape[1]
    assert dec_lora + aaa_lora == LANES and dec_lora == aaa_lora
    assert fox_heads <= SUBLANES and fox_w % LANES == 0 and rw_w % LANES == 0
    row2 = lambda v: v.reshape(1, -1)

    h = x.reshape(batch * seq, d)
    mem2 = mem.reshape(batch * mem_len, d)
    bd_ones = _block_diag_ones(rw_w, HEAD_DIM, 1.0)
    bd_avg = _block_diag_ones(rw_w, HEAD_DIM, 1.0 / HEAD_DIM)
    for l in range(depth):
        wi = w_in[l]
        o_ff = 3 * fox_w
        o_rw = o_ff + fox_heads
        o_mq = o_rw + rw_cols
        o_gt = o_mq + mem_w
        scale = HEAD_DIM ** -0.5 * LOG2E
        sections = [("qkv", jnp.concatenate([wi[:, :fox_w] * scale, wi[:, fox_w:o_ff]], axis=1)),
                    ("ff", jnp.pad(wi[:, o_ff:o_rw], ((0, 0), (0, LANES - fox_heads)))),
                    ("rwkv", wi[:, o_rw:o_mq]), ("memq", wi[:, o_mq:o_gt]), ("gate", wi[:, o_gt:])]
        w_all = jnp.concatenate([sec for _, sec in sections], axis=1).astype(BF16)
        cols, start = {}, 0
        for name, sec in sections:
            assert sec.shape[1] % LANES == 0
            cols[name] = (start, start + sec.shape[1])
            start += sec.shape[1]
        fbias = jnp.pad(row2(fox_f_bias[l]), ((0, 0), (0, LANES - fox_heads)))
        wlora = jnp.zeros((LANES, 2 * rw_w), F32)
        wlora = wlora.at[:dec_lora, :rw_w].set(rwkv_w_up[l]).at[dec_lora:, rw_w:].set(rwkv_a_up[l]).astype(BF16)

        qkv, crow, mq, gates, *scan_in = _inproj(
            h, row2(pre1_g[l]), w_all, cols, fbias, row2(rwkv_mu[l]), row2(rwkv_w0[l]), row2(rwkv_a0[l]),
            row2(rwkv_k_k[l]), row2(rwkv_k_a[l]), row2(rwkv_r_k[l]), wlora, rwkv_g_up[l].astype(BF16), bd_ones,
            seq=seq, mem_scale=(mem_w // MEM_HEADS) ** -0.5)

        fox_out = _fox(qkv, crow, batch=batch, seq=seq)

        memkv = _memkv(mem2, row2(mem_norm_g[l]), w_mem_kv[l].astype(BF16))
        rwkv_out = _rwkv_scan(*scan_in, row2(rwkv_gn_g[l]), row2(rwkv_gn_b[l]), bd_avg, batch=batch, seq=seq)

        h = _merge(h, fox_out, rwkv_out, mq, memkv, gates, w_fox_out[l].astype(BF16),
                   w_rwkv_out[l].astype(BF16), w_mem_out[l].astype(BF16), w_o[l].astype(BF16),
                   row2(post1_g[l]), seq=seq, mem_len=mem_len)
        h = _ffn(h, row2(pre2_g[l]), w_ffn_gate[l].astype(BF16), w_ffn_up[l].astype(BF16),
                 w_ffn_down[l].astype(BF16), row2(post2_g[l]))
    return h.reshape(batch, seq, d).astype(x.dtype)
```

```python
import functools

import jax
import jax.numpy as jnp
from jax import lax
from jax.experimental import pallas as pl
from jax.experimental.pallas import tpu as pltpu

F32 = jnp.float32
BF16 = jnp.bfloat16

NORM_EPS = 1e-6
GN_EPS = 64e-5
HEAD_DIM = 64
LANES = 128
SUBLANES = 8
LOG2E = 1.4426950408889634
MEM_HEADS = 4
CHUNK = 64
PAIR = 2 * CHUNK
F32_TERMS = 3
DECAY_SUM_TERMS = 2
HEAD_SUM_TERMS = 1
SUM_BLOCK = 256
INPROJ_ROWS = 256
SCAN_ROWS = 512
NEG_BIG = -1e30
VMEM_LIMIT = 56 * 1024 * 1024


def _dot(a, b):
    return jnp.dot(a, b, preferred_element_type=F32)


def _dot_nt(a, b):
    return lax.dot_general(a, b, (((1,), (1,)), ((), ())), preferred_element_type=F32)


def _dot_tn(a, b):
    return lax.dot_general(a, b, (((0,), (0,)), ((), ())), preferred_element_type=F32)


def _bf16_terms(x, n):
    terms = []
    for _ in range(n - 1):
        t = x.astype(BF16)
        terms.append(t)
        x = x - t.astype(F32)
    return terms + [x.astype(BF16)]


def _dot_terms_lhs(m_bf16, x, n):
    return sum(_dot(m_bf16, t) for t in _bf16_terms(x, n))


def _dot_terms_rhs(x, m_bf16, n):
    return sum(_dot(t, m_bf16) for t in _bf16_terms(x, n))


def _rms(xf, g):
    return xf * lax.rsqrt(jnp.mean(xf * xf, axis=-1, keepdims=True) + NORM_EPS) * g


def _sigmoid(x):
    return 1.0 / (1.0 + jnp.exp(-x))


def _log_sigmoid(x):
    return jnp.minimum(x, 0.0) - jnp.log(1.0 + jnp.exp(-jnp.abs(x)))


def _const_spec(shape):
    nd = len(shape)
    return pl.BlockSpec(shape, lambda *_: (0,) * nd, pipeline_mode=pl.Buffered(1))


def _params(n_axes):
    return pltpu.CompilerParams(dimension_semantics=("arbitrary",) * n_axes,
                                vmem_limit_bytes=VMEM_LIMIT)


def _inproj_kernel(x_ref, g_ref, w_ref, fb_ref, mu_ref, w0_ref, a0_ref, kk_ref, ka_ref, rk_ref,
                   wlora_ref, gup_ref, bd_ref,
                   qkv_ref, crow_ref, mq_ref, gate_ref,
                   at_ref, rt_ref, bt_ref, kt_ref, bh_ref, kh_ref, v_ref, gam_ref, bonus_ref, g_out_ref,
                   carry_ref, prev_ref, *, tiles_per_seq, mem_scale, cols):
    i = pl.program_id(0)
    tm = x_ref.shape[0]
    w = v_ref.shape[1]

    @pl.when(i % tiles_per_seq == 0)
    def _():
        carry_ref[...] = jnp.zeros_like(carry_ref)
        prev_ref[...] = jnp.zeros_like(prev_ref)

    def proj(name, lo=0, hi=None):
        start, stop = cols[name]
        return _dot(u, w_ref[:, start + lo:(stop if hi is None else start + hi)])

    u = _rms(x_ref[...], g_ref[...]).astype(BF16)

    p = proj("rwkv")
    rowid = lax.broadcasted_iota(jnp.int32, p.shape, 0)
    shifted = jnp.where(rowid == 0, prev_ref[...], pltpu.roll(p, 1, axis=0))
    prev_ref[...] = p[tm - 1:tm, :]
    xs = p + (shifted - p) * mu_ref[...]
    r = xs[:, 0:w]
    k = xs[:, w:2 * w]
    v = xs[:, 2 * w:3 * w]
    lora_in = xs[:, 3 * w:3 * w + LANES]
    gd = xs[:, 3 * w + LANES:]
    lane = lax.broadcasted_iota(jnp.int32, lora_in.shape, 1)
    lora_in = jnp.where(lane < lora_in.shape[1] // 2, jnp.tanh(lora_in), lora_in)
    qkv_ref[...] = proj("qkv").astype(BF16)
    lora = _dot(lora_in.astype(BF16), wlora_ref[...])
    g_out_ref[...] = _dot(_sigmoid(gd).astype(BF16), gup_ref[...])
    w_log = -jnp.exp(_log_sigmoid(w0_ref[...] + lora[:, :w]) - 0.5)
    a = _sigmoid(a0_ref[...] + lora[:, w:])

    d = gate_ref.shape[1] // 3
    gate_ref[:, 0:d] = _sigmoid(proj("gate", 0, d)).astype(BF16)

    bd = bd_ref[...]
    kk = k * kk_ref[...]
    kk = kk * lax.rsqrt(jnp.maximum(_dot_terms_rhs(kk * kk, bd, HEAD_SUM_TERMS), 1e-24))
    k2 = k * (1.0 + (a - 1.0) * ka_ref[...])
    bonus_ref[...] = _dot_terms_rhs(r * k2 * rk_ref[...], bd, HEAD_SUM_TERMS) * v

    sb = min(tm, SUM_BLOCK)
    ti = lax.broadcasted_iota(jnp.int32, (sb, sb), 0)
    si = lax.broadcasted_iota(jnp.int32, (sb, sb), 1)
    same = (ti // CHUNK) == (si // CHUNK)
    lower = jnp.where(same & (si <= ti), 1.0, 0.0).astype(BF16)
    upper = jnp.where(same & (si > ti), 1.0, 0.0).astype(BF16)
    w_terms = _bf16_terms(w_log, DECAY_SUM_TERMS)
    blocks = [slice(s0, s0 + sb) for s0 in range(0, tm, sb)]
    cum = jnp.concatenate([sum(_dot(lower, t[blk]) for t in w_terms) for blk in blocks], axis=0)
    rem = jnp.concatenate([sum(_dot(upper, t[blk]) for t in w_terms) for blk in blocks], axis=0)
    n_sel = gam_ref.shape[1]
    ci = lax.broadcasted_iota(jnp.int32, (n_sel, tm), 0)
    sj = lax.broadcasted_iota(jnp.int32, (n_sel, tm), 1)
    sel = jnp.where(ci == sj // CHUNK, 1.0, 0.0).astype(BF16)
    gam_ref[0] = jnp.exp(sum(_dot(sel, t) for t in w_terms))

    gate_ref[:, d:2 * d] = _sigmoid(proj("gate", d, 2 * d)).astype(BF16)

    e_prev = jnp.exp(cum - w_log)
    e_cum = jnp.exp(cum)
    e_neg = jnp.exp(-cum)
    e_rem = jnp.exp(rem)
    b = kk * a
    at_ref[...] = (-kk * e_prev).astype(BF16)
    rt_ref[...] = (r * e_cum).astype(BF16)
    bt_ref[...] = (b * e_neg).astype(BF16)
    kt_ref[...] = (k2 * e_neg).astype(BF16)
    bh_ref[...] = (b * e_rem).astype(BF16)
    kh_ref[...] = (k2 * e_rem).astype(BF16)
    v_ref[...] = v.astype(BF16)

    gate_ref[:, 2 * d:3 * d] = _sigmoid(proj("gate", 2 * d, 3 * d)).astype(BF16)
    mq_ref[...] = (proj("memq") * mem_scale).astype(BF16)

    ls = _log_sigmoid(proj("ff") + fb_ref[...])
    tri = jnp.where(lax.broadcasted_iota(jnp.int32, (tm, tm), 0) >= lax.broadcasted_iota(jnp.int32, (tm, tm), 1),
                    1.0, 0.0).astype(BF16)
    c = _dot_terms_lhs(tri, ls, F32_TERMS) + carry_ref[...]
    carry_ref[...] = c[tm - 1:tm, :]
    crow_ref[...] = jnp.transpose(c * LOG2E)[:crow_ref.shape[0], :]


def _inproj(x2, g, w_all, cols, fbias, mu, w0, a0, k_k, k_a, r_k, wlora, gup, bd, *, seq, mem_scale,
            tm=INPROJ_ROWS):
    t, d = x2.shape
    w = w0.shape[1]
    n_tiles = t // tm
    width = lambda name: cols[name][1] - cols[name][0]
    tok = lambda n, dt: jax.ShapeDtypeStruct((t, n), dt)
    n_sel = max(SUBLANES, tm // CHUNK)
    outs = ([tok(width("qkv"), BF16), jax.ShapeDtypeStruct((SUBLANES, t), F32), tok(width("memq"), BF16),
             tok(width("gate"), BF16)] + [tok(w, BF16)] * 7
            + [jax.ShapeDtypeStruct((n_tiles, n_sel, w), F32), tok(w, F32), tok(w, F32)])
    row = lambda n: pl.BlockSpec((tm, n), lambda i: (i, 0))
    consts = (g, w_all, fbias, mu, w0, a0, k_k, k_a, r_k, wlora, gup, bd)
    gam_index = 11
    results = list(pl.pallas_call(
        functools.partial(_inproj_kernel, tiles_per_seq=seq // tm, mem_scale=mem_scale, cols=cols),
        grid=(n_tiles,),
        in_specs=[row(d)] + [_const_spec(c.shape) for c in consts],
        out_specs=([row(width("qkv")), pl.BlockSpec((SUBLANES, tm), lambda i: (0, i)), row(width("memq")),
                    row(width("gate"))] + [row(w)] * 7
                   + [pl.BlockSpec((1, n_sel, w), lambda i: (i, 0, 0)), row(w), row(w)]),
        out_shape=outs,
        scratch_shapes=[pltpu.VMEM((1, LANES), F32), pltpu.VMEM((1, cols["rwkv"][1] - cols["rwkv"][0]), F32)],
        compiler_params=_params(1),
        name="inproj",
    )(x2, *consts))
    results[gam_index] = results[gam_index][:, :tm // CHUNK]
    return results


def _fox_kernel(q_ref, k_ref, v_ref, crow_ref, o_ref, *, bk, rb, ahead):
    hp = pl.program_id(1)
    seq = q_ref.shape[0]
    first_q = lax.broadcasted_iota(jnp.int32, (rb, LANES), 1) < HEAD_DIM
    lower = (lax.broadcasted_iota(jnp.int32, (rb, rb), 0) >= lax.broadcasted_iota(jnp.int32, (rb, rb), 1))
    sub = lax.broadcasted_iota(jnp.int32, crow_ref.shape, 0)
    crow_all = crow_ref[...]
    crow = [jnp.sum(jnp.where(sub == 2 * hp + e, crow_all, 0.0), axis=0, keepdims=True) for e in range(2)]

    units = []
    for r in range(seq // rb):
        stop = (r + 1) * rb
        units += [(r, k0, min(bk, stop - k0)) for k0 in range(0, stop, bk)]
    units.sort(key=lambda u: (u[0] * rb // bk, u[1], u[0]))

    scores, state = {}, {}

    def issue_scores(u):
        r, k0, width = u
        q2 = q_ref[r * rb:(r + 1) * rb, :]
        zero = jnp.zeros_like(q2)
        kt = k_ref[k0:k0 + width, :]
        scores[u, 0] = _dot_nt(jnp.where(first_q, q2, zero), kt)
        scores[u, 1] = _dot_nt(jnp.where(first_q, zero, q2), kt)

    def finish(u):
        r, k0, width = u
        vt = v_ref[k0:k0 + width, :]
        one = jnp.ones_like(vt)
        first_k = lax.broadcasted_iota(jnp.int32, vt.shape, 1) < HEAD_DIM
        on_diagonal = k0 + width == (r + 1) * rb
        for e in range(2):
            v_aug = jnp.where(first_k, vt, one) if e == 0 else jnp.where(first_k, one, vt)
            s = scores.pop((u, e)) - crow[e][:, k0:k0 + width]
            if on_diagonal:
                tail = jnp.where(lower, s[:, width - rb:], NEG_BIG)
                s = tail if width == rb else jnp.concatenate([s[:, :width - rb], tail], axis=1)
            m_tile = jnp.max(s, axis=-1, keepdims=True)
            if k0 == 0:
                m_new = m_tile
                acc = _dot(jnp.exp2(s - m_new).astype(BF16), v_aug)
            else:
                m_old, acc_old = state[r, e]
                m_new = jnp.maximum(m_old, m_tile)
                acc = jnp.exp2(m_old - m_new) * acc_old + _dot(jnp.exp2(s - m_new).astype(BF16), v_aug)
            state[r, e] = (m_new, acc)
        if on_diagonal:
            a0, a1 = state.pop((r, 0))[1], state.pop((r, 1))[1]
            den = jnp.where(first_q, pltpu.roll(a0, HEAD_DIM, axis=1), pltpu.roll(a1, HEAD_DIM, axis=1))
            o_ref[r * rb:(r + 1) * rb, :] = (jnp.where(first_q, a0, a1) / den).astype(o_ref.dtype)

    for u in units[:ahead]:
        issue_scores(u)
    for n, u in enumerate(units):
        if n + ahead < len(units):
            issue_scores(units[n + ahead])
        finish(u)


def _fox(qkv, crow, *, batch, seq, bk=512, rb=128, ahead=2):
    t = qkv.shape[0]
    n_pairs = qkv.shape[1] // (3 * LANES)
    return pl.pallas_call(
        functools.partial(_fox_kernel, bk=bk, rb=rb, ahead=ahead),
        grid=(batch, n_pairs),
        in_specs=[
            pl.BlockSpec((seq, LANES), lambda b, h: (b, h)),
            pl.BlockSpec((seq, LANES), lambda b, h: (b, n_pairs + h)),
            pl.BlockSpec((seq, LANES), lambda b, h: (b, 2 * n_pairs + h)),
            pl.BlockSpec((crow.shape[0], seq), lambda b, h: (0, b)),
        ],
        out_specs=pl.BlockSpec((seq, LANES), lambda b, h: (b, h)),
        out_shape=jax.ShapeDtypeStruct((t, n_pairs * LANES), BF16),
        compiler_params=_params(2),
        name="fox_attention",
    )(qkv, qkv, qkv, crow)


def _memkv_kernel(m_ref, g_ref, w_ref, o_ref):
    o_ref[...] = _dot(_rms(m_ref[...], g_ref[...]).astype(BF16), w_ref[...]).astype(BF16)


def _memkv(mem2, g, w, tm=512):
    t, d = mem2.shape
    return pl.pallas_call(
        _memkv_kernel,
        grid=(t // tm,),
        in_specs=[pl.BlockSpec((tm, d), lambda i: (i, 0)), _const_spec(g.shape), _const_spec(w.shape)],
        out_specs=pl.BlockSpec((tm, w.shape[1]), lambda i: (i, 0)),
        out_shape=jax.ShapeDtypeStruct((t, w.shape[1]), BF16),
        compiler_params=_params(1),
        name="mem_kv",
    )(mem2, g, w)


def _rwkv_scan_kernel(at_ref, rt_ref, bt_ref, kt_ref, bh_ref, kh_ref, v_ref, gam_ref, bonus_ref, g_ref,
                      gng_ref, gnb_ref, bdavg_ref, o_ref, state_ref, y_ref):
    @pl.when(pl.program_id(1) == 0)
    def _():
        state_ref[...] = jnp.zeros_like(state_ref)

    n_t, w = v_ref.shape
    n_pairs = w // LANES
    n_chunks = n_t // CHUNK
    lane = lax.broadcasted_iota(jnp.int32, (CHUNK, LANES), 1)
    first = lane < HEAD_DIM
    ri = lax.broadcasted_iota(jnp.int32, (PAIR, PAIR), 0)
    ci = lax.broadcasted_iota(jnp.int32, (PAIR, PAIR), 1)
    strict = ri > ci
    eye = jnp.where(ri == ci, 1.0, 0.0)
    incl2 = (lax.broadcasted_iota(jnp.int32, (PAIR, 2 * PAIR), 0)
             >= (lax.broadcasted_iota(jnp.int32, (PAIR, 2 * PAIR), 1) & (PAIR - 1)))

    def stacked(ref, c, p):
        x = ref[c * CHUNK:(c + 1) * CHUNK, p * LANES:(p + 1) * LANES]
        z = jnp.zeros_like(x)
        return jnp.concatenate([jnp.where(first, x, z), jnp.where(first, z, x)], axis=0)

    pairs = range(n_pairs)
    states = [state_ref[p] for p in pairs]
    p_mat, d_mat, m_t, c_t = {}, {}, {}, {}

    def chain_step(c):
        h_bf = [states[p].astype(BF16) for p in pairs]
        for p in pairs:
            y_s = _dot_nt(p_mat.pop((c, p)), h_bf[p]) + d_mat.pop((c, p))
            gam = gam_ref[0, c:c + 1, p * LANES:(p + 1) * LANES]
            states[p] = states[p] * gam + _dot(h_bf[p], m_t.pop((c, p))) + c_t.pop((c, p))
            y_ref[c * CHUNK:(c + 1) * CHUNK, p * LANES:(p + 1) * LANES] = y_s[:CHUNK] + y_s[CHUNK:]

    def chunk_stages(cps):
        gram, s_inv, q_pow, a_ak, a_rbk, x_rhs, sx = {}, {}, {}, {}, {}, {}, {}
        for cp in cps:
            lhs = jnp.concatenate([stacked(at_ref, *cp), stacked(rt_ref, *cp)], axis=0)
            rhs = jnp.concatenate([stacked(bt_ref, *cp), stacked(kt_ref, *cp)], axis=0)
            gram[cp] = _dot_nt(lhs, rhs)
        yield
        for cp in cps:
            g_all = gram.pop(cp)
            n_ab = jnp.where(strict, g_all[:PAIR, :PAIR], 0.0)
            a_ak[cp] = jnp.where(strict, g_all[:PAIR, PAIR:], 0.0).astype(BF16)
            a_rbk[cp] = jnp.where(incl2, g_all[PAIR:, :], 0.0).astype(BF16)
            s_inv[cp] = eye + n_ab
            q_pow[cp] = n_ab.astype(BF16)
        for cp in cps:
            q_pow[cp] = _dot(q_pow[cp], q_pow[cp]).astype(BF16)
            x_rhs[cp] = _dot(a_ak.pop(cp), stacked(v_ref, *cp))
        yield
        span = 2
        while 2 * span < CHUNK:
            for cp in cps:
                both = _dot(q_pow[cp], jnp.concatenate([s_inv[cp].astype(BF16), q_pow[cp]], axis=1))
                s_inv[cp] = s_inv[cp] + both[:, :PAIR]
                q_pow[cp] = both[:, PAIR:].astype(BF16)
            span *= 2
            yield
        for cp in cps:
            s_inv[cp] = (s_inv[cp] + _dot(q_pow.pop(cp), s_inv[cp].astype(BF16))).astype(BF16)
        yield
        for cp in cps:
            x = jnp.concatenate([stacked(at_ref, *cp), x_rhs.pop(cp).astype(BF16)], axis=1)
            sx[cp] = _dot(s_inv.pop(cp), x).astype(BF16)
        yield
        for cp in cps:
            zero = jnp.zeros((PAIR, PAIR), BF16)
            lower_rows = jnp.concatenate([zero, stacked(v_ref, *cp)], axis=1)
            full = jnp.concatenate([sx.pop(cp), lower_rows], axis=0)
            pd = _dot(a_rbk.pop(cp), full)
            p_mat[cp] = (pd[:, :PAIR] + stacked(rt_ref, *cp).astype(F32)).astype(BF16)
            d_mat[cp] = pd[:, PAIR:]
            mc = _dot_tn(full, jnp.concatenate([stacked(bh_ref, *cp), stacked(kh_ref, *cp)], axis=0))
            m_t[cp] = mc[:PAIR].astype(BF16)
            c_t[cp] = mc[PAIR:]
        yield

    group = max(1, n_chunks // 2)
    pending = []
    for c0 in range(0, n_chunks, group):
        chunks = range(c0, min(c0 + group, n_chunks))
        for _ in chunk_stages([(c, p) for c in chunks for p in pairs]):
            if pending:
                chain_step(pending.pop(0))
        pending += list(chunks)
    for c in pending:
        chain_step(c)
    for p in pairs:
        state_ref[p] = states[p]

    y = y_ref[...]
    bdavg = bdavg_ref[...]
    dev = y - _dot_terms_rhs(y, bdavg, HEAD_SUM_TERMS + 1)
    var = _dot_terms_rhs(dev * dev, bdavg, HEAD_SUM_TERMS)
    yn = dev * lax.rsqrt(var + GN_EPS) * gng_ref[...] + gnb_ref[...]
    o_ref[...] = ((yn + bonus_ref[...]) * g_ref[...]).astype(o_ref.dtype)


def _rwkv_scan(at, rt, bt, kt, bh, kh, v, gam, bonus, g, gn_g, gn_b, bdavg, *, batch, seq, tl=SCAN_ROWS):
    t, w = v.shape
    n_sel = tl // CHUNK
    gam = gam.reshape(t // tl, n_sel, w)
    nt = seq // tl
    tok = pl.BlockSpec((tl, w), lambda b, i: (b * nt + i, 0))
    return pl.pallas_call(
        _rwkv_scan_kernel,
        grid=(batch, nt),
        in_specs=[tok] * 7 + [pl.BlockSpec((1, n_sel, w), lambda b, i: (b * nt + i, 0, 0)), tok, tok,
                              _const_spec(gn_g.shape), _const_spec(gn_b.shape), _const_spec(bdavg.shape)],
        out_specs=tok,
        out_shape=jax.ShapeDtypeStruct((t, w), BF16),
        scratch_shapes=[pltpu.VMEM((w // LANES, LANES, LANES), F32), pltpu.VMEM((tl, w), F32)],
        compiler_params=_params(2),
        name="rwkv_scan",
    )(at, rt, bt, kt, bh, kh, v, gam, bonus, g, gn_g, gn_b, bdavg)


def _merge_kernel(x_ref, fox_ref, rw_ref, mq_ref, kv_ref, gate_ref, wf_ref, wr_ref, wm_ref, wo_ref, g_ref, o_ref):
    d = x_ref.shape[1]
    w = mq_ref.shape[1]
    hd = w // MEM_HEADS
    heads = range(MEM_HEADS)
    scores = [_dot_nt(mq_ref[:, h * hd:(h + 1) * hd], kv_ref[:, h * hd:(h + 1) * hd]) for h in heads]
    merged = (gate_ref[:, 0:d].astype(F32) * _dot(fox_ref[...], wf_ref[...])
              + gate_ref[:, d:2 * d].astype(F32) * _dot(rw_ref[...], wr_ref[...]))
    probs, norms = [], []
    for h in heads:
        p = jnp.exp(scores[h] - jnp.max(scores[h], axis=-1, keepdims=True))
        norms.append(jnp.sum(p, axis=-1, keepdims=True))
        probs.append(p.astype(BF16))
    mem_out = jnp.concatenate(
        [(_dot(probs[h], kv_ref[:, w + h * hd:w + (h + 1) * hd]) / norms[h]).astype(BF16) for h in heads], axis=1)
    merged = merged + gate_ref[:, 2 * d:3 * d].astype(F32) * _dot(mem_out, wm_ref[...])
    y = _dot(merged.astype(BF16), wo_ref[...])
    o_ref[...] = x_ref[...] + _rms(y, g_ref[...])


def _merge(x2, fox, rw, mq, memkv, gates, wf, wr, wm, wo, g, *, seq, mem_len, tm=1024):
    t, d = x2.shape
    assert seq % tm == 0
    tiles_per_seq = seq // tm
    row = lambda width: pl.BlockSpec((tm, width), lambda i: (i, 0))
    return pl.pallas_call(
        _merge_kernel,
        grid=(t // tm,),
        in_specs=[row(d), row(fox.shape[1]), row(rw.shape[1]), row(mq.shape[1]),
                  pl.BlockSpec((mem_len, memkv.shape[1]), lambda i: (i // tiles_per_seq, 0)), row(gates.shape[1]),
                  _const_spec(wf.shape), _const_spec(wr.shape), _const_spec(wm.shape), _const_spec(wo.shape),
                  _const_spec(g.shape)],
        out_specs=row(d),
        out_shape=jax.ShapeDtypeStruct((t, d), F32),
        compiler_params=_params(1),
        name="merge_out",
    )(x2, fox, rw, mq, memkv, gates, wf, wr, wm, wo, g)


def _ffn_kernel(h_ref, g1_ref, wg_ref, wu_ref, wd_ref, g2_ref, o_ref, *, ff_chunk):
    h = h_ref[...]
    u = _rms(h, g1_ref[...]).astype(BF16)
    d_ff = wg_ref.shape[1]
    acc = jnp.zeros(h.shape, F32)
    for lo in range(0, d_ff, ff_chunk):
        hi = min(lo + ff_chunk, d_ff)
        gt = _dot(u, wg_ref[:, lo:hi])
        up = _dot(u, wu_ref[:, lo:hi])
        act = (gt * _sigmoid(gt) * up).astype(BF16)
        acc = acc + _dot(act, wd_ref[lo:hi, :])
    o_ref[...] = h + _rms(acc, g2_ref[...])


def _ffn(h2, g1, wg, wu, wd, g2, tm=1024, ff_chunk=1024):
    t, d = h2.shape
    row = pl.BlockSpec((tm, d), lambda i: (i, 0))
    return pl.pallas_call(
        functools.partial(_ffn_kernel, ff_chunk=ff_chunk),
        grid=(t // tm,),
        in_specs=[row, _const_spec(g1.shape), _const_spec(wg.shape), _const_spec(wu.shape),
                  _const_spec(wd.shape), _const_spec(g2.shape)],
        out_specs=row,
        out_shape=jax.ShapeDtypeStruct((t, d), F32),
        compiler_params=_params(1),
        name="ffn",
    )(h2, g1, wg, wu, wd, g2)


def _block_diag_ones(width, block, value):
    idx = jnp.arange(width) // block
    return jnp.where(idx[:, None] == idx[None, :], value, 0.0).astype(BF16)


def kernel(x, mem, pre1_g, post1_g, pre2_g, post2_g, mem_norm_g, w_in, fox_f_bias, rwkv_mu, rwkv_w0,
           rwkv_w_up, rwkv_a0, rwkv_a_up, rwkv_g_up, rwkv_k_k, rwkv_k_a, rwkv_r_k, rwkv_gn_g, rwkv_gn_b,
           w_mem_kv, w_fox_out, w_rwkv_out, w_mem_out, w_o, w_ffn_gate, w_ffn_up, w_ffn_down):
    batch, seq, d = x.shape
    mem_len = mem.shape[1]
    depth = w_in.shape[0]
    fox_heads = fox_f_bias.shape[1]
    fox_w = fox_heads * HEAD_DIM
    rw_w = rwkv_w0.shape[1]
    rw_cols = rwkv_mu.shape[1]
    mem_w = w_mem_kv.shape[2] // 2
    dec_lora = rwkv_w_up.shape[1]
    aaa_lora = rwkv_a_up.shape[1]
    assert dec_lora + aaa_lora == LANES and dec_lora == aaa_lora
    assert fox_heads <= SUBLANES and fox_w % LANES == 0 and rw_w % LANES == 0
    row2 = lambda v: v.reshape(1, -1)

    h = x.reshape(batch * seq, d)
    mem2 = mem.reshape(batch * mem_len, d)
    bd_ones = _block_diag_ones(rw_w, HEAD_DIM, 1.0)
    bd_avg = _block_diag_ones(rw_w, HEAD_DIM, 1.0 / HEAD_DIM)
    for l in range(depth):
        wi = w_in[l]
        o_ff = 3 * fox_w
        o_rw = o_ff + fox_heads
        o_mq = o_rw + rw_cols
        o_gt = o_mq + mem_w
        scale = HEAD_DIM ** -0.5 * LOG2E
        sections = [("qkv", jnp.concatenate([wi[:, :fox_w] * scale, wi[:, fox_w:o_ff]], axis=1)),
                    ("ff", jnp.pad(wi[:, o_ff:o_rw], ((0, 0), (0, LANES - fox_heads)))),
                    ("rwkv", wi[:, o_rw:o_mq]), ("memq", wi[:, o_mq:o_gt]), ("gate", wi[:, o_gt:])]
        w_all = jnp.concatenate([sec for _, sec in sections], axis=1).astype(BF16)
        cols, start = {}, 0
        for name, sec in sections:
            assert sec.shape[1] % LANES == 0
            cols[name] = (start, start + sec.shape[1])
            start += sec.shape[1]
        fbias = jnp.pad(row2(fox_f_bias[l]), ((0, 0), (0, LANES - fox_heads)))
        wlora = jnp.zeros((LANES, 2 * rw_w), F32)
        wlora = wlora.at[:dec_lora, :rw_w].set(rwkv_w_up[l]).at[dec_lora:, rw_w:].set(rwkv_a_up[l]).astype(BF16)

        qkv, crow, mq, gates, *scan_in = _inproj(
            h, row2(pre1_g[l]), w_all, cols, fbias, row2(rwkv_mu[l]), row2(rwkv_w0[l]), row2(rwkv_a0[l]),
            row2(rwkv_k_k[l]), row2(rwkv_k_a[l]), row2(rwkv_r_k[l]), wlora, rwkv_g_up[l].astype(BF16), bd_ones,
            seq=seq, mem_scale=(mem_w // MEM_HEADS) ** -0.5)

        fox_out = _fox(qkv, crow, batch=batch, seq=seq)

        memkv = _memkv(mem2, row2(mem_norm_g[l]), w_mem_kv[l].astype(BF16))
        rwkv_out = _rwkv_scan(*scan_in, row2(rwkv_gn_g[l]), row2(rwkv_gn_b[l]), bd_avg, batch=batch, seq=seq)

        h = _merge(h, fox_out, rwkv_out, mq, memkv, gates, w_fox_out[l].astype(BF16),
                   w_rwkv_out[l].astype(BF16), w_mem_out[l].astype(BF16), w_o[l].astype(BF16),
                   row2(post1_g[l]), seq=seq, mem_len=mem_len)
        h = _ffn(h, row2(pre2_g[l]), w_ffn_gate[l].astype(BF16), w_ffn_up[l].astype(BF16),
                 w_ffn_down[l].astype(BF16), row2(post2_g[l]))
    return h.reshape(batch, seq, d).astype(x.dtype)
```

```python
import functools

import jax
import jax.numpy as jnp
from jax import lax
from jax.experimental import pallas as pl
from jax.experimental.pallas import tpu as pltpu

F32 = jnp.float32
BF16 = jnp.bfloat16

NORM_EPS = 1e-6
GN_EPS = 64e-5
HEAD_DIM = 64
LANES = 128
SUBLANES = 8
LOG2E = 1.4426950408889634
MEM_HEADS = 4
CHUNK = 64
PAIR = 2 * CHUNK
F32_TERMS = 3
DECAY_SUM_TERMS = 2
HEAD_SUM_TERMS = 1
NEG_BIG = -1e30
VMEM_LIMIT = 56 * 1024 * 1024

INPROJ_ROWS = 256
SCAN_ROWS = 512
FOX_KEY_TILE = 512
FOX_ROW_BLOCK = 128
FOX_SCORES_AHEAD = 2
MEMKV_ROWS = 512
MERGE_ROWS = 1024
FFN_ROWS = 1024
FFN_CHUNK = 1024


def _dot(a, b):
    return jnp.dot(a, b, preferred_element_type=F32)


def _dot_nt(a, b):
    return lax.dot_general(a, b, (((1,), (1,)), ((), ())), preferred_element_type=F32)


def _dot_tn(a, b):
    return lax.dot_general(a, b, (((0,), (0,)), ((), ())), preferred_element_type=F32)


def _bf16_terms(x, n):
    terms = []
    for _ in range(n - 1):
        t = x.astype(BF16)
        terms.append(t)
        x = x - t.astype(F32)
    return terms + [x.astype(BF16)]


def _dot_terms_lhs(m_bf16, x, n):
    return sum(_dot(m_bf16, t) for t in _bf16_terms(x, n))


def _dot_terms_rhs(x, m_bf16, n):
    return sum(_dot(t, m_bf16) for t in _bf16_terms(x, n))


def _rms(xf, g):
    return xf * lax.rsqrt(jnp.mean(xf * xf, axis=-1, keepdims=True) + NORM_EPS) * g


def _sigmoid(x):
    return 1.0 / (1.0 + jnp.exp(-x))


def _log_sigmoid(x):
    return jnp.minimum(x, 0.0) - jnp.log(1.0 + jnp.exp(-jnp.abs(x)))


def _const_spec(shape):
    nd = len(shape)
    return pl.BlockSpec(shape, lambda *_: (0,) * nd, pipeline_mode=pl.Buffered(1))


def _params(n_axes):
    return pltpu.CompilerParams(dimension_semantics=("arbitrary",) * n_axes,
                                vmem_limit_bytes=VMEM_LIMIT)


def _sum_matrices(tm):
    t = jnp.arange(tm)[:, None]
    s = jnp.arange(tm)[None, :]
    same = (t // CHUNK) == (s // CHUNK)
    stacked = jnp.stack([same & (s <= t), same & (s > t), s <= t]).astype(BF16)
    n_sel = max(SUBLANES, tm // CHUNK)
    selector = (jnp.arange(n_sel)[:, None] == s // CHUNK).astype(BF16)
    return stacked, selector


def _inproj_kernel(x_ref, g_ref, w_ref, fb_ref, mu_ref, w0_ref, a0_ref, kk_ref, ka_ref, rk_ref,
                   wlora_ref, gup_ref, bd_ref, sums_ref, sel_ref,
                   qkv_ref, crow_ref, mq_ref, gate_ref,
                   at_ref, rt_ref, bt_ref, kt_ref, bh_ref, kh_ref, v_ref, gam_ref, bonus_ref, g_out_ref,
                   carry_ref, prev_ref, *, tiles_per_seq, mem_scale, cols):
    i = pl.program_id(0)
    tm = x_ref.shape[0]
    w = v_ref.shape[1]

    @pl.when(i % tiles_per_seq == 0)
    def _():
        carry_ref[...] = jnp.zeros_like(carry_ref)
        prev_ref[...] = jnp.zeros_like(prev_ref)

    def proj(name, lo=0, hi=None):
        start, stop = cols[name]
        return _dot(u, w_ref[:, start + lo:(stop if hi is None else start + hi)])

    u = _rms(x_ref[...], g_ref[...]).astype(BF16)

    p = proj("rwkv")
    rowid = lax.broadcasted_iota(jnp.int32, p.shape, 0)
    shifted = jnp.where(rowid == 0, prev_ref[...], pltpu.roll(p, 1, axis=0))
    prev_ref[...] = p[tm - 1:tm, :]
    xs = p + (shifted - p) * mu_ref[...]
    r = xs[:, 0:w]
    k = xs[:, w:2 * w]
    v = xs[:, 2 * w:3 * w]
    lora_in = xs[:, 3 * w:3 * w + LANES]
    gd = xs[:, 3 * w + LANES:]
    lane = lax.broadcasted_iota(jnp.int32, lora_in.shape, 1)
    lora_in = jnp.where(lane < lora_in.shape[1] // 2, jnp.tanh(lora_in), lora_in)
    qkv_ref[...] = proj("qkv").astype(BF16)
    lora = _dot(lora_in.astype(BF16), wlora_ref[...])
    g_out_ref[...] = _dot(_sigmoid(gd).astype(BF16), gup_ref[...])
    w_log = -jnp.exp(_log_sigmoid(w0_ref[...] + lora[:, :w]) - 0.5)
    a = _sigmoid(a0_ref[...] + lora[:, w:])

    d = gate_ref.shape[1] // 3
    gate_ref[:, 0:d] = _sigmoid(proj("gate", 0, d)).astype(BF16)

    bd = bd_ref[...]
    kk = k * kk_ref[...]
    kk = kk * lax.rsqrt(jnp.maximum(_dot_terms_rhs(kk * kk, bd, HEAD_SUM_TERMS), 1e-24))
    k2 = k * (1.0 + (a - 1.0) * ka_ref[...])
    bonus_ref[...] = _dot_terms_rhs(r * k2 * rk_ref[...], bd, HEAD_SUM_TERMS) * v

    lower, upper = sums_ref[0], sums_ref[1]
    w_terms = _bf16_terms(w_log, DECAY_SUM_TERMS)
    cum = sum(_dot(lower, t) for t in w_terms)
    rem = sum(_dot(upper, t) for t in w_terms)
    gam_ref[0] = jnp.exp(sum(_dot(sel_ref[...], t) for t in w_terms))

    gate_ref[:, d:2 * d] = _sigmoid(proj("gate", d, 2 * d)).astype(BF16)

    e_prev = jnp.exp(cum - w_log)
    e_cum = jnp.exp(cum)
    e_neg = jnp.exp(-cum)
    e_rem = jnp.exp(rem)
    b = kk * a
    at_ref[...] = (-kk * e_prev).astype(BF16)
    rt_ref[...] = (r * e_cum).astype(BF16)
    bt_ref[...] = (b * e_neg).astype(BF16)
    kt_ref[...] = (k2 * e_neg).astype(BF16)
    bh_ref[...] = (b * e_rem).astype(BF16)
    kh_ref[...] = (k2 * e_rem).astype(BF16)
    v_ref[...] = v.astype(BF16)

    gate_ref[:, 2 * d:3 * d] = _sigmoid(proj("gate", 2 * d, 3 * d)).astype(BF16)
    mq_ref[...] = (proj("memq") * mem_scale).astype(BF16)

    ls = _log_sigmoid(proj("ff") + fb_ref[...])
    c = _dot_terms_lhs(sums_ref[2], ls, F32_TERMS) + carry_ref[...]
    carry_ref[...] = c[tm - 1:tm, :]
    crow_ref[...] = jnp.transpose(c * LOG2E)[:crow_ref.shape[0], :]


def _inproj(x2, g, w_all, cols, fbias, mu, w0, a0, k_k, k_a, r_k, wlora, gup, bd, *, seq, mem_scale,
            tm=INPROJ_ROWS):
    t, d = x2.shape
    w = w0.shape[1]
    n_tiles = t // tm
    width = lambda name: cols[name][1] - cols[name][0]
    tok = lambda n, dt: jax.ShapeDtypeStruct((t, n), dt)
    sums, selector = _sum_matrices(tm)
    n_sel = selector.shape[0]
    outs = ([tok(width("qkv"), BF16), jax.ShapeDtypeStruct((SUBLANES, t), F32), tok(width("memq"), BF16),
             tok(width("gate"), BF16)] + [tok(w, BF16)] * 7
            + [jax.ShapeDtypeStruct((n_tiles, n_sel, w), F32), tok(w, F32), tok(w, F32)])
    row = lambda n: pl.BlockSpec((tm, n), lambda i: (i, 0))
    consts = (g, w_all, fbias, mu, w0, a0, k_k, k_a, r_k, wlora, gup, bd, sums, selector)
    gam_index = 11
    results = list(pl.pallas_call(
        functools.partial(_inproj_kernel, tiles_per_seq=seq // tm, mem_scale=mem_scale, cols=cols),
        grid=(n_tiles,),
        in_specs=[row(d)] + [_const_spec(c.shape) for c in consts],
        out_specs=([row(width("qkv")), pl.BlockSpec((SUBLANES, tm), lambda i: (0, i)), row(width("memq")),
                    row(width("gate"))] + [row(w)] * 7
                   + [pl.BlockSpec((1, n_sel, w), lambda i: (i, 0, 0)), row(w), row(w)]),
        out_shape=outs,
        scratch_shapes=[pltpu.VMEM((1, LANES), F32), pltpu.VMEM((1, cols["rwkv"][1] - cols["rwkv"][0]), F32)],
        compiler_params=_params(1),
        name="inproj",
    )(x2, *consts))
    results[gam_index] = results[gam_index][:, :tm // CHUNK]
    return results


def _fox_kernel(q_ref, k_ref, v_ref, crow_ref, o_ref, *, bk, rb, ahead):
    hp = pl.program_id(1)
    seq = q_ref.shape[0]
    first_q = lax.broadcasted_iota(jnp.int32, (rb, LANES), 1) < HEAD_DIM
    lower = (lax.broadcasted_iota(jnp.int32, (rb, rb), 0) >= lax.broadcasted_iota(jnp.int32, (rb, rb), 1))
    sub = lax.broadcasted_iota(jnp.int32, crow_ref.shape, 0)
    crow_all = crow_ref[...]
    crow = [jnp.sum(jnp.where(sub == 2 * hp + e, crow_all, 0.0), axis=0, keepdims=True) for e in range(2)]

    units = []
    for r in range(seq // rb):
        stop = (r + 1) * rb
        units += [(r, k0, min(bk, stop - k0)) for k0 in range(0, stop, bk)]
    units.sort(key=lambda u: (u[0] * rb // bk, u[1], u[0]))

    scores, state = {}, {}

    def issue_scores(u):
        r, k0, width = u
        q2 = q_ref[r * rb:(r + 1) * rb, :]
        zero = jnp.zeros_like(q2)
        kt = k_ref[k0:k0 + width, :]
        scores[u, 0] = _dot_nt(jnp.where(first_q, q2, zero), kt)
        scores[u, 1] = _dot_nt(jnp.where(first_q, zero, q2), kt)

    def finish(u):
        r, k0, width = u
        vt = v_ref[k0:k0 + width, :]
        one = jnp.ones_like(vt)
        first_k = lax.broadcasted_iota(jnp.int32, vt.shape, 1) < HEAD_DIM
        on_diagonal = k0 + width == (r + 1) * rb
        for e in range(2):
            v_aug = jnp.where(first_k, vt, one) if e == 0 else jnp.where(first_k, one, vt)
            s = scores.pop((u, e)) - crow[e][:, k0:k0 + width]
            if on_diagonal:
                tail = jnp.where(lower, s[:, width - rb:], NEG_BIG)
                s = tail if width == rb else jnp.concatenate([s[:, :width - rb], tail], axis=1)
            m_tile = jnp.max(s, axis=-1, keepdims=True)
            if k0 == 0:
                m_new = m_tile
                acc = _dot(jnp.exp2(s - m_new).astype(BF16), v_aug)
            else:
                m_old, acc_old = state[r, e]
                m_new = jnp.maximum(m_old, m_tile)
                acc = jnp.exp2(m_old - m_new) * acc_old + _dot(jnp.exp2(s - m_new).astype(BF16), v_aug)
            state[r, e] = (m_new, acc)
        if on_diagonal:
            a0, a1 = state.pop((r, 0))[1], state.pop((r, 1))[1]
            den = jnp.where(first_q, pltpu.roll(a0, HEAD_DIM, axis=1), pltpu.roll(a1, HEAD_DIM, axis=1))
            o_ref[r * rb:(r + 1) * rb, :] = (jnp.where(first_q, a0, a1) / den).astype(o_ref.dtype)

    for u in units[:ahead]:
        issue_scores(u)
    for n, u in enumerate(units):
        if n + ahead < len(units):
            issue_scores(units[n + ahead])
        finish(u)


def _fox(qkv, crow, *, batch, seq, bk=FOX_KEY_TILE, rb=FOX_ROW_BLOCK, ahead=FOX_SCORES_AHEAD):
    t = qkv.shape[0]
    n_pairs = qkv.shape[1] // (3 * LANES)
    return pl.pallas_call(
        functools.partial(_fox_kernel, bk=bk, rb=rb, ahead=ahead),
        grid=(batch, n_pairs),
        in_specs=[
            pl.BlockSpec((seq, LANES), lambda b, h: (b, h)),
            pl.BlockSpec((seq, LANES), lambda b, h: (b, n_pairs + h)),
            pl.BlockSpec((seq, LANES), lambda b, h: (b, 2 * n_pairs + h)),
            pl.BlockSpec((crow.shape[0], seq), lambda b, h: (0, b)),
        ],
        out_specs=pl.BlockSpec((seq, LANES), lambda b, h: (b, h)),
        out_shape=jax.ShapeDtypeStruct((t, n_pairs * LANES), BF16),
        compiler_params=_params(2),
        name="fox_attention",
    )(qkv, qkv, qkv, crow)


def _memkv_kernel(m_ref, g_ref, w_ref, o_ref):
    o_ref[...] = _dot(_rms(m_ref[...], g_ref[...]).astype(BF16), w_ref[...]).astype(BF16)


def _memkv(mem2, g, w, tm=MEMKV_ROWS):
    t, d = mem2.shape
    return pl.pallas_call(
        _memkv_kernel,
        grid=(t // tm,),
        in_specs=[pl.BlockSpec((tm, d), lambda i: (i, 0)), _const_spec(g.shape), _const_spec(w.shape)],
        out_specs=pl.BlockSpec((tm, w.shape[1]), lambda i: (i, 0)),
        out_shape=jax.ShapeDtypeStruct((t, w.shape[1]), BF16),
        compiler_params=_params(1),
        name="mem_kv",
    )(mem2, g, w)


def _rwkv_scan_kernel(at_ref, rt_ref, bt_ref, kt_ref, bh_ref, kh_ref, v_ref, gam_ref, bonus_ref, g_ref,
                      gng_ref, gnb_ref, bdavg_ref, o_ref, state_ref, y_ref):
    @pl.when(pl.program_id(1) == 0)
    def _():
        state_ref[...] = jnp.zeros_like(state_ref)

    n_t, w = v_ref.shape
    n_pairs = w // LANES
    n_chunks = n_t // CHUNK
    lane = lax.broadcasted_iota(jnp.int32, (CHUNK, LANES), 1)
    first = lane < HEAD_DIM
    ri = lax.broadcasted_iota(jnp.int32, (PAIR, PAIR), 0)
    ci = lax.broadcasted_iota(jnp.int32, (PAIR, PAIR), 1)
    strict = ri > ci
    eye = jnp.where(ri == ci, 1.0, 0.0)
    incl2 = (lax.broadcasted_iota(jnp.int32, (PAIR, 2 * PAIR), 0)
             >= (lax.broadcasted_iota(jnp.int32, (PAIR, 2 * PAIR), 1) & (PAIR - 1)))

    def stacked(ref, c, p):
        x = ref[c * CHUNK:(c + 1) * CHUNK, p * LANES:(p + 1) * LANES]
        z = jnp.zeros_like(x)
        return jnp.concatenate([jnp.where(first, x, z), jnp.where(first, z, x)], axis=0)

    pairs = range(n_pairs)
    states = [state_ref[p] for p in pairs]
    p_mat, d_mat, m_t, c_t = {}, {}, {}, {}

    def chain_step(c):
        h_bf = [states[p].astype(BF16) for p in pairs]
        for p in pairs:
            y_s = _dot_nt(p_mat.pop((c, p)), h_bf[p]) + d_mat.pop((c, p))
            gam = gam_ref[0, c:c + 1, p * LANES:(p + 1) * LANES]
            states[p] = states[p] * gam + _dot(h_bf[p], m_t.pop((c, p))) + c_t.pop((c, p))
            y_ref[c * CHUNK:(c + 1) * CHUNK, p * LANES:(p + 1) * LANES] = y_s[:CHUNK] + y_s[CHUNK:]

    def chunk_stages(cps):
        gram, s_inv, q_pow, a_ak, a_rbk, x_rhs, sx = {}, {}, {}, {}, {}, {}, {}
        for cp in cps:
            lhs = jnp.concatenate([stacked(at_ref, *cp), stacked(rt_ref, *cp)], axis=0)
            rhs = jnp.concatenate([stacked(bt_ref, *cp), stacked(kt_ref, *cp)], axis=0)
            gram[cp] = _dot_nt(lhs, rhs)
        yield
        for cp in cps:
            g_all = gram.pop(cp)
            n_ab = jnp.where(strict, g_all[:PAIR, :PAIR], 0.0)
            a_ak[cp] = jnp.where(strict, g_all[:PAIR, PAIR:], 0.0).astype(BF16)
            a_rbk[cp] = jnp.where(incl2, g_all[PAIR:, :], 0.0).astype(BF16)
            s_inv[cp] = eye + n_ab
            q_pow[cp] = n_ab.astype(BF16)
        for cp in cps:
            q_pow[cp] = _dot(q_pow[cp], q_pow[cp]).astype(BF16)
            x_rhs[cp] = _dot(a_ak.pop(cp), stacked(v_ref, *cp))
        yield
        span = 2
        while 2 * span < CHUNK:
            for cp in cps:
                both = _dot(q_pow[cp], jnp.concatenate([s_inv[cp].astype(BF16), q_pow[cp]], axis=1))
                s_inv[cp] = s_inv[cp] + both[:, :PAIR]
                q_pow[cp] = both[:, PAIR:].astype(BF16)
            span *= 2
            yield
        for cp in cps:
            s_inv[cp] = (s_inv[cp] + _dot(q_pow.pop(cp), s_inv[cp].astype(BF16))).astype(BF16)
        yield
        for cp in cps:
            x = jnp.concatenate([stacked(at_ref, *cp), x_rhs.pop(cp).astype(BF16)], axis=1)
            sx[cp] = _dot(s_inv.pop(cp), x).astype(BF16)
        yield
        for cp in cps:
            zero = jnp.zeros((PAIR, PAIR), BF16)
            lower_rows = jnp.concatenate([zero, stacked(v_ref, *cp)], axis=1)
            full = jnp.concatenate([sx.pop(cp), lower_rows], axis=0)
            pd = _dot(a_rbk.pop(cp), full)
            p_mat[cp] = (pd[:, :PAIR] + stacked(rt_ref, *cp).astype(F32)).astype(BF16)
            d_mat[cp] = pd[:, PAIR:]
            mc = _dot_tn(full, jnp.concatenate([stacked(bh_ref, *cp), stacked(kh_ref, *cp)], axis=0))
            m_t[cp] = mc[:PAIR].astype(BF16)
            c_t[cp] = mc[PAIR:]
        yield

    group = max(1, n_chunks // 2)
    pending = []
    for c0 in range(0, n_chunks, group):
        chunks = range(c0, min(c0 + group, n_chunks))
        for _ in chunk_stages([(c, p) for c in chunks for p in pairs]):
            if pending:
                chain_step(pending.pop(0))
        pending += list(chunks)
    for c in pending:
        chain_step(c)
    for p in pairs:
        state_ref[p] = states[p]

    y = y_ref[...]
    bdavg = bdavg_ref[...]
    dev = y - _dot_terms_rhs(y, bdavg, HEAD_SUM_TERMS + 1)
    var = _dot_terms_rhs(dev * dev, bdavg, HEAD_SUM_TERMS)
    yn = dev * lax.rsqrt(var + GN_EPS) * gng_ref[...] + gnb_ref[...]
    o_ref[...] = ((yn + bonus_ref[...]) * g_ref[...]).astype(o_ref.dtype)


def _rwkv_scan(at, rt, bt, kt, bh, kh, v, gam, bonus, g, gn_g, gn_b, bdavg, *, batch, seq, tl=SCAN_ROWS):
    t, w = v.shape
    n_sel = tl // CHUNK
    gam = gam.reshape(t // tl, n_sel, w)
    nt = seq // tl
    tok = pl.BlockSpec((tl, w), lambda b, i: (b * nt + i, 0))
    return pl.pallas_call(
        _rwkv_scan_kernel,
        grid=(batch, nt),
        in_specs=[tok] * 7 + [pl.BlockSpec((1, n_sel, w), lambda b, i: (b * nt + i, 0, 0)), tok, tok,
                              _const_spec(gn_g.shape), _const_spec(gn_b.shape), _const_spec(bdavg.shape)],
        out_specs=tok,
        out_shape=jax.ShapeDtypeStruct((t, w), BF16),
        scratch_shapes=[pltpu.VMEM((w // LANES, LANES, LANES), F32), pltpu.VMEM((tl, w), F32)],
        compiler_params=_params(2),
        name="rwkv_scan",
    )(at, rt, bt, kt, bh, kh, v, gam, bonus, g, gn_g, gn_b, bdavg)


def _merge_kernel(x_ref, fox_ref, rw_ref, mq_ref, kv_ref, gate_ref, wf_ref, wr_ref, wm_ref, wo_ref, g_ref, o_ref):
    d = x_ref.shape[1]
    w = mq_ref.shape[1]
    hd = w // MEM_HEADS
    heads = range(MEM_HEADS)
    scores = [_dot_nt(mq_ref[:, h * hd:(h + 1) * hd], kv_ref[:, h * hd:(h + 1) * hd]) for h in heads]
    merged = (gate_ref[:, 0:d].astype(F32) * _dot(fox_ref[...], wf_ref[...])
              + gate_ref[:, d:2 * d].astype(F32) * _dot(rw_ref[...], wr_ref[...]))
    probs, norms = [], []
    for h in heads:
        p = jnp.exp(scores[h] - jnp.max(scores[h], axis=-1, keepdims=True))
        norms.append(jnp.sum(p, axis=-1, keepdims=True))
        probs.append(p.astype(BF16))
    mem_out = jnp.concatenate(
        [(_dot(probs[h], kv_ref[:, w + h * hd:w + (h + 1) * hd]) / norms[h]).astype(BF16) for h in heads], axis=1)
    merged = merged + gate_ref[:, 2 * d:3 * d].astype(F32) * _dot(mem_out, wm_ref[...])
    y = _dot(merged.astype(BF16), wo_ref[...])
    o_ref[...] = x_ref[...] + _rms(y, g_ref[...])


def _merge(x2, fox, rw, mq, memkv, gates, wf, wr, wm, wo, g, *, seq, mem_len, tm=MERGE_ROWS):
    t, d = x2.shape
    assert seq % tm == 0
    tiles_per_seq = seq // tm
    row = lambda width: pl.BlockSpec((tm, width), lambda i: (i, 0))
    return pl.pallas_call(
        _merge_kernel,
        grid=(t // tm,),
        in_specs=[row(d), row(fox.shape[1]), row(rw.shape[1]), row(mq.shape[1]),
                  pl.BlockSpec((mem_len, memkv.shape[1]), lambda i: (i // tiles_per_seq, 0)), row(gates.shape[1]),
                  _const_spec(wf.shape), _const_spec(wr.shape), _const_spec(wm.shape), _const_spec(wo.shape),
                  _const_spec(g.shape)],
        out_specs=row(d),
        out_shape=jax.ShapeDtypeStruct((t, d), F32),
        compiler_params=_params(1),
        name="merge_out",
    )(x2, fox, rw, mq, memkv, gates, wf, wr, wm, wo, g)


def _ffn_kernel(h_ref, g1_ref, wg_ref, wu_ref, wd_ref, g2_ref, o_ref, *, ff_chunk):
    h = h_ref[...]
    u = _rms(h, g1_ref[...]).astype(BF16)
    d_ff = wg_ref.shape[1]
    acc = jnp.zeros(h.shape, F32)
    for lo in range(0, d_ff, ff_chunk):
        hi = min(lo + ff_chunk, d_ff)
        gt = _dot(u, wg_ref[:, lo:hi])
        up = _dot(u, wu_ref[:, lo:hi])
        act = (gt * _sigmoid(gt) * up).astype(BF16)
        acc = acc + _dot(act, wd_ref[lo:hi, :])
    o_ref[...] = h + _rms(acc, g2_ref[...])


def _ffn(h2, g1, wg, wu, wd, g2, tm=FFN_ROWS, ff_chunk=FFN_CHUNK):
    t, d = h2.shape
    row = pl.BlockSpec((tm, d), lambda i: (i, 0))
    return pl.pallas_call(
        functools.partial(_ffn_kernel, ff_chunk=ff_chunk),
        grid=(t // tm,),
        in_specs=[row, _const_spec(g1.shape), _const_spec(wg.shape), _const_spec(wu.shape),
                  _const_spec(wd.shape), _const_spec(g2.shape)],
        out_specs=row,
        out_shape=jax.ShapeDtypeStruct((t, d), F32),
        compiler_params=_params(1),
        name="ffn",
    )(h2, g1, wg, wu, wd, g2)


def _block_diag_ones(width, block, value):
    idx = jnp.arange(width) // block
    return jnp.where(idx[:, None] == idx[None, :], value, 0.0).astype(BF16)


def kernel(x, mem, pre1_g, post1_g, pre2_g, post2_g, mem_norm_g, w_in, fox_f_bias, rwkv_mu, rwkv_w0,
           rwkv_w_up, rwkv_a0, rwkv_a_up, rwkv_g_up, rwkv_k_k, rwkv_k_a, rwkv_r_k, rwkv_gn_g, rwkv_gn_b,
           w_mem_kv, w_fox_out, w_rwkv_out, w_mem_out, w_o, w_ffn_gate, w_ffn_up, w_ffn_down):
    batch, seq, d = x.shape
    mem_len = mem.shape[1]
    depth = w_in.shape[0]
    fox_heads = fox_f_bias.shape[1]
    fox_w = fox_heads * HEAD_DIM
    rw_w = rwkv_w0.shape[1]
    rw_cols = rwkv_mu.shape[1]
    mem_w = w_mem_kv.shape[2] // 2
    dec_lora = rwkv_w_up.shape[1]
    aaa_lora = rwkv_a_up.shape[1]
    assert dec_lora + aaa_lora == LANES and dec_lora == aaa_lora
    assert fox_heads <= SUBLANES and fox_w % LANES == 0 and rw_w % LANES == 0
    row2 = lambda v: v.reshape(1, -1)

    h = x.reshape(batch * seq, d)
    mem2 = mem.reshape(batch * mem_len, d)
    bd_ones = _block_diag_ones(rw_w, HEAD_DIM, 1.0)
    bd_avg = _block_diag_ones(rw_w, HEAD_DIM, 1.0 / HEAD_DIM)
    for l in range(depth):
        wi = w_in[l]
        o_ff = 3 * fox_w
        o_rw = o_ff + fox_heads
        o_mq = o_rw + rw_cols
        o_gt = o_mq + mem_w
        scale = HEAD_DIM ** -0.5 * LOG2E
        sections = [("qkv", jnp.concatenate([wi[:, :fox_w] * scale, wi[:, fox_w:o_ff]], axis=1)),
                    ("ff", jnp.pad(wi[:, o_ff:o_rw], ((0, 0), (0, LANES - fox_heads)))),
                    ("rwkv", wi[:, o_rw:o_mq]), ("memq", wi[:, o_mq:o_gt]), ("gate", wi[:, o_gt:])]
        w_all = jnp.concatenate([sec for _, sec in sections], axis=1).astype(BF16)
        cols, start = {}, 0
        for name, sec in sections:
            assert sec.shape[1] % LANES == 0
            cols[name] = (start, start + sec.shape[1])
            start += sec.shape[1]
        fbias = jnp.pad(row2(fox_f_bias[l]), ((0, 0), (0, LANES - fox_heads)))
        wlora = jnp.zeros((LANES, 2 * rw_w), F32)
        wlora = wlora.at[:dec_lora, :rw_w].set(rwkv_w_up[l]).at[dec_lora:, rw_w:].set(rwkv_a_up[l]).astype(BF16)

        qkv, crow, mq, gates, *scan_in = _inproj(
            h, row2(pre1_g[l]), w_all, cols, fbias, row2(rwkv_mu[l]), row2(rwkv_w0[l]), row2(rwkv_a0[l]),
            row2(rwkv_k_k[l]), row2(rwkv_k_a[l]), row2(rwkv_r_k[l]), wlora, rwkv_g_up[l].astype(BF16), bd_ones,
            seq=seq, mem_scale=(mem_w // MEM_HEADS) ** -0.5)

        fox_out = _fox(qkv, crow, batch=batch, seq=seq)

        memkv = _memkv(mem2, row2(mem_norm_g[l]), w_mem_kv[l].astype(BF16))
        rwkv_out = _rwkv_scan(*scan_in, row2(rwkv_gn_g[l]), row2(rwkv_gn_b[l]), bd_avg, batch=batch, seq=seq)

        h = _merge(h, fox_out, rwkv_out, mq, memkv, gates, w_fox_out[l].astype(BF16),
                   w_rwkv_out[l].astype(BF16), w_mem_out[l].astype(BF16), w_o[l].astype(BF16),
                   row2(post1_g[l]), seq=seq, mem_len=mem_len)
        h = _ffn(h, row2(pre2_g[l]), w_ffn_gate[l].astype(BF16), w_ffn_up[l].astype(BF16),
                 w_ffn_down[l].astype(BF16), row2(post2_g[l]))
    return h.reshape(batch, seq, d).astype(x.dtype)
```

```python
import functools

import jax
import jax.numpy as jnp
from jax import lax
from jax.experimental import pallas as pl
from jax.experimental.pallas import tpu as pltpu

F32 = jnp.float32
BF16 = jnp.bfloat16

NORM_EPS = 1e-6
GN_EPS = 64e-5
HEAD_DIM = 64
LANES = 128
SUBLANES = 8
LOG2E = 1.4426950408889634
MEM_HEADS = 4
CHUNK = 64
PAIR = 2 * CHUNK
F32_TERMS = 3
DECAY_SUM_TERMS = 2
HEAD_SUM_TERMS = 1
NEG_BIG = -1e30
VMEM_LIMIT = 56 * 1024 * 1024

INPROJ_ROWS = 256
SCAN_ROWS = 512
FOX_KEY_TILE = 512
FOX_ROW_BLOCK = 128
FOX_SCORES_AHEAD = 2
MEMKV_ROWS = 512
MERGE_ROWS = 1024
FFN_ROWS = 1024
FFN_CHUNK = 1024


def _dot(a, b):
    return jnp.dot(a, b, preferred_element_type=F32)


def _dot_nt(a, b):
    return lax.dot_general(a, b, (((1,), (1,)), ((), ())), preferred_element_type=F32)


def _dot_tn(a, b):
    return lax.dot_general(a, b, (((0,), (0,)), ((), ())), preferred_element_type=F32)


def _bf16_terms(x, n):
    terms = []
    for _ in range(n - 1):
        t = x.astype(BF16)
        terms.append(t)
        x = x - t.astype(F32)
    return terms + [x.astype(BF16)]


def _dot_terms_lhs(m_bf16, x, n):
    return sum(_dot(m_bf16, t) for t in _bf16_terms(x, n))


def _dot_terms_rhs(x, m_bf16, n):
    return sum(_dot(t, m_bf16) for t in _bf16_terms(x, n))


def _rms(xf, g):
    return xf * lax.rsqrt(jnp.mean(xf * xf, axis=-1, keepdims=True) + NORM_EPS) * g


def _sigmoid(x):
    return 1.0 / (1.0 + jnp.exp(-x))


def _log_sigmoid(x):
    return jnp.minimum(x, 0.0) - jnp.log(1.0 + jnp.exp(-jnp.abs(x)))


def _row_halves(n_rows):
    return slice(0, n_rows // 2), slice(n_rows // 2, n_rows)


def _const_spec(shape):
    nd = len(shape)
    return pl.BlockSpec(shape, lambda *_: (0,) * nd, pipeline_mode=pl.Buffered(1))


def _params(n_axes):
    return pltpu.CompilerParams(dimension_semantics=("arbitrary",) * n_axes,
                                vmem_limit_bytes=VMEM_LIMIT)


def _sum_matrices(tm):
    t = jnp.arange(tm)[:, None]
    s = jnp.arange(tm)[None, :]
    same = (t // CHUNK) == (s // CHUNK)
    stacked = jnp.stack([same & (s <= t), same & (s > t), s <= t]).astype(BF16)
    n_sel = max(SUBLANES, tm // CHUNK)
    selector = (jnp.arange(n_sel)[:, None] == s // CHUNK).astype(BF16)
    return stacked, selector


def _inproj_kernel(x_ref, g_ref, w_ref, fb_ref, mu_ref, w0_ref, a0_ref, kk_ref, ka_ref, rk_ref,
                   wlora_ref, gup_ref, bd_ref, sums_ref, sel_ref,
                   qkv_ref, crow_ref, mq_ref, gate_ref,
                   at_ref, rt_ref, bt_ref, kt_ref, bh_ref, kh_ref, v_ref, gam_ref, bonus_ref, g_out_ref,
                   carry_ref, prev_ref, *, tiles_per_seq, mem_scale, cols):
    i = pl.program_id(0)
    tm = x_ref.shape[0]
    w = v_ref.shape[1]

    @pl.when(i % tiles_per_seq == 0)
    def _():
        carry_ref[...] = jnp.zeros_like(carry_ref)
        prev_ref[...] = jnp.zeros_like(prev_ref)

    def proj(name, lo=0, hi=None):
        start, stop = cols[name]
        return _dot(u, w_ref[:, start + lo:(stop if hi is None else start + hi)])

    u = _rms(x_ref[...], g_ref[...]).astype(BF16)

    p = proj("rwkv")
    rowid = lax.broadcasted_iota(jnp.int32, p.shape, 0)
    shifted = jnp.where(rowid == 0, prev_ref[...], pltpu.roll(p, 1, axis=0))
    prev_ref[...] = p[tm - 1:tm, :]
    xs = p + (shifted - p) * mu_ref[...]
    r = xs[:, 0:w]
    k = xs[:, w:2 * w]
    v = xs[:, 2 * w:3 * w]
    lora_in = xs[:, 3 * w:3 * w + LANES]
    gd = xs[:, 3 * w + LANES:]
    lane = lax.broadcasted_iota(jnp.int32, lora_in.shape, 1)
    lora_in = jnp.where(lane < lora_in.shape[1] // 2, jnp.tanh(lora_in), lora_in)
    qkv_ref[...] = proj("qkv").astype(BF16)
    lora = _dot(lora_in.astype(BF16), wlora_ref[...])
    g_out_ref[...] = _dot(_sigmoid(gd).astype(BF16), gup_ref[...])
    w_log = -jnp.exp(_log_sigmoid(w0_ref[...] + lora[:, :w]) - 0.5)
    a = _sigmoid(a0_ref[...] + lora[:, w:])

    d = gate_ref.shape[1] // 3
    gate_ref[:, 0:d] = _sigmoid(proj("gate", 0, d)).astype(BF16)

    bd = bd_ref[...]
    kk = k * kk_ref[...]
    kk = kk * lax.rsqrt(jnp.maximum(_dot_terms_rhs(kk * kk, bd, HEAD_SUM_TERMS), 1e-24))
    k2 = k * (1.0 + (a - 1.0) * ka_ref[...])
    bonus_ref[...] = _dot_terms_rhs(r * k2 * rk_ref[...], bd, HEAD_SUM_TERMS) * v

    lower, upper = sums_ref[0], sums_ref[1]
    w_terms = _bf16_terms(w_log, DECAY_SUM_TERMS)
    cum = sum(_dot(lower, t) for t in w_terms)
    rem = sum(_dot(upper, t) for t in w_terms)
    gam_ref[0] = jnp.exp(sum(_dot(sel_ref[...], t) for t in w_terms))

    gate_ref[:, d:2 * d] = _sigmoid(proj("gate", d, 2 * d)).astype(BF16)

    e_prev = jnp.exp(cum - w_log)
    e_cum = jnp.exp(cum)
    e_neg = jnp.exp(-cum)
    e_rem = jnp.exp(rem)
    b = kk * a
    at_ref[...] = (-kk * e_prev).astype(BF16)
    rt_ref[...] = (r * e_cum).astype(BF16)
    bt_ref[...] = (b * e_neg).astype(BF16)
    kt_ref[...] = (k2 * e_neg).astype(BF16)
    bh_ref[...] = (b * e_rem).astype(BF16)
    kh_ref[...] = (k2 * e_rem).astype(BF16)
    v_ref[...] = v.astype(BF16)

    gate_ref[:, 2 * d:3 * d] = _sigmoid(proj("gate", 2 * d, 3 * d)).astype(BF16)
    mq_ref[...] = (proj("memq") * mem_scale).astype(BF16)

    ls = _log_sigmoid(proj("ff") + fb_ref[...])
    c = _dot_terms_lhs(sums_ref[2], ls, F32_TERMS) + carry_ref[...]
    carry_ref[...] = c[tm - 1:tm, :]
    crow_ref[...] = jnp.transpose(c * LOG2E)[:crow_ref.shape[0], :]


def _inproj(x2, g, w_all, cols, fbias, mu, w0, a0, k_k, k_a, r_k, wlora, gup, bd, *, seq, mem_scale,
            tm=INPROJ_ROWS):
    t, d = x2.shape
    w = w0.shape[1]
    n_tiles = t // tm
    width = lambda name: cols[name][1] - cols[name][0]
    tok = lambda n, dt: jax.ShapeDtypeStruct((t, n), dt)
    sums, selector = _sum_matrices(tm)
    n_sel = selector.shape[0]
    outs = ([tok(width("qkv"), BF16), jax.ShapeDtypeStruct((SUBLANES, t), F32), tok(width("memq"), BF16),
             tok(width("gate"), BF16)] + [tok(w, BF16)] * 7
            + [jax.ShapeDtypeStruct((n_tiles, n_sel, w), F32), tok(w, F32), tok(w, F32)])
    row = lambda n: pl.BlockSpec((tm, n), lambda i: (i, 0))
    consts = (g, w_all, fbias, mu, w0, a0, k_k, k_a, r_k, wlora, gup, bd, sums, selector)
    gam_index = 11
    results = list(pl.pallas_call(
        functools.partial(_inproj_kernel, tiles_per_seq=seq // tm, mem_scale=mem_scale, cols=cols),
        grid=(n_tiles,),
        in_specs=[row(d)] + [_const_spec(c.shape) for c in consts],
        out_specs=([row(width("qkv")), pl.BlockSpec((SUBLANES, tm), lambda i: (0, i)), row(width("memq")),
                    row(width("gate"))] + [row(w)] * 7
                   + [pl.BlockSpec((1, n_sel, w), lambda i: (i, 0, 0)), row(w), row(w)]),
        out_shape=outs,
        scratch_shapes=[pltpu.VMEM((1, LANES), F32), pltpu.VMEM((1, cols["rwkv"][1] - cols["rwkv"][0]), F32)],
        compiler_params=_params(1),
        name="inproj",
    )(x2, *consts))
    results[gam_index] = results[gam_index][:, :tm // CHUNK]
    return results


def _fox_kernel(q_ref, k_ref, v_ref, crow_ref, o_ref, *, bk, rb, ahead):
    hp = pl.program_id(1)
    seq = q_ref.shape[0]
    first_q = lax.broadcasted_iota(jnp.int32, (rb, LANES), 1) < HEAD_DIM
    lower = (lax.broadcasted_iota(jnp.int32, (rb, rb), 0) >= lax.broadcasted_iota(jnp.int32, (rb, rb), 1))
    sub = lax.broadcasted_iota(jnp.int32, crow_ref.shape, 0)
    crow_all = crow_ref[...]
    crow = [jnp.sum(jnp.where(sub == 2 * hp + e, crow_all, 0.0), axis=0, keepdims=True) for e in range(2)]

    units = []
    for r in range(seq // rb):
        stop = (r + 1) * rb
        units += [(r, k0, min(bk, stop - k0)) for k0 in range(0, stop, bk)]
    units.sort(key=lambda u: (u[0] * rb // bk, u[1], u[0]))

    scores, state = {}, {}

    def issue_scores(u):
        r, k0, width = u
        q2 = q_ref[r * rb:(r + 1) * rb, :]
        zero = jnp.zeros_like(q2)
        kt = k_ref[k0:k0 + width, :]
        scores[u, 0] = _dot_nt(jnp.where(first_q, q2, zero), kt)
        scores[u, 1] = _dot_nt(jnp.where(first_q, zero, q2), kt)

    def finish(u):
        r, k0, width = u
        vt = v_ref[k0:k0 + width, :]
        one = jnp.ones_like(vt)
        first_k = lax.broadcasted_iota(jnp.int32, vt.shape, 1) < HEAD_DIM
        on_diagonal = k0 + width == (r + 1) * rb
        for e in range(2):
            v_aug = jnp.where(first_k, vt, one) if e == 0 else jnp.where(first_k, one, vt)
            s = scores.pop((u, e)) - crow[e][:, k0:k0 + width]
            if on_diagonal:
                tail = jnp.where(lower, s[:, width - rb:], NEG_BIG)
                s = tail if width == rb else jnp.concatenate([s[:, :width - rb], tail], axis=1)
            m_tile = jnp.max(s, axis=-1, keepdims=True)
            if k0 == 0:
                m_new = m_tile
                acc = _dot(jnp.exp2(s - m_new).astype(BF16), v_aug)
            else:
                m_old, acc_old = state[r, e]
                m_new = jnp.maximum(m_old, m_tile)
                acc = jnp.exp2(m_old - m_new) * acc_old + _dot(jnp.exp2(s - m_new).astype(BF16), v_aug)
            state[r, e] = (m_new, acc)
        if on_diagonal:
            a0, a1 = state.pop((r, 0))[1], state.pop((r, 1))[1]
            den = jnp.where(first_q, pltpu.roll(a0, HEAD_DIM, axis=1), pltpu.roll(a1, HEAD_DIM, axis=1))
            o_ref[r * rb:(r + 1) * rb, :] = (jnp.where(first_q, a0, a1) / den).astype(o_ref.dtype)

    for u in units[:ahead]:
        issue_scores(u)
    for n, u in enumerate(units):
        if n + ahead < len(units):
            issue_scores(units[n + ahead])
        finish(u)


def _fox(qkv, crow, *, batch, seq, bk=FOX_KEY_TILE, rb=FOX_ROW_BLOCK, ahead=FOX_SCORES_AHEAD):
    t = qkv.shape[0]
    n_pairs = qkv.shape[1] // (3 * LANES)
    return pl.pallas_call(
        functools.partial(_fox_kernel, bk=bk, rb=rb, ahead=ahead),
        grid=(batch, n_pairs),
        in_specs=[
            pl.BlockSpec((seq, LANES), lambda b, h: (b, h)),
            pl.BlockSpec((seq, LANES), lambda b, h: (b, n_pairs + h)),
            pl.BlockSpec((seq, LANES), lambda b, h: (b, 2 * n_pairs + h)),
            pl.BlockSpec((crow.shape[0], seq), lambda b, h: (0, b)),
        ],
        out_specs=pl.BlockSpec((seq, LANES), lambda b, h: (b, h)),
        out_shape=jax.ShapeDtypeStruct((t, n_pairs * LANES), BF16),
        compiler_params=_params(2),
        name="fox_attention",
    )(qkv, qkv, qkv, crow)


def _memkv_kernel(m_ref, g_ref, w_ref, o_ref):
    o_ref[...] = _dot(_rms(m_ref[...], g_ref[...]).astype(BF16), w_ref[...]).astype(BF16)


def _memkv(mem2, g, w, tm=MEMKV_ROWS):
    t, d = mem2.shape
    return pl.pallas_call(
        _memkv_kernel,
        grid=(t // tm,),
        in_specs=[pl.BlockSpec((tm, d), lambda i: (i, 0)), _const_spec(g.shape), _const_spec(w.shape)],
        out_specs=pl.BlockSpec((tm, w.shape[1]), lambda i: (i, 0)),
        out_shape=jax.ShapeDtypeStruct((t, w.shape[1]), BF16),
        compiler_params=_params(1),
        name="mem_kv",
    )(mem2, g, w)


def _rwkv_scan_kernel(at_ref, rt_ref, bt_ref, kt_ref, bh_ref, kh_ref, v_ref, gam_ref, bonus_ref, g_ref,
                      gng_ref, gnb_ref, bdavg_ref, o_ref, state_ref, y_ref):
    @pl.when(pl.program_id(1) == 0)
    def _():
        state_ref[...] = jnp.zeros_like(state_ref)

    n_t, w = v_ref.shape
    n_pairs = w // LANES
    n_chunks = n_t // CHUNK
    lane = lax.broadcasted_iota(jnp.int32, (CHUNK, LANES), 1)
    first = lane < HEAD_DIM
    ri = lax.broadcasted_iota(jnp.int32, (PAIR, PAIR), 0)
    ci = lax.broadcasted_iota(jnp.int32, (PAIR, PAIR), 1)
    strict = ri > ci
    eye = jnp.where(ri == ci, 1.0, 0.0)
    incl2 = (lax.broadcasted_iota(jnp.int32, (PAIR, 2 * PAIR), 0)
             >= (lax.broadcasted_iota(jnp.int32, (PAIR, 2 * PAIR), 1) & (PAIR - 1)))

    def stacked(ref, c, p):
        x = ref[c * CHUNK:(c + 1) * CHUNK, p * LANES:(p + 1) * LANES]
        z = jnp.zeros_like(x)
        return jnp.concatenate([jnp.where(first, x, z), jnp.where(first, z, x)], axis=0)

    pairs = range(n_pairs)
    states = [state_ref[p] for p in pairs]
    p_mat, d_mat, m_t, c_t = {}, {}, {}, {}

    def chain_step(c):
        h_bf = [states[p].astype(BF16) for p in pairs]
        for p in pairs:
            y_s = _dot_nt(p_mat.pop((c, p)), h_bf[p]) + d_mat.pop((c, p))
            gam = gam_ref[0, c:c + 1, p * LANES:(p + 1) * LANES]
            states[p] = states[p] * gam + _dot(h_bf[p], m_t.pop((c, p))) + c_t.pop((c, p))
            y_ref[c * CHUNK:(c + 1) * CHUNK, p * LANES:(p + 1) * LANES] = y_s[:CHUNK] + y_s[CHUNK:]

    def chunk_stages(cps):
        gram, s_inv, q_pow, a_ak, a_rbk, x_rhs, sx = {}, {}, {}, {}, {}, {}, {}
        for cp in cps:
            lhs = jnp.concatenate([stacked(at_ref, *cp), stacked(rt_ref, *cp)], axis=0)
            rhs = jnp.concatenate([stacked(bt_ref, *cp), stacked(kt_ref, *cp)], axis=0)
            gram[cp] = _dot_nt(lhs, rhs)
        yield
        for cp in cps:
            g_all = gram.pop(cp)
            n_ab = jnp.where(strict, g_all[:PAIR, :PAIR], 0.0)
            a_ak[cp] = jnp.where(strict, g_all[:PAIR, PAIR:], 0.0).astype(BF16)
            a_rbk[cp] = jnp.where(incl2, g_all[PAIR:, :], 0.0).astype(BF16)
            s_inv[cp] = eye + n_ab
            q_pow[cp] = n_ab.astype(BF16)
        for cp in cps:
            q_pow[cp] = _dot(q_pow[cp], q_pow[cp]).astype(BF16)
            x_rhs[cp] = _dot(a_ak.pop(cp), stacked(v_ref, *cp))
        yield
        span = 2
        while 2 * span < CHUNK:
            for cp in cps:
                both = _dot(q_pow[cp], jnp.concatenate([s_inv[cp].astype(BF16), q_pow[cp]], axis=1))
                s_inv[cp] = s_inv[cp] + both[:, :PAIR]
                q_pow[cp] = both[:, PAIR:].astype(BF16)
            span *= 2
            yield
        for cp in cps:
            s_inv[cp] = (s_inv[cp] + _dot(q_pow.pop(cp), s_inv[cp].astype(BF16))).astype(BF16)
        yield
        for cp in cps:
            x = jnp.concatenate([stacked(at_ref, *cp), x_rhs.pop(cp).astype(BF16)], axis=1)
            sx[cp] = _dot(s_inv.pop(cp), x).astype(BF16)
        yield
        for cp in cps:
            zero = jnp.zeros((PAIR, PAIR), BF16)
            lower_rows = jnp.concatenate([zero, stacked(v_ref, *cp)], axis=1)
            full = jnp.concatenate([sx.pop(cp), lower_rows], axis=0)
            pd = _dot(a_rbk.pop(cp), full)
            p_mat[cp] = (pd[:, :PAIR] + stacked(rt_ref, *cp).astype(F32)).astype(BF16)
            d_mat[cp] = pd[:, PAIR:]
            mc = _dot_tn(full, jnp.concatenate([stacked(bh_ref, *cp), stacked(kh_ref, *cp)], axis=0))
            m_t[cp] = mc[:PAIR].astype(BF16)
            c_t[cp] = mc[PAIR:]
        yield

    group = max(1, n_chunks // 2)
    pending = []
    for c0 in range(0, n_chunks, group):
        chunks = range(c0, min(c0 + group, n_chunks))
        for _ in chunk_stages([(c, p) for c in chunks for p in pairs]):
            if pending:
                chain_step(pending.pop(0))
        pending += list(chunks)
    for c in pending:
        chain_step(c)
    for p in pairs:
        state_ref[p] = states[p]

    y = y_ref[...]
    bdavg = bdavg_ref[...]
    dev = y - _dot_terms_rhs(y, bdavg, HEAD_SUM_TERMS + 1)
    var = _dot_terms_rhs(dev * dev, bdavg, HEAD_SUM_TERMS)
    yn = dev * lax.rsqrt(var + GN_EPS) * gng_ref[...] + gnb_ref[...]
    o_ref[...] = ((yn + bonus_ref[...]) * g_ref[...]).astype(o_ref.dtype)


def _rwkv_scan(at, rt, bt, kt, bh, kh, v, gam, bonus, g, gn_g, gn_b, bdavg, *, batch, seq, tl=SCAN_ROWS):
    t, w = v.shape
    n_sel = tl // CHUNK
    gam = gam.reshape(t // tl, n_sel, w)
    nt = seq // tl
    tok = pl.BlockSpec((tl, w), lambda b, i: (b * nt + i, 0))
    return pl.pallas_call(
        _rwkv_scan_kernel,
        grid=(batch, nt),
        in_specs=[tok] * 7 + [pl.BlockSpec((1, n_sel, w), lambda b, i: (b * nt + i, 0, 0)), tok, tok,
                              _const_spec(gn_g.shape), _const_spec(gn_b.shape), _const_spec(bdavg.shape)],
        out_specs=tok,
        out_shape=jax.ShapeDtypeStruct((t, w), BF16),
        scratch_shapes=[pltpu.VMEM((w // LANES, LANES, LANES), F32), pltpu.VMEM((tl, w), F32)],
        compiler_params=_params(2),
        name="rwkv_scan",
    )(at, rt, bt, kt, bh, kh, v, gam, bonus, g, gn_g, gn_b, bdavg)


def _merge_kernel(x_ref, fox_ref, rw_ref, mq_ref, kv_ref, gate_ref, wf_ref, wr_ref, wm_ref, wo_ref, g_ref, o_ref):
    d = x_ref.shape[1]
    w = mq_ref.shape[1]
    hd = w // MEM_HEADS
    heads = range(MEM_HEADS)
    scores = [_dot_nt(mq_ref[:, h * hd:(h + 1) * hd], kv_ref[:, h * hd:(h + 1) * hd]) for h in heads]
    merged = (gate_ref[:, 0:d].astype(F32) * _dot(fox_ref[...], wf_ref[...])
              + gate_ref[:, d:2 * d].astype(F32) * _dot(rw_ref[...], wr_ref[...]))
    probs, norms = [], []
    for h in heads:
        p = jnp.exp(scores[h] - jnp.max(scores[h], axis=-1, keepdims=True))
        norms.append(jnp.sum(p, axis=-1, keepdims=True))
        probs.append(p.astype(BF16))
    mem_out = jnp.concatenate(
        [(_dot(probs[h], kv_ref[:, w + h * hd:w + (h + 1) * hd]) / norms[h]).astype(BF16) for h in heads], axis=1)
    merged = (merged + gate_ref[:, 2 * d:3 * d].astype(F32) * _dot(mem_out, wm_ref[...])).astype(BF16)
    for rows in _row_halves(merged.shape[0]):
        y = _dot(merged[rows], wo_ref[...])
        o_ref[rows, :] = x_ref[rows, :] + _rms(y, g_ref[...])


def _merge(x2, fox, rw, mq, memkv, gates, wf, wr, wm, wo, g, *, seq, mem_len, tm=MERGE_ROWS):
    t, d = x2.shape
    assert seq % tm == 0
    tiles_per_seq = seq // tm
    row = lambda width: pl.BlockSpec((tm, width), lambda i: (i, 0))
    return pl.pallas_call(
        _merge_kernel,
        grid=(t // tm,),
        in_specs=[row(d), row(fox.shape[1]), row(rw.shape[1]), row(mq.shape[1]),
                  pl.BlockSpec((mem_len, memkv.shape[1]), lambda i: (i // tiles_per_seq, 0)), row(gates.shape[1]),
                  _const_spec(wf.shape), _const_spec(wr.shape), _const_spec(wm.shape), _const_spec(wo.shape),
                  _const_spec(g.shape)],
        out_specs=row(d),
        out_shape=jax.ShapeDtypeStruct((t, d), F32),
        compiler_params=_params(1),
        name="merge_out",
    )(x2, fox, rw, mq, memkv, gates, wf, wr, wm, wo, g)


def _ffn_kernel(h_ref, g1_ref, wg_ref, wu_ref, wd_ref, g2_ref, o_ref, *, ff_chunk):
    h = h_ref[...]
    u = _rms(h, g1_ref[...]).astype(BF16)
    d_ff = wg_ref.shape[1]
    acc = jnp.zeros(h.shape, F32)
    for lo in range(0, d_ff, ff_chunk):
        hi = min(lo + ff_chunk, d_ff)
        gt = _dot(u, wg_ref[:, lo:hi])
        up = _dot(u, wu_ref[:, lo:hi])
        act = (gt * _sigmoid(gt) * up).astype(BF16)
        if hi < d_ff:
            acc = acc + _dot(act, wd_ref[lo:hi, :])
        else:
            for rows in _row_halves(h.shape[0]):
                out = acc[rows] + _dot(act[rows], wd_ref[lo:hi, :])
                o_ref[rows, :] = h[rows] + _rms(out, g2_ref[...])


def _ffn(h2, g1, wg, wu, wd, g2, tm=FFN_ROWS, ff_chunk=FFN_CHUNK):
    t, d = h2.shape
    row = pl.BlockSpec((tm, d), lambda i: (i, 0))
    return pl.pallas_call(
        functools.partial(_ffn_kernel, ff_chunk=ff_chunk),
        grid=(t // tm,),
        in_specs=[row, _const_spec(g1.shape), _const_spec(wg.shape), _const_spec(wu.shape),
                  _const_spec(wd.shape), _const_spec(g2.shape)],
        out_specs=row,
        out_shape=jax.ShapeDtypeStruct((t, d), F32),
        compiler_params=_params(1),
        name="ffn",
    )(h2, g1, wg, wu, wd, g2)


def _block_diag_ones(width, block, value):
    idx = jnp.arange(width) // block
    return jnp.where(idx[:, None] == idx[None, :], value, 0.0).astype(BF16)


def kernel(x, mem, pre1_g, post1_g, pre2_g, post2_g, mem_norm_g, w_in, fox_f_bias, rwkv_mu, rwkv_w0,
           rwkv_w_up, rwkv_a0, rwkv_a_up, rwkv_g_up, rwkv_k_k, rwkv_k_a, rwkv_r_k, rwkv_gn_g, rwkv_gn_b,
           w_mem_kv, w_fox_out, w_rwkv_out, w_mem_out, w_o, w_ffn_gate, w_ffn_up, w_ffn_down):
    batch, seq, d = x.shape
    mem_len = mem.shape[1]
    depth = w_in.shape[0]
    fox_heads = fox_f_bias.shape[1]
    fox_w = fox_heads * HEAD_DIM
    rw_w = rwkv_w0.shape[1]
    rw_cols = rwkv_mu.shape[1]
    mem_w = w_mem_kv.shape[2] // 2
    dec_lora = rwkv_w_up.shape[1]
    aaa_lora = rwkv_a_up.shape[1]
    assert dec_lora + aaa_lora == LANES and dec_lora == aaa_lora
    assert fox_heads <= SUBLANES and fox_w % LANES == 0 and rw_w % LANES == 0
    row2 = lambda v: v.reshape(1, -1)

    h = x.reshape(batch * seq, d)
    mem2 = mem.reshape(batch * mem_len, d)
    bd_ones = _block_diag_ones(rw_w, HEAD_DIM, 1.0)
    bd_avg = _block_diag_ones(rw_w, HEAD_DIM, 1.0 / HEAD_DIM)
    for l in range(depth):
        wi = w_in[l]
        o_ff = 3 * fox_w
        o_rw = o_ff + fox_heads
        o_mq = o_rw + rw_cols
        o_gt = o_mq + mem_w
        scale = HEAD_DIM ** -0.5 * LOG2E
        sections = [("qkv", jnp.concatenate([wi[:, :fox_w] * scale, wi[:, fox_w:o_ff]], axis=1)),
                    ("ff", jnp.pad(wi[:, o_ff:o_rw], ((0, 0), (0, LANES - fox_heads)))),
                    ("rwkv", wi[:, o_rw:o_mq]), ("memq", wi[:, o_mq:o_gt]), ("gate", wi[:, o_gt:])]
        w_all = jnp.concatenate([sec for _, sec in sections], axis=1).astype(BF16)
        cols, start = {}, 0
        for name, sec in sections:
            assert sec.shape[1] % LANES == 0
            cols[name] = (start, start + sec.shape[1])
            start += sec.shape[1]
        fbias = jnp.pad(row2(fox_f_bias[l]), ((0, 0), (0, LANES - fox_heads)))
        wlora = jnp.zeros((LANES, 2 * rw_w), F32)
        wlora = wlora.at[:dec_lora, :rw_w].set(rwkv_w_up[l]).at[dec_lora:, rw_w:].set(rwkv_a_up[l]).astype(BF16)

        qkv, crow, mq, gates, *scan_in = _inproj(
            h, row2(pre1_g[l]), w_all, cols, fbias, row2(rwkv_mu[l]), row2(rwkv_w0[l]), row2(rwkv_a0[l]),
            row2(rwkv_k_k[l]), row2(rwkv_k_a[l]), row2(rwkv_r_k[l]), wlora, rwkv_g_up[l].astype(BF16), bd_ones,
            seq=seq, mem_scale=(mem_w // MEM_HEADS) ** -0.5)

        fox_out = _fox(qkv, crow, batch=batch, seq=seq)

        memkv = _memkv(mem2, row2(mem_norm_g[l]), w_mem_kv[l].astype(BF16))
        rwkv_out = _rwkv_scan(*scan_in, row2(rwkv_gn_g[l]), row2(rwkv_gn_b[l]), bd_avg, batch=batch, seq=seq)

        h = _merge(h, fox_out, rwkv_out, mq, memkv, gates, w_fox_out[l].astype(BF16),
                   w_rwkv_out[l].astype(BF16), w_mem_out[l].astype(BF16), w_o[l].astype(BF16),
                   row2(post1_g[l]), seq=seq, mem_len=mem_len)
        h = _ffn(h, row2(pre2_g[l]), w_ffn_gate[l].astype(BF16), w_ffn_up[l].astype(BF16),
                 w_ffn_down[l].astype(BF16), row2(post2_g[l]))
    return h.reshape(batch, seq, d).astype(x.dtype)
```

```python
import functools

import jax
import jax.numpy as jnp
from jax import lax
from jax.experimental import pallas as pl
from jax.experimental.pallas import tpu as pltpu

F32 = jnp.float32
BF16 = jnp.bfloat16

NORM_EPS = 1e-6
GN_EPS = 64e-5
HEAD_DIM = 64
LANES = 128
SUBLANES = 8
LOG2E = 1.4426950408889634
MEM_HEADS = 4
CHUNK = 64
PAIR = 2 * CHUNK
F32_TERMS = 3
DECAY_SUM_TERMS = 2
HEAD_SUM_TERMS = 1
SUM_BLOCK = 256
INPROJ_ROWS = 512
SCAN_ROWS = 512
NEG_BIG = -1e30
VMEM_LIMIT = 56 * 1024 * 1024


def _dot(a, b):
    return jnp.dot(a, b, preferred_element_type=F32)


def _dot_nt(a, b):
    return lax.dot_general(a, b, (((1,), (1,)), ((), ())), preferred_element_type=F32)


def _dot_tn(a, b):
    return lax.dot_general(a, b, (((0,), (0,)), ((), ())), preferred_element_type=F32)


def _bf16_terms(x, n):
    terms = []
    for _ in range(n - 1):
        t = x.astype(BF16)
        terms.append(t)
        x = x - t.astype(F32)
    return terms + [x.astype(BF16)]


def _dot_terms_lhs(m_bf16, x, n):
    return sum(_dot(m_bf16, t) for t in _bf16_terms(x, n))


def _dot_terms_rhs(x, m_bf16, n):
    return sum(_dot(t, m_bf16) for t in _bf16_terms(x, n))


def _rms(xf, g):
    return xf * lax.rsqrt(jnp.mean(xf * xf, axis=-1, keepdims=True) + NORM_EPS) * g


def _sigmoid(x):
    return 1.0 / (1.0 + jnp.exp(-x))


def _log_sigmoid(x):
    return jnp.minimum(x, 0.0) - jnp.log(1.0 + jnp.exp(-jnp.abs(x)))


def _const_spec(shape):
    nd = len(shape)
    return pl.BlockSpec(shape, lambda *_: (0,) * nd, pipeline_mode=pl.Buffered(1))


def _params(n_axes):
    return pltpu.CompilerParams(dimension_semantics=("arbitrary",) * n_axes,
                                vmem_limit_bytes=VMEM_LIMIT)


def _inproj_kernel(x_ref, g_ref, w_ref, fb_ref, mu_ref, w0_ref, a0_ref, kk_ref, ka_ref, rk_ref,
                   wlora_ref, gup_ref, bd_ref,
                   qkv_ref, crow_ref, mq_ref, gate_ref,
                   at_ref, rt_ref, bt_ref, kt_ref, bh_ref, kh_ref, v_ref, gam_ref, bonus_ref, g_out_ref,
                   carry_ref, prev_ref, *, tiles_per_seq, mem_scale, cols):
    i = pl.program_id(0)
    tm = x_ref.shape[0]
    w = v_ref.shape[1]

    @pl.when(i % tiles_per_seq == 0)
    def _():
        carry_ref[...] = jnp.zeros_like(carry_ref)
        prev_ref[...] = jnp.zeros_like(prev_ref)

    def proj(name, lo=0, hi=None):
        start, stop = cols[name]
        return _dot(u, w_ref[:, start + lo:(stop if hi is None else start + hi)])

    u = _rms(x_ref[...], g_ref[...]).astype(BF16)

    p = proj("rwkv")
    rowid = lax.broadcasted_iota(jnp.int32, p.shape, 0)
    shifted = jnp.where(rowid == 0, prev_ref[...], pltpu.roll(p, 1, axis=0))
    prev_ref[...] = p[tm - 1:tm, :]
    xs = p + (shifted - p) * mu_ref[...]
    r = xs[:, 0:w]
    k = xs[:, w:2 * w]
    v = xs[:, 2 * w:3 * w]
    lora_in = xs[:, 3 * w:3 * w + LANES]
    gd = xs[:, 3 * w + LANES:]
    lane = lax.broadcasted_iota(jnp.int32, lora_in.shape, 1)
    lora_in = jnp.where(lane < lora_in.shape[1] // 2, jnp.tanh(lora_in), lora_in)
    qkv_ref[...] = proj("qkv").astype(BF16)
    lora = _dot(lora_in.astype(BF16), wlora_ref[...])
    g_out_ref[...] = _dot(_sigmoid(gd).astype(BF16), gup_ref[...])
    w_log = -jnp.exp(_log_sigmoid(w0_ref[...] + lora[:, :w]) - 0.5)
    a = _sigmoid(a0_ref[...] + lora[:, w:])

    d = gate_ref.shape[1] // 3
    gate_ref[:, 0:d] = _sigmoid(proj("gate", 0, d)).astype(BF16)

    bd = bd_ref[...]
    kk = k * kk_ref[...]
    kk = kk * lax.rsqrt(jnp.maximum(_dot_terms_rhs(kk * kk, bd, HEAD_SUM_TERMS), 1e-24))
    k2 = k * (1.0 + (a - 1.0) * ka_ref[...])
    bonus_ref[...] = _dot_terms_rhs(r * k2 * rk_ref[...], bd, HEAD_SUM_TERMS) * v

    sb = min(tm, SUM_BLOCK)
    ti = lax.broadcasted_iota(jnp.int32, (sb, sb), 0)
    si = lax.broadcasted_iota(jnp.int32, (sb, sb), 1)
    same = (ti // CHUNK) == (si // CHUNK)
    lower = jnp.where(same & (si <= ti), 1.0, 0.0).astype(BF16)
    upper = jnp.where(same & (si > ti), 1.0, 0.0).astype(BF16)
    w_terms = _bf16_terms(w_log, DECAY_SUM_TERMS)
    blocks = [slice(s0, s0 + sb) for s0 in range(0, tm, sb)]
    cum = jnp.concatenate([sum(_dot(lower, t[blk]) for t in w_terms) for blk in blocks], axis=0)
    rem = jnp.concatenate([sum(_dot(upper, t[blk]) for t in w_terms) for blk in blocks], axis=0)
    n_sel = gam_ref.shape[1]
    ci = lax.broadcasted_iota(jnp.int32, (n_sel, tm), 0)
    sj = lax.broadcasted_iota(jnp.int32, (n_sel, tm), 1)
    sel = jnp.where(ci == sj // CHUNK, 1.0, 0.0).astype(BF16)
    gam_ref[0] = jnp.exp(sum(_dot(sel, t) for t in w_terms))

    gate_ref[:, d:2 * d] = _sigmoid(proj("gate", d, 2 * d)).astype(BF16)

    e_prev = jnp.exp(cum - w_log)
    e_cum = jnp.exp(cum)
    e_neg = jnp.exp(-cum)
    e_rem = jnp.exp(rem)
    b = kk * a
    at_ref[...] = (-kk * e_prev).astype(BF16)
    rt_ref[...] = (r * e_cum).astype(BF16)
    bt_ref[...] = (b * e_neg).astype(BF16)
    kt_ref[...] = (k2 * e_neg).astype(BF16)
    bh_ref[...] = (b * e_rem).astype(BF16)
    kh_ref[...] = (k2 * e_rem).astype(BF16)
    v_ref[...] = v.astype(BF16)

    gate_ref[:, 2 * d:3 * d] = _sigmoid(proj("gate", 2 * d, 3 * d)).astype(BF16)
    mq_ref[...] = (proj("memq") * mem_scale).astype(BF16)

    ls = _log_sigmoid(proj("ff") + fb_ref[...])
    tri = jnp.where(lax.broadcasted_iota(jnp.int32, (tm, tm), 0) >= lax.broadcasted_iota(jnp.int32, (tm, tm), 1),
                    1.0, 0.0).astype(BF16)
    c = _dot_terms_lhs(tri, ls, F32_TERMS) + carry_ref[...]
    carry_ref[...] = c[tm - 1:tm, :]
    crow_ref[...] = jnp.transpose(c * LOG2E)[:crow_ref.shape[0], :]


def _inproj(x2, g, w_all, cols, fbias, mu, w0, a0, k_k, k_a, r_k, wlora, gup, bd, *, seq, mem_scale,
            tm=INPROJ_ROWS):
    t, d = x2.shape
    w = w0.shape[1]
    n_tiles = t // tm
    width = lambda name: cols[name][1] - cols[name][0]
    tok = lambda n, dt: jax.ShapeDtypeStruct((t, n), dt)
    n_sel = max(SUBLANES, tm // CHUNK)
    outs = ([tok(width("qkv"), BF16), jax.ShapeDtypeStruct((SUBLANES, t), F32), tok(width("memq"), BF16),
             tok(width("gate"), BF16)] + [tok(w, BF16)] * 7
            + [jax.ShapeDtypeStruct((n_tiles, n_sel, w), F32), tok(w, F32), tok(w, F32)])
    row = lambda n: pl.BlockSpec((tm, n), lambda i: (i, 0))
    consts = (g, w_all, fbias, mu, w0, a0, k_k, k_a, r_k, wlora, gup, bd)
    gam_index = 11
    results = list(pl.pallas_call(
        functools.partial(_inproj_kernel, tiles_per_seq=seq // tm, mem_scale=mem_scale, cols=cols),
        grid=(n_tiles,),
        in_specs=[row(d)] + [_const_spec(c.shape) for c in consts],
        out_specs=([row(width("qkv")), pl.BlockSpec((SUBLANES, tm), lambda i: (0, i)), row(width("memq")),
                    row(width("gate"))] + [row(w)] * 7
                   + [pl.BlockSpec((1, n_sel, w), lambda i: (i, 0, 0)), row(w), row(w)]),
        out_shape=outs,
        scratch_shapes=[pltpu.VMEM((1, LANES), F32), pltpu.VMEM((1, cols["rwkv"][1] - cols["rwkv"][0]), F32)],
        compiler_params=_params(1),
        name="inproj",
    )(x2, *consts))
    results[gam_index] = results[gam_index][:, :tm // CHUNK]
    return results


def _fox_kernel(q_ref, k_ref, v_ref, crow_ref, o_ref, *, bk, rb, ahead):
    hp = pl.program_id(1)
    seq = q_ref.shape[0]
    first_q = lax.broadcasted_iota(jnp.int32, (rb, LANES), 1) < HEAD_DIM
    lower = (lax.broadcasted_iota(jnp.int32, (rb, rb), 0) >= lax.broadcasted_iota(jnp.int32, (rb, rb), 1))
    sub = lax.broadcasted_iota(jnp.int32, crow_ref.shape, 0)
    crow_all = crow_ref[...]
    crow = [jnp.sum(jnp.where(sub == 2 * hp + e, crow_all, 0.0), axis=0, keepdims=True) for e in range(2)]

    units = []
    for r in range(seq // rb):
        stop = (r + 1) * rb
        units += [(r, k0, min(bk, stop - k0)) for k0 in range(0, stop, bk)]
    units.sort(key=lambda u: (u[0] * rb // bk, u[1], u[0]))

    scores, state = {}, {}

    def issue_scores(u):
        r, k0, width = u
        q2 = q_ref[r * rb:(r + 1) * rb, :]
        zero = jnp.zeros_like(q2)
        kt = k_ref[k0:k0 + width, :]
        scores[u, 0] = _dot_nt(jnp.where(first_q, q2, zero), kt)
        scores[u, 1] = _dot_nt(jnp.where(first_q, zero, q2), kt)

    def finish(u):
        r, k0, width = u
        vt = v_ref[k0:k0 + width, :]
        one = jnp.ones_like(vt)
        first_k = lax.broadcasted_iota(jnp.int32, vt.shape, 1) < HEAD_DIM
        on_diagonal = k0 + width == (r + 1) * rb
        for e in range(2):
            v_aug = jnp.where(first_k, vt, one) if e == 0 else jnp.where(first_k, one, vt)
            s = scores.pop((u, e)) - crow[e][:, k0:k0 + width]
            if on_diagonal:
                tail = jnp.where(lower, s[:, width - rb:], NEG_BIG)
                s = tail if width == rb else jnp.concatenate([s[:, :width - rb], tail], axis=1)
            m_tile = jnp.max(s, axis=-1, keepdims=True)
            if k0 == 0:
                m_new = m_tile
                acc = _dot(jnp.exp2(s - m_new).astype(BF16), v_aug)
            else:
                m_old, acc_old = state[r, e]
                m_new = jnp.maximum(m_old, m_tile)
                acc = jnp.exp2(m_old - m_new) * acc_old + _dot(jnp.exp2(s - m_new).astype(BF16), v_aug)
            state[r, e] = (m_new, acc)
        if on_diagonal:
            a0, a1 = state.pop((r, 0))[1], state.pop((r, 1))[1]
            den = jnp.where(first_q, pltpu.roll(a0, HEAD_DIM, axis=1), pltpu.roll(a1, HEAD_DIM, axis=1))
            o_ref[r * rb:(r + 1) * rb, :] = (jnp.where(first_q, a0, a1) / den).astype(o_ref.dtype)

    for u in units[:ahead]:
        issue_scores(u)
    for n, u in enumerate(units):
        if n + ahead < len(units):
            issue_scores(units[n + ahead])
        finish(u)


def _fox(qkv, crow, *, batch, seq, bk=512, rb=128, ahead=2):
    t = qkv.shape[0]
    n_pairs = qkv.shape[1] // (3 * LANES)
    return pl.pallas_call(
        functools.partial(_fox_kernel, bk=bk, rb=rb, ahead=ahead),
        grid=(batch, n_pairs),
        in_specs=[
            pl.BlockSpec((seq, LANES), lambda b, h: (b, h)),
            pl.BlockSpec((seq, LANES), lambda b, h: (b, n_pairs + h)),
            pl.BlockSpec((seq, LANES), lambda b, h: (b, 2 * n_pairs + h)),
            pl.BlockSpec((crow.shape[0], seq), lambda b, h: (0, b)),
        ],
        out_specs=pl.BlockSpec((seq, LANES), lambda b, h: (b, h)),
        out_shape=jax.ShapeDtypeStruct((t, n_pairs * LANES), BF16),
        compiler_params=_params(2),
        name="fox_attention",
    )(qkv, qkv, qkv, crow)


def _memkv_kernel(m_ref, g_ref, w_ref, o_ref):
    o_ref[...] = _dot(_rms(m_ref[...], g_ref[...]).astype(BF16), w_ref[...]).astype(BF16)


def _memkv(mem2, g, w, tm=512):
    t, d = mem2.shape
    return pl.pallas_call(
        _memkv_kernel,
        grid=(t // tm,),
        in_specs=[pl.BlockSpec((tm, d), lambda i: (i, 0)), _const_spec(g.shape), _const_spec(w.shape)],
        out_specs=pl.BlockSpec((tm, w.shape[1]), lambda i: (i, 0)),
        out_shape=jax.ShapeDtypeStruct((t, w.shape[1]), BF16),
        compiler_params=_params(1),
        name="mem_kv",
    )(mem2, g, w)


def _rwkv_scan_kernel(at_ref, rt_ref, bt_ref, kt_ref, bh_ref, kh_ref, v_ref, gam_ref, bonus_ref, g_ref,
                      gng_ref, gnb_ref, bdavg_ref, o_ref, state_ref, y_ref):
    @pl.when(pl.program_id(1) == 0)
    def _():
        state_ref[...] = jnp.zeros_like(state_ref)

    n_t, w = v_ref.shape
    n_pairs = w // LANES
    n_chunks = n_t // CHUNK
    lane = lax.broadcasted_iota(jnp.int32, (CHUNK, LANES), 1)
    first = lane < HEAD_DIM
    ri = lax.broadcasted_iota(jnp.int32, (PAIR, PAIR), 0)
    ci = lax.broadcasted_iota(jnp.int32, (PAIR, PAIR), 1)
    strict = ri > ci
    eye = jnp.where(ri == ci, 1.0, 0.0)
    incl2 = (lax.broadcasted_iota(jnp.int32, (PAIR, 2 * PAIR), 0)
             >= (lax.broadcasted_iota(jnp.int32, (PAIR, 2 * PAIR), 1) & (PAIR - 1)))

    def stacked(ref, c, p):
        x = ref[c * CHUNK:(c + 1) * CHUNK, p * LANES:(p + 1) * LANES]
        z = jnp.zeros_like(x)
        return jnp.concatenate([jnp.where(first, x, z), jnp.where(first, z, x)], axis=0)

    pairs = range(n_pairs)
    states = [state_ref[p] for p in pairs]
    p_mat, d_mat, m_t, c_t = {}, {}, {}, {}

    def chain_step(c):
        h_bf = [states[p].astype(BF16) for p in pairs]
        for p in pairs:
            y_s = _dot_nt(p_mat.pop((c, p)), h_bf[p]) + d_mat.pop((c, p))
            gam = gam_ref[0, c:c + 1, p * LANES:(p + 1) * LANES]
            states[p] = states[p] * gam + _dot(h_bf[p], m_t.pop((c, p))) + c_t.pop((c, p))
            y_ref[c * CHUNK:(c + 1) * CHUNK, p * LANES:(p + 1) * LANES] = y_s[:CHUNK] + y_s[CHUNK:]

    def chunk_stages(cps):
        gram, s_inv, q_pow, a_ak, a_rbk, x_rhs, sx = {}, {}, {}, {}, {}, {}, {}
        for cp in cps:
            lhs = jnp.concatenate([stacked(at_ref, *cp), stacked(rt_ref, *cp)], axis=0)
            rhs = jnp.concatenate([stacked(bt_ref, *cp), stacked(kt_ref, *cp)], axis=0)
            gram[cp] = _dot_nt(lhs, rhs)
        yield
        for cp in cps:
            g_all = gram.pop(cp)
            n_ab = jnp.where(strict, g_all[:PAIR, :PAIR], 0.0)
            a_ak[cp] = jnp.where(strict, g_all[:PAIR, PAIR:], 0.0).astype(BF16)
            a_rbk[cp] = jnp.where(incl2, g_all[PAIR:, :], 0.0).astype(BF16)
            s_inv[cp] = eye + n_ab
            q_pow[cp] = n_ab.astype(BF16)
        for cp in cps:
            q_pow[cp] = _dot(q_pow[cp], q_pow[cp]).astype(BF16)
            x_rhs[cp] = _dot(a_ak.pop(cp), stacked(v_ref, *cp))
        yield
        span = 2
        while 2 * span < CHUNK:
            for cp in cps:
                both = _dot(q_pow[cp], jnp.concatenate([s_inv[cp].astype(BF16), q_pow[cp]], axis=1))
                s_inv[cp] = s_inv[cp] + both[:, :PAIR]
                q_pow[cp] = both[:, PAIR:].astype(BF16)
            span *= 2
            yield
        for cp in cps:
            s_inv[cp] = (s_inv[cp] + _dot(q_pow.pop(cp), s_inv[cp].astype(BF16))).astype(BF16)
        yield
        for cp in cps:
            x = jnp.concatenate([stacked(at_ref, *cp), x_rhs.pop(cp).astype(BF16)], axis=1)
            sx[cp] = _dot(s_inv.pop(cp), x).astype(BF16)
        yield
        for cp in cps:
            zero = jnp.zeros((PAIR, PAIR), BF16)
            lower_rows = jnp.concatenate([zero, stacked(v_ref, *cp)], axis=1)
            full = jnp.concatenate([sx.pop(cp), lower_rows], axis=0)
            pd = _dot(a_rbk.pop(cp), full)
            p_mat[cp] = (pd[:, :PAIR] + stacked(rt_ref, *cp).astype(F32)).astype(BF16)
            d_mat[cp] = pd[:, PAIR:]
            mc = _dot_tn(full, jnp.concatenate([stacked(bh_ref, *cp), stacked(kh_ref, *cp)], axis=0))
            m_t[cp] = mc[:PAIR].astype(BF16)
            c_t[cp] = mc[PAIR:]
        yield

    group = max(1, n_chunks // 2)
    pending = []
    for c0 in range(0, n_chunks, group):
        chunks = range(c0, min(c0 + group, n_chunks))
        for _ in chunk_stages([(c, p) for c in chunks for p in pairs]):
            if pending:
                chain_step(pending.pop(0))
        pending += list(chunks)
    for c in pending:
        chain_step(c)
    for p in pairs:
        state_ref[p] = states[p]

    y = y_ref[...]
    bdavg = bdavg_ref[...]
    dev = y - _dot_terms_rhs(y, bdavg, HEAD_SUM_TERMS + 1)
    var = _dot_terms_rhs(dev * dev, bdavg, HEAD_SUM_TERMS)
    yn = dev * lax.rsqrt(var + GN_EPS) * gng_ref[...] + gnb_ref[...]
    o_ref[...] = ((yn + bonus_ref[...]) * g_ref[...]).astype(o_ref.dtype)


def _rwkv_scan(at, rt, bt, kt, bh, kh, v, gam, bonus, g, gn_g, gn_b, bdavg, *, batch, seq, tl=SCAN_ROWS):
    t, w = v.shape
    n_sel = tl // CHUNK
    gam = gam.reshape(t // tl, n_sel, w)
    nt = seq // tl
    tok = pl.BlockSpec((tl, w), lambda b, i: (b * nt + i, 0))
    return pl.pallas_call(
        _rwkv_scan_kernel,
        grid=(batch, nt),
        in_specs=[tok] * 7 + [pl.BlockSpec((1, n_sel, w), lambda b, i: (b * nt + i, 0, 0)), tok, tok,
                              _const_spec(gn_g.shape), _const_spec(gn_b.shape), _const_spec(bdavg.shape)],
        out_specs=tok,
        out_shape=jax.ShapeDtypeStruct((t, w), BF16),
        scratch_shapes=[pltpu.VMEM((w // LANES, LANES, LANES), F32), pltpu.VMEM((tl, w), F32)],
        compiler_params=_params(2),
        name="rwkv_scan",
    )(at, rt, bt, kt, bh, kh, v, gam, bonus, g, gn_g, gn_b, bdavg)


def _merge_kernel(x_ref, fox_ref, rw_ref, mq_ref, kv_ref, gate_ref, wf_ref, wr_ref, wm_ref, wo_ref, g_ref, o_ref):
    d = x_ref.shape[1]
    w = mq_ref.shape[1]
    hd = w // MEM_HEADS
    heads = range(MEM_HEADS)
    scores = [_dot_nt(mq_ref[:, h * hd:(h + 1) * hd], kv_ref[:, h * hd:(h + 1) * hd]) for h in heads]
    merged = (gate_ref[:, 0:d].astype(F32) * _dot(fox_ref[...], wf_ref[...])
              + gate_ref[:, d:2 * d].astype(F32) * _dot(rw_ref[...], wr_ref[...]))
    probs, norms = [], []
    for h in heads:
        p = jnp.exp(scores[h] - jnp.max(scores[h], axis=-1, keepdims=True))
        norms.append(jnp.sum(p, axis=-1, keepdims=True))
        probs.append(p.astype(BF16))
    mem_out = jnp.concatenate(
        [(_dot(probs[h], kv_ref[:, w + h * hd:w + (h + 1) * hd]) / norms[h]).astype(BF16) for h in heads], axis=1)
    merged = merged + gate_ref[:, 2 * d:3 * d].astype(F32) * _dot(mem_out, wm_ref[...])
    y = _dot(merged.astype(BF16), wo_ref[...])
    o_ref[...] = x_ref[...] + _rms(y, g_ref[...])


def _merge(x2, fox, rw, mq, memkv, gates, wf, wr, wm, wo, g, *, seq, mem_len, tm=1024):
    t, d = x2.shape
    assert seq % tm == 0
    tiles_per_seq = seq // tm
    row = lambda width: pl.BlockSpec((tm, width), lambda i: (i, 0))
    return pl.pallas_call(
        _merge_kernel,
        grid=(t // tm,),
        in_specs=[row(d), row(fox.shape[1]), row(rw.shape[1]), row(mq.shape[1]),
                  pl.BlockSpec((mem_len, memkv.shape[1]), lambda i: (i // tiles_per_seq, 0)), row(gates.shape[1]),
                  _const_spec(wf.shape), _const_spec(wr.shape), _const_spec(wm.shape), _const_spec(wo.shape),
                  _const_spec(g.shape)],
        out_specs=row(d),
        out_shape=jax.ShapeDtypeStruct((t, d), F32),
        compiler_params=_params(1),
        name="merge_out",
    )(x2, fox, rw, mq, memkv, gates, wf, wr, wm, wo, g)


def _ffn_kernel(h_ref, g1_ref, wg_ref, wu_ref, wd_ref, g2_ref, o_ref, *, ff_chunk):
    h = h_ref[...]
    u = _rms(h, g1_ref[...]).astype(BF16)
    d_ff = wg_ref.shape[1]
    acc = jnp.zeros(h.shape, F32)
    for lo in range(0, d_ff, ff_chunk):
        hi = min(lo + ff_chunk, d_ff)
        gt = _dot(u, wg_ref[:, lo:hi])
        up = _dot(u, wu_ref[:, lo:hi])
        act = (gt * _sigmoid(gt) * up).astype(BF16)
        acc = acc + _dot(act, wd_ref[lo:hi, :])
    o_ref[...] = h + _rms(acc, g2_ref[...])


def _ffn(h2, g1, wg, wu, wd, g2, tm=1024, ff_chunk=1024):
    t, d = h2.shape
    row = pl.BlockSpec((tm, d), lambda i: (i, 0))
    return pl.pallas_call(
        functools.partial(_ffn_kernel, ff_chunk=ff_chunk),
        grid=(t // tm,),
        in_specs=[row, _const_spec(g1.shape), _const_spec(wg.shape), _const_spec(wu.shape),
                  _const_spec(wd.shape), _const_spec(g2.shape)],
        out_specs=row,
        out_shape=jax.ShapeDtypeStruct((t, d), F32),
        compiler_params=_params(1),
        name="ffn",
    )(h2, g1, wg, wu, wd, g2)


def _block_diag_ones(width, block, value):
    idx = jnp.arange(width) // block
    return jnp.where(idx[:, None] == idx[None, :], value, 0.0).astype(BF16)


def kernel(x, mem, pre1_g, post1_g, pre2_g, post2_g, mem_norm_g, w_in, fox_f_bias, rwkv_mu, rwkv_w0,
           rwkv_w_up, rwkv_a0, rwkv_a_up, rwkv_g_up, rwkv_k_k, rwkv_k_a, rwkv_r_k, rwkv_gn_g, rwkv_gn_b,
           w_mem_kv, w_fox_out, w_rwkv_out, w_mem_out, w_o, w_ffn_gate, w_ffn_up, w_ffn_down):
    batch, seq, d = x.shape
    mem_len = mem.shape[1]
    depth = w_in.shape[0]
    fox_heads = fox_f_bias.shape[1]
    fox_w = fox_heads * HEAD_DIM
    rw_w = rwkv_w0.shape[1]
    rw_cols = rwkv_mu.shape[1]
    mem_w = w_mem_kv.shape[2] // 2
    dec_lora = rwkv_w_up.shape[1]
    aaa_lora = rwkv_a_up.shape[1]
    assert dec_lora + aaa_lora == LANES and dec_lora == aaa_lora
    assert fox_heads <= SUBLANES and fox_w % LANES == 0 and rw_w % LANES == 0
    row2 = lambda v: v.reshape(1, -1)

    h = x.reshape(batch * seq, d)
    mem2 = mem.reshape(batch * mem_len, d)
    bd_ones = _block_diag_ones(rw_w, HEAD_DIM, 1.0)
    bd_avg = _block_diag_ones(rw_w, HEAD_DIM, 1.0 / HEAD_DIM)
    for l in range(depth):
        wi = w_in[l]
        o_ff = 3 * fox_w
        o_rw = o_ff + fox_heads
        o_mq = o_rw + rw_cols
        o_gt = o_mq + mem_w
        scale = HEAD_DIM ** -0.5 * LOG2E
        sections = [("qkv", jnp.concatenate([wi[:, :fox_w] * scale, wi[:, fox_w:o_ff]], axis=1)),
                    ("ff", jnp.pad(wi[:, o_ff:o_rw], ((0, 0), (0, LANES - fox_heads)))),
                    ("rwkv", wi[:, o_rw:o_mq]), ("memq", wi[:, o_mq:o_gt]), ("gate", wi[:, o_gt:])]
        w_all = jnp.concatenate([sec for _, sec in sections], axis=1).astype(BF16)
        cols, start = {}, 0
        for name, sec in sections:
            assert sec.shape[1] % LANES == 0
            cols[name] = (start, start + sec.shape[1])
            start += sec.shape[1]
        fbias = jnp.pad(row2(fox_f_bias[l]), ((0, 0), (0, LANES - fox_heads)))
        wlora = jnp.zeros((LANES, 2 * rw_w), F32)
        wlora = wlora.at[:dec_lora, :rw_w].set(rwkv_w_up[l]).at[dec_lora:, rw_w:].set(rwkv_a_up[l]).astype(BF16)

        qkv, crow, mq, gates, *scan_in = _inproj(
            h, row2(pre1_g[l]), w_all, cols, fbias, row2(rwkv_mu[l]), row2(rwkv_w0[l]), row2(rwkv_a0[l]),
            row2(rwkv_k_k[l]), row2(rwkv_k_a[l]), row2(rwkv_r_k[l]), wlora, rwkv_g_up[l].astype(BF16), bd_ones,
            seq=seq, mem_scale=(mem_w // MEM_HEADS) ** -0.5)

        fox_out = _fox(qkv, crow, batch=batch, seq=seq)

        memkv = _memkv(mem2, row2(mem_norm_g[l]), w_mem_kv[l].astype(BF16))
        rwkv_out = _rwkv_scan(*scan_in, row2(rwkv_gn_g[l]), row2(rwkv_gn_b[l]), bd_avg, batch=batch, seq=seq)

        h = _merge(h, fox_out, rwkv_out, mq, memkv, gates, w_fox_out[l].astype(BF16),
                   w_rwkv_out[l].astype(BF16), w_mem_out[l].astype(BF16), w_o[l].astype(BF16),
                   row2(post1_g[l]), seq=seq, mem_len=mem_len)
        h = _ffn(h, row2(pre2_g[l]), w_ffn_gate[l].astype(BF16), w_ffn_up[l].astype(BF16),
                 w_ffn_down[l].astype(BF16), row2(post2_g[l]))
    return h.reshape(batch, seq, d).astype(x.dtype)
```

```python
import functools

import jax
import jax.numpy as jnp
from jax import lax
from jax.experimental import pallas as pl
from jax.experimental.pallas import tpu as pltpu

F32 = jnp.float32
BF16 = jnp.bfloat16

NORM_EPS = 1e-6
GN_EPS = 64e-5
HEAD_DIM = 64
LANES = 128
SUBLANES = 8
LOG2E = 1.4426950408889634
MEM_HEADS = 4
CHUNK = 64
PAIR = 2 * CHUNK
F32_TERMS = 3
DECAY_SUM_TERMS = 2
HEAD_SUM_TERMS = 1
INPROJ_GROUP = 256
INPROJ_ROWS = 512
SCAN_ROWS = 512
NEG_BIG = -1e30
VMEM_LIMIT = 56 * 1024 * 1024


def _dot(a, b):
    return jnp.dot(a, b, preferred_element_type=F32)


def _dot_nt(a, b):
    return lax.dot_general(a, b, (((1,), (1,)), ((), ())), preferred_element_type=F32)


def _dot_tn(a, b):
    return lax.dot_general(a, b, (((0,), (0,)), ((), ())), preferred_element_type=F32)


def _bf16_terms(x, n):
    terms = []
    for _ in range(n - 1):
        t = x.astype(BF16)
        terms.append(t)
        x = x - t.astype(F32)
    return terms + [x.astype(BF16)]


def _dot_terms_lhs(m_bf16, x, n):
    return sum(_dot(m_bf16, t) for t in _bf16_terms(x, n))


def _dot_terms_rhs(x, m_bf16, n):
    return sum(_dot(t, m_bf16) for t in _bf16_terms(x, n))


def _rms(xf, g):
    return xf * lax.rsqrt(jnp.mean(xf * xf, axis=-1, keepdims=True) + NORM_EPS) * g


def _sigmoid(x):
    return 1.0 / (1.0 + jnp.exp(-x))


def _log_sigmoid(x):
    return jnp.minimum(x, 0.0) - jnp.log(1.0 + jnp.exp(-jnp.abs(x)))


def _const_spec(shape):
    nd = len(shape)
    return pl.BlockSpec(shape, lambda *_: (0,) * nd, pipeline_mode=pl.Buffered(1))


def _params(n_axes):
    return pltpu.CompilerParams(dimension_semantics=("arbitrary",) * n_axes,
                                vmem_limit_bytes=VMEM_LIMIT)


def _inproj_kernel(x_ref, g_ref, w_ref, fb_ref, mu_ref, w0_ref, a0_ref, kk_ref, ka_ref, rk_ref,
                   wlora_ref, gup_ref, bd_ref,
                   qkv_ref, crow_ref, mq_ref, gate_ref,
                   at_ref, rt_ref, bt_ref, kt_ref, bh_ref, kh_ref, v_ref, gam_ref, bonus_ref, g_out_ref,
                   carry_ref, prev_ref, *, tiles_per_seq, mem_scale, cols):
    i = pl.program_id(0)
    w = v_ref.shape[1]

    @pl.when(i % tiles_per_seq == 0)
    def _():
        carry_ref[...] = jnp.zeros_like(carry_ref)
        prev_ref[...] = jnp.zeros_like(prev_ref)

    n_groups = x_ref.shape[0] // INPROJ_GROUP
    for gi in range(n_groups):
        rows = slice(gi * INPROJ_GROUP, (gi + 1) * INPROJ_GROUP)
        _inproj_rows(rows, gi, x_ref, g_ref, w_ref, fb_ref, mu_ref, w0_ref, a0_ref, kk_ref, ka_ref, rk_ref,
                     wlora_ref, gup_ref, bd_ref, qkv_ref, crow_ref, mq_ref, gate_ref,
                     at_ref, rt_ref, bt_ref, kt_ref, bh_ref, kh_ref, v_ref, gam_ref, bonus_ref, g_out_ref,
                     carry_ref, prev_ref, w=w, mem_scale=mem_scale, cols=cols)


def _inproj_rows(rows, gi, x_ref, g_ref, w_ref, fb_ref, mu_ref, w0_ref, a0_ref, kk_ref, ka_ref, rk_ref,
                 wlora_ref, gup_ref, bd_ref, qkv_ref, crow_ref, mq_ref, gate_ref,
                 at_ref, rt_ref, bt_ref, kt_ref, bh_ref, kh_ref, v_ref, gam_ref, bonus_ref, g_out_ref,
                 carry_ref, prev_ref, *, w, mem_scale, cols):
    tm = INPROJ_GROUP
    chunks_per_group = tm // CHUNK

    def proj(name, lo=0, hi=None):
        start, stop = cols[name]
        return _dot(u, w_ref[:, start + lo:(stop if hi is None else start + hi)])

    u = _rms(x_ref[rows, :], g_ref[...]).astype(BF16)

    p = proj("rwkv")
    rowid = lax.broadcasted_iota(jnp.int32, p.shape, 0)
    shifted = jnp.where(rowid == 0, prev_ref[...], pltpu.roll(p, 1, axis=0))
    prev_ref[...] = p[tm - 1:tm, :]
    xs = p + (shifted - p) * mu_ref[...]
    r = xs[:, 0:w]
    k = xs[:, w:2 * w]
    v = xs[:, 2 * w:3 * w]
    lora_in = xs[:, 3 * w:3 * w + LANES]
    gd = xs[:, 3 * w + LANES:]
    lane = lax.broadcasted_iota(jnp.int32, lora_in.shape, 1)
    lora_in = jnp.where(lane < lora_in.shape[1] // 2, jnp.tanh(lora_in), lora_in)
    qkv_ref[rows, :] = proj("qkv").astype(BF16)
    lora = _dot(lora_in.astype(BF16), wlora_ref[...])
    g_out_ref[rows, :] = _dot(_sigmoid(gd).astype(BF16), gup_ref[...])
    w_log = -jnp.exp(_log_sigmoid(w0_ref[...] + lora[:, :w]) - 0.5)
    a = _sigmoid(a0_ref[...] + lora[:, w:])

    d = gate_ref.shape[1] // 3
    gate_ref[rows, 0:d] = _sigmoid(proj("gate", 0, d)).astype(BF16)

    bd = bd_ref[...]
    kk = k * kk_ref[...]
    kk = kk * lax.rsqrt(jnp.maximum(_dot_terms_rhs(kk * kk, bd, HEAD_SUM_TERMS), 1e-24))
    k2 = k * (1.0 + (a - 1.0) * ka_ref[...])
    bonus_ref[rows, :] = _dot_terms_rhs(r * k2 * rk_ref[...], bd, HEAD_SUM_TERMS) * v

    ti = lax.broadcasted_iota(jnp.int32, (tm, tm), 0)
    si = lax.broadcasted_iota(jnp.int32, (tm, tm), 1)
    same = (ti // CHUNK) == (si // CHUNK)
    lower = jnp.where(same & (si <= ti), 1.0, 0.0).astype(BF16)
    upper = jnp.where(same & (si > ti), 1.0, 0.0).astype(BF16)
    w_terms = _bf16_terms(w_log, DECAY_SUM_TERMS)
    cum = sum(_dot(lower, t) for t in w_terms)
    rem = sum(_dot(upper, t) for t in w_terms)
    ci = lax.broadcasted_iota(jnp.int32, (SUBLANES, tm), 0)
    sj = lax.broadcasted_iota(jnp.int32, (SUBLANES, tm), 1)
    sel = jnp.where(ci == sj // CHUNK, 1.0, 0.0).astype(BF16)
    gam = jnp.exp(sum(_dot(sel, t) for t in w_terms))
    gam_ref[0, gi * chunks_per_group:(gi + 1) * chunks_per_group, :] = gam[:chunks_per_group]

    gate_ref[rows, d:2 * d] = _sigmoid(proj("gate", d, 2 * d)).astype(BF16)

    e_prev = jnp.exp(cum - w_log)
    e_cum = jnp.exp(cum)
    e_neg = jnp.exp(-cum)
    e_rem = jnp.exp(rem)
    b = kk * a
    at_ref[rows, :] = (-kk * e_prev).astype(BF16)
    rt_ref[rows, :] = (r * e_cum).astype(BF16)
    bt_ref[rows, :] = (b * e_neg).astype(BF16)
    kt_ref[rows, :] = (k2 * e_neg).astype(BF16)
    bh_ref[rows, :] = (b * e_rem).astype(BF16)
    kh_ref[rows, :] = (k2 * e_rem).astype(BF16)
    v_ref[rows, :] = v.astype(BF16)

    gate_ref[rows, 2 * d:3 * d] = _sigmoid(proj("gate", 2 * d, 3 * d)).astype(BF16)
    mq_ref[rows, :] = (proj("memq") * mem_scale).astype(BF16)

    ls = _log_sigmoid(proj("ff") + fb_ref[...])
    tri = jnp.where(ti >= si, 1.0, 0.0).astype(BF16)
    c = _dot_terms_lhs(tri, ls, F32_TERMS) + carry_ref[...]
    carry_ref[...] = c[tm - 1:tm, :]
    crow_ref[:, rows] = jnp.transpose(c * LOG2E)[:crow_ref.shape[0], :]


def _inproj(x2, g, w_all, cols, fbias, mu, w0, a0, k_k, k_a, r_k, wlora, gup, bd, *, seq, mem_scale,
            tm=INPROJ_ROWS):
    t, d = x2.shape
    w = w0.shape[1]
    n_tiles = t // tm
    width = lambda name: cols[name][1] - cols[name][0]
    tok = lambda n, dt: jax.ShapeDtypeStruct((t, n), dt)
    n_sel = max(SUBLANES, tm // CHUNK)
    outs = ([tok(width("qkv"), BF16), jax.ShapeDtypeStruct((SUBLANES, t), F32), tok(width("memq"), BF16),
             tok(width("gate"), BF16)] + [tok(w, BF16)] * 7
            + [jax.ShapeDtypeStruct((n_tiles, n_sel, w), F32), tok(w, F32), tok(w, F32)])
    row = lambda n: pl.BlockSpec((tm, n), lambda i: (i, 0))
    consts = (g, w_all, fbias, mu, w0, a0, k_k, k_a, r_k, wlora, gup, bd)
    gam_index = 11
    results = list(pl.pallas_call(
        functools.partial(_inproj_kernel, tiles_per_seq=seq // tm, mem_scale=mem_scale, cols=cols),
        grid=(n_tiles,),
        in_specs=[row(d)] + [_const_spec(c.shape) for c in consts],
        out_specs=([row(width("qkv")), pl.BlockSpec((SUBLANES, tm), lambda i: (0, i)), row(width("memq")),
                    row(width("gate"))] + [row(w)] * 7
                   + [pl.BlockSpec((1, n_sel, w), lambda i: (i, 0, 0)), row(w), row(w)]),
        out_shape=outs,
        scratch_shapes=[pltpu.VMEM((1, LANES), F32), pltpu.VMEM((1, cols["rwkv"][1] - cols["rwkv"][0]), F32)],
        compiler_params=_params(1),
        name="inproj",
    )(x2, *consts))
    results[gam_index] = results[gam_index][:, :tm // CHUNK]
    return results


def _fox_kernel(q_ref, k_ref, v_ref, crow_ref, o_ref, *, bk, rb, ahead):
    hp = pl.program_id(1)
    seq = q_ref.shape[0]
    first_q = lax.broadcasted_iota(jnp.int32, (rb, LANES), 1) < HEAD_DIM
    lower = (lax.broadcasted_iota(jnp.int32, (rb, rb), 0) >= lax.broadcasted_iota(jnp.int32, (rb, rb), 1))
    sub = lax.broadcasted_iota(jnp.int32, crow_ref.shape, 0)
    crow_all = crow_ref[...]
    crow = [jnp.sum(jnp.where(sub == 2 * hp + e, crow_all, 0.0), axis=0, keepdims=True) for e in range(2)]

    units = []
    for r in range(seq // rb):
        stop = (r + 1) * rb
        units += [(r, k0, min(bk, stop - k0)) for k0 in range(0, stop, bk)]
    units.sort(key=lambda u: (u[0] * rb // bk, u[1], u[0]))

    scores, state = {}, {}

    def issue_scores(u):
        r, k0, width = u
        q2 = q_ref[r * rb:(r + 1) * rb, :]
        zero = jnp.zeros_like(q2)
        kt = k_ref[k0:k0 + width, :]
        scores[u, 0] = _dot_nt(jnp.where(first_q, q2, zero), kt)
        scores[u, 1] = _dot_nt(jnp.where(first_q, zero, q2), kt)

    def finish(u):
        r, k0, width = u
        vt = v_ref[k0:k0 + width, :]
        one = jnp.ones_like(vt)
        first_k = lax.broadcasted_iota(jnp.int32, vt.shape, 1) < HEAD_DIM
        on_diagonal = k0 + width == (r + 1) * rb
        for e in range(2):
            v_aug = jnp.where(first_k, vt, one) if e == 0 else jnp.where(first_k, one, vt)
            s = scores.pop((u, e)) - crow[e][:, k0:k0 + width]
            if on_diagonal:
                tail = jnp.where(lower, s[:, width - rb:], NEG_BIG)
                s = tail if width == rb else jnp.concatenate([s[:, :width - rb], tail], axis=1)
            m_tile = jnp.max(s, axis=-1, keepdims=True)
            if k0 == 0:
                m_new = m_tile
                acc = _dot(jnp.exp2(s - m_new).astype(BF16), v_aug)
            else:
                m_old, acc_old = state[r, e]
                m_new = jnp.maximum(m_old, m_tile)
                acc = jnp.exp2(m_old - m_new) * acc_old + _dot(jnp.exp2(s - m_new).astype(BF16), v_aug)
            state[r, e] = (m_new, acc)
        if on_diagonal:
            a0, a1 = state.pop((r, 0))[1], state.pop((r, 1))[1]
            den = jnp.where(first_q, pltpu.roll(a0, HEAD_DIM, axis=1), pltpu.roll(a1, HEAD_DIM, axis=1))
            o_ref[r * rb:(r + 1) * rb, :] = (jnp.where(first_q, a0, a1) / den).astype(o_ref.dtype)

    for u in units[:ahead]:
        issue_scores(u)
    for n, u in enumerate(units):
        if n + ahead < len(units):
            issue_scores(units[n + ahead])
        finish(u)


def _fox(qkv, crow, *, batch, seq, bk=512, rb=128, ahead=2):
    t = qkv.shape[0]
    n_pairs = qkv.shape[1] // (3 * LANES)
    return pl.pallas_call(
        functools.partial(_fox_kernel, bk=bk, rb=rb, ahead=ahead),
        grid=(batch, n_pairs),
        in_specs=[
            pl.BlockSpec((seq, LANES), lambda b, h: (b, h)),
            pl.BlockSpec((seq, LANES), lambda b, h: (b, n_pairs + h)),
            pl.BlockSpec((seq, LANES), lambda b, h: (b, 2 * n_pairs + h)),
            pl.BlockSpec((crow.shape[0], seq), lambda b, h: (0, b)),
        ],
        out_specs=pl.BlockSpec((seq, LANES), lambda b, h: (b, h)),
        out_shape=jax.ShapeDtypeStruct((t, n_pairs * LANES), BF16),
        compiler_params=_params(2),
        name="fox_attention",
    )(qkv, qkv, qkv, crow)


def _memkv_kernel(m_ref, g_ref, w_ref, o_ref):
    o_ref[...] = _dot(_rms(m_ref[...], g_ref[...]).astype(BF16), w_ref[...]).astype(BF16)


def _memkv(mem2, g, w, tm=512):
    t, d = mem2.shape
    return pl.pallas_call(
        _memkv_kernel,
        grid=(t // tm,),
        in_specs=[pl.BlockSpec((tm, d), lambda i: (i, 0)), _const_spec(g.shape), _const_spec(w.shape)],
        out_specs=pl.BlockSpec((tm, w.shape[1]), lambda i: (i, 0)),
        out_shape=jax.ShapeDtypeStruct((t, w.shape[1]), BF16),
        compiler_params=_params(1),
        name="mem_kv",
    )(mem2, g, w)


def _rwkv_scan_kernel(at_ref, rt_ref, bt_ref, kt_ref, bh_ref, kh_ref, v_ref, gam_ref, bonus_ref, g_ref,
                      gng_ref, gnb_ref, bdavg_ref, o_ref, state_ref, y_ref):
    @pl.when(pl.program_id(1) == 0)
    def _():
        state_ref[...] = jnp.zeros_like(state_ref)

    n_t, w = v_ref.shape
    n_pairs = w // LANES
    n_chunks = n_t // CHUNK
    lane = lax.broadcasted_iota(jnp.int32, (CHUNK, LANES), 1)
    first = lane < HEAD_DIM
    ri = lax.broadcasted_iota(jnp.int32, (PAIR, PAIR), 0)
    ci = lax.broadcasted_iota(jnp.int32, (PAIR, PAIR), 1)
    strict = ri > ci
    eye = jnp.where(ri == ci, 1.0, 0.0)
    incl2 = (lax.broadcasted_iota(jnp.int32, (PAIR, 2 * PAIR), 0)
             >= (lax.broadcasted_iota(jnp.int32, (PAIR, 2 * PAIR), 1) & (PAIR - 1)))

    def stacked(ref, c, p):
        x = ref[c * CHUNK:(c + 1) * CHUNK, p * LANES:(p + 1) * LANES]
        z = jnp.zeros_like(x)
        return jnp.concatenate([jnp.where(first, x, z), jnp.where(first, z, x)], axis=0)

    pairs = range(n_pairs)
    states = [state_ref[p] for p in pairs]
    p_mat, d_mat, m_t, c_t = {}, {}, {}, {}

    def chain_step(c):
        h_bf = [states[p].astype(BF16) for p in pairs]
        for p in pairs:
            y_s = _dot_nt(p_mat.pop((c, p)), h_bf[p]) + d_mat.pop((c, p))
            gam = gam_ref[0, c:c + 1, p * LANES:(p + 1) * LANES]
            states[p] = states[p] * gam + _dot(h_bf[p], m_t.pop((c, p))) + c_t.pop((c, p))
            y_ref[c * CHUNK:(c + 1) * CHUNK, p * LANES:(p + 1) * LANES] = y_s[:CHUNK] + y_s[CHUNK:]

    def chunk_stages(cps):
        gram, s_inv, q_pow, a_ak, a_rbk, x_rhs, sx = {}, {}, {}, {}, {}, {}, {}
        for cp in cps:
            lhs = jnp.concatenate([stacked(at_ref, *cp), stacked(rt_ref, *cp)], axis=0)
            rhs = jnp.concatenate([stacked(bt_ref, *cp), stacked(kt_ref, *cp)], axis=0)
            gram[cp] = _dot_nt(lhs, rhs)
        yield
        for cp in cps:
            g_all = gram.pop(cp)
            n_ab = jnp.where(strict, g_all[:PAIR, :PAIR], 0.0)
            a_ak[cp] = jnp.where(strict, g_all[:PAIR, PAIR:], 0.0).astype(BF16)
            a_rbk[cp] = jnp.where(incl2, g_all[PAIR:, :], 0.0).astype(BF16)
            s_inv[cp] = eye + n_ab
            q_pow[cp] = n_ab.astype(BF16)
        for cp in cps:
            q_pow[cp] = _dot(q_pow[cp], q_pow[cp]).astype(BF16)
            x_rhs[cp] = _dot(a_ak.pop(cp), stacked(v_ref, *cp))
        yield
        span = 2
        while 2 * span < CHUNK:
            for cp in cps:
                both = _dot(q_pow[cp], jnp.concatenate([s_inv[cp].astype(BF16), q_pow[cp]], axis=1))
                s_inv[cp] = s_inv[cp] + both[:, :PAIR]
                q_pow[cp] = both[:, PAIR:].astype(BF16)
            span *= 2
            yield
        for cp in cps:
            s_inv[cp] = (s_inv[cp] + _dot(q_pow.pop(cp), s_inv[cp].astype(BF16))).astype(BF16)
        yield
        for cp in cps:
            x = jnp.concatenate([stacked(at_ref, *cp), x_rhs.pop(cp).astype(BF16)], axis=1)
            sx[cp] = _dot(s_inv.pop(cp), x).astype(BF16)
        yield
        for cp in cps:
            zero = jnp.zeros((PAIR, PAIR), BF16)
            lower_rows = jnp.concatenate([zero, stacked(v_ref, *cp)], axis=1)
            full = jnp.concatenate([sx.pop(cp), lower_rows], axis=0)
            pd = _dot(a_rbk.pop(cp), full)
            p_mat[cp] = (pd[:, :PAIR] + stacked(rt_ref, *cp).astype(F32)).astype(BF16)
            d_mat[cp] = pd[:, PAIR:]
            mc = _dot_tn(full, jnp.concatenate([stacked(bh_ref, *cp), stacked(kh_ref, *cp)], axis=0))
            m_t[cp] = mc[:PAIR].astype(BF16)
            c_t[cp] = mc[PAIR:]
        yield

    group = max(1, n_chunks // 2)
    pending = []
    for c0 in range(0, n_chunks, group):
        chunks = range(c0, min(c0 + group, n_chunks))
        for _ in chunk_stages([(c, p) for c in chunks for p in pairs]):
            if pending:
                chain_step(pending.pop(0))
        pending += list(chunks)
    for c in pending:
        chain_step(c)
    for p in pairs:
        state_ref[p] = states[p]

    y = y_ref[...]
    bdavg = bdavg_ref[...]
    dev = y - _dot_terms_rhs(y, bdavg, HEAD_SUM_TERMS + 1)
    var = _dot_terms_rhs(dev * dev, bdavg, HEAD_SUM_TERMS)
    yn = dev * lax.rsqrt(var + GN_EPS) * gng_ref[...] + gnb_ref[...]
    o_ref[...] = ((yn + bonus_ref[...]) * g_ref[...]).astype(o_ref.dtype)


def _rwkv_scan(at, rt, bt, kt, bh, kh, v, gam, bonus, g, gn_g, gn_b, bdavg, *, batch, seq, tl=SCAN_ROWS):
    t, w = v.shape
    n_sel = tl // CHUNK
    gam = gam.reshape(t // tl, n_sel, w)
    nt = seq // tl
    tok = pl.BlockSpec((tl, w), lambda b, i: (b * nt + i, 0))
    return pl.pallas_call(
        _rwkv_scan_kernel,
        grid=(batch, nt),
        in_specs=[tok] * 7 + [pl.BlockSpec((1, n_sel, w), lambda b, i: (b * nt + i, 0, 0)), tok, tok,
                              _const_spec(gn_g.shape), _const_spec(gn_b.shape), _const_spec(bdavg.shape)],
        out_specs=tok,
        out_shape=jax.ShapeDtypeStruct((t, w), BF16),
        scratch_shapes=[pltpu.VMEM((w // LANES, LANES, LANES), F32), pltpu.VMEM((tl, w), F32)],
        compiler_params=_params(2),
        name="rwkv_scan",
    )(at, rt, bt, kt, bh, kh, v, gam, bonus, g, gn_g, gn_b, bdavg)


def _merge_kernel(x_ref, fox_ref, rw_ref, mq_ref, kv_ref, gate_ref, wf_ref, wr_ref, wm_ref, wo_ref, g_ref, o_ref):
    d = x_ref.shape[1]
    w = mq_ref.shape[1]
    hd = w // MEM_HEADS
    heads = range(MEM_HEADS)
    scores = [_dot_nt(mq_ref[:, h * hd:(h + 1) * hd], kv_ref[:, h * hd:(h + 1) * hd]) for h in heads]
    merged = (gate_ref[:, 0:d].astype(F32) * _dot(fox_ref[...], wf_ref[...])
              + gate_ref[:, d:2 * d].astype(F32) * _dot(rw_ref[...], wr_ref[...]))
    probs, norms = [], []
    for h in heads:
        p = jnp.exp(scores[h] - jnp.max(scores[h], axis=-1, keepdims=True))
        norms.append(jnp.sum(p, axis=-1, keepdims=True))
        probs.append(p.astype(BF16))
    mem_out = jnp.concatenate(
        [(_dot(probs[h], kv_ref[:, w + h * hd:w + (h + 1) * hd]) / norms[h]).astype(BF16) for h in heads], axis=1)
    merged = merged + gate_ref[:, 2 * d:3 * d].astype(F32) * _dot(mem_out, wm_ref[...])
    y = _dot(merged.astype(BF16), wo_ref[...])
    o_ref[...] = x_ref[...] + _rms(y, g_ref[...])


def _merge(x2, fox, rw, mq, memkv, gates, wf, wr, wm, wo, g, *, seq, mem_len, tm=1024):
    t, d = x2.shape
    assert seq % tm == 0
    tiles_per_seq = seq // tm
    row = lambda width: pl.BlockSpec((tm, width), lambda i: (i, 0))
    return pl.pallas_call(
        _merge_kernel,
        grid=(t // tm,),
        in_specs=[row(d), row(fox.shape[1]), row(rw.shape[1]), row(mq.shape[1]),
                  pl.BlockSpec((mem_len, memkv.shape[1]), lambda i: (i // tiles_per_seq, 0)), row(gates.shape[1]),
                  _const_spec(wf.shape), _const_spec(wr.shape), _const_spec(wm.shape), _const_spec(wo.shape),
                  _const_spec(g.shape)],
        out_specs=row(d),
        out_shape=jax.ShapeDtypeStruct((t, d), F32),
        compiler_params=_params(1),
        name="merge_out",
    )(x2, fox, rw, mq, memkv, gates, wf, wr, wm, wo, g)


def _ffn_kernel(h_ref, g1_ref, wg_ref, wu_ref, wd_ref, g2_ref, o_ref, *, ff_chunk):
    h = h_ref[...]
    u = _rms(h, g1_ref[...]).astype(BF16)
    d_ff = wg_ref.shape[1]
    acc = jnp.zeros(h.shape, F32)
    for lo in range(0, d_ff, ff_chunk):
        hi = min(lo + ff_chunk, d_ff)
        gt = _dot(u, wg_ref[:, lo:hi])
        up = _dot(u, wu_ref[:, lo:hi])
        act = (gt * _sigmoid(gt) * up).astype(BF16)
        acc = acc + _dot(act, wd_ref[lo:hi, :])
    o_ref[...] = h + _rms(acc, g2_ref[...])


def _ffn(h2, g1, wg, wu, wd, g2, tm=1024, ff_chunk=1024):
    t, d = h2.shape
    row = pl.BlockSpec((tm, d), lambda i: (i, 0))
    return pl.pallas_call(
        functools.partial(_ffn_kernel, ff_chunk=ff_chunk),
        grid=(t // tm,),
        in_specs=[row, _const_spec(g1.shape), _const_spec(wg.shape), _const_spec(wu.shape),
                  _const_spec(wd.shape), _const_spec(g2.shape)],
        out_specs=row,
        out_shape=jax.ShapeDtypeStruct((t, d), F32),
        compiler_params=_params(1),
        name="ffn",
    )(h2, g1, wg, wu, wd, g2)


def _block_diag_ones(width, block, value):
    idx = jnp.arange(width) // block
    return jnp.where(idx[:, None] == idx[None, :], value, 0.0).astype(BF16)


def kernel(x, mem, pre1_g, post1_g, pre2_g, post2_g, mem_norm_g, w_in, fox_f_bias, rwkv_mu, rwkv_w0,
           rwkv_w_up, rwkv_a0, rwkv_a_up, rwkv_g_up, rwkv_k_k, rwkv_k_a, rwkv_r_k, rwkv_gn_g, rwkv_gn_b,
           w_mem_kv, w_fox_out, w_rwkv_out, w_mem_out, w_o, w_ffn_gate, w_ffn_up, w_ffn_down):
    batch, seq, d = x.shape
    mem_len = mem.shape[1]
    depth = w_in.shape[0]
    fox_heads = fox_f_bias.shape[1]
    fox_w = fox_heads * HEAD_DIM
    rw_w = rwkv_w0.shape[1]
    rw_cols = rwkv_mu.shape[1]
    mem_w = w_mem_kv.shape[2] // 2
    dec_lora = rwkv_w_up.shape[1]
    aaa_lora = rwkv_a_up.shape[1]
    assert dec_lora + aaa_lora == LANES and dec_lora == aaa_lora
    assert fox_heads <= SUBLANES and fox_w % LANES == 0 and rw_w % LANES == 0
    row2 = lambda v: v.reshape(1, -1)

    h = x.reshape(batch * seq, d)
    mem2 = mem.reshape(batch * mem_len, d)
    bd_ones = _block_diag_ones(rw_w, HEAD_DIM, 1.0)
    bd_avg = _block_diag_ones(rw_w, HEAD_DIM, 1.0 / HEAD_DIM)
    for l in range(depth):
        wi = w_in[l]
        o_ff = 3 * fox_w
        o_rw = o_ff + fox_heads
        o_mq = o_rw + rw_cols
        o_gt = o_mq + mem_w
        scale = HEAD_DIM ** -0.5 * LOG2E
        sections = [("qkv", jnp.concatenate([wi[:, :fox_w] * scale, wi[:, fox_w:o_ff]], axis=1)),
                    ("ff", jnp.pad(wi[:, o_ff:o_rw], ((0, 0), (0, LANES - fox_heads)))),
                    ("rwkv", wi[:, o_rw:o_mq]), ("memq", wi[:, o_mq:o_gt]), ("gate", wi[:, o_gt:])]
        w_all = jnp.concatenate([sec for _, sec in sections], axis=1).astype(BF16)
        cols, start = {}, 0
        for name, sec in sections:
            assert sec.shape[1] % LANES == 0
            cols[name] = (start, start + sec.shape[1])
            start += sec.shape[1]
        fbias = jnp.pad(row2(fox_f_bias[l]), ((0, 0), (0, LANES - fox_heads)))
        wlora = jnp.zeros((LANES, 2 * rw_w), F32)
        wlora = wlora.at[:dec_lora, :rw_w].set(rwkv_w_up[l]).at[dec_lora:, rw_w:].set(rwkv_a_up[l]).astype(BF16)

        qkv, crow, mq, gates, *scan_in = _inproj(
            h, row2(pre1_g[l]), w_all, cols, fbias, row2(rwkv_mu[l]), row2(rwkv_w0[l]), row2(rwkv_a0[l]),
            row2(rwkv_k_k[l]), row2(rwkv_k_a[l]), row2(rwkv_r_k[l]), wlora, rwkv_g_up[l].astype(BF16), bd_ones,
            seq=seq, mem_scale=(mem_w // MEM_HEADS) ** -0.5)

        fox_out = _fox(qkv, crow, batch=batch, seq=seq)

        memkv = _memkv(mem2, row2(mem_norm_g[l]), w_mem_kv[l].astype(BF16))
        rwkv_out = _rwkv_scan(*scan_in, row2(rwkv_gn_g[l]), row2(rwkv_gn_b[l]), bd_avg, batch=batch, seq=seq)

        h = _merge(h, fox_out, rwkv_out, mq, memkv, gates, w_fox_out[l].astype(BF16),
                   w_rwkv_out[l].astype(BF16), w_mem_out[l].astype(BF16), w_o[l].astype(BF16),
                   row2(post1_g[l]), seq=seq, mem_len=mem_len)
        h = _ffn(h, row2(pre2_g[l]), w_ffn_gate[l].astype(BF16), w_ffn_up[l].astype(BF16),
                 w_ffn_down[l].astype(BF16), row2(post2_g[l]))
    return h.reshape(batch, seq, d).astype(x.dtype)
```

```python
import functools

import jax
import jax.numpy as jnp
from jax import lax
from jax.experimental import pallas as pl
from jax.experimental.pallas import tpu as pltpu

F32 = jnp.float32
BF16 = jnp.bfloat16

NORM_EPS = 1e-6
GN_EPS = 64e-5
HEAD_DIM = 64
LANES = 128
SUBLANES = 8
LOG2E = 1.4426950408889634
MEM_HEADS = 4
CHUNK = 64
PAIR = 2 * CHUNK
F32_TERMS = 3
DECAY_SUM_TERMS = 2
HEAD_SUM_TERMS = 1
INPROJ_GROUP = 256
INPROJ_ROWS = 512
SCAN_ROWS = 1024
SCAN_GROUP_CHUNKS = 4
NEG_BIG = -1e30
VMEM_LIMIT = 56 * 1024 * 1024


def _dot(a, b):
    return jnp.dot(a, b, preferred_element_type=F32)


def _dot_nt(a, b):
    return lax.dot_general(a, b, (((1,), (1,)), ((), ())), preferred_element_type=F32)


def _dot_tn(a, b):
    return lax.dot_general(a, b, (((0,), (0,)), ((), ())), preferred_element_type=F32)


def _bf16_terms(x, n):
    terms = []
    for _ in range(n - 1):
        t = x.astype(BF16)
        terms.append(t)
        x = x - t.astype(F32)
    return terms + [x.astype(BF16)]


def _dot_terms_lhs(m_bf16, x, n):
    return sum(_dot(m_bf16, t) for t in _bf16_terms(x, n))


def _dot_terms_rhs(x, m_bf16, n):
    return sum(_dot(t, m_bf16) for t in _bf16_terms(x, n))


def _rms(xf, g):
    return xf * lax.rsqrt(jnp.mean(xf * xf, axis=-1, keepdims=True) + NORM_EPS) * g


def _sigmoid(x):
    return 1.0 / (1.0 + jnp.exp(-x))


def _log_sigmoid(x):
    return jnp.minimum(x, 0.0) - jnp.log(1.0 + jnp.exp(-jnp.abs(x)))


def _const_spec(shape):
    nd = len(shape)
    return pl.BlockSpec(shape, lambda *_: (0,) * nd, pipeline_mode=pl.Buffered(1))


def _params(n_axes):
    return pltpu.CompilerParams(dimension_semantics=("arbitrary",) * n_axes,
                                vmem_limit_bytes=VMEM_LIMIT)


def _inproj_kernel(x_ref, g_ref, w_ref, fb_ref, mu_ref, w0_ref, a0_ref, kk_ref, ka_ref, rk_ref,
                   wlora_ref, gup_ref, bd_ref,
                   qkv_ref, crow_ref, mq_ref, gate_ref,
                   at_ref, rt_ref, bt_ref, kt_ref, bh_ref, kh_ref, v_ref, gam_ref, bonus_ref, g_out_ref,
                   carry_ref, prev_ref, *, tiles_per_seq, mem_scale, cols):
    i = pl.program_id(0)
    w = v_ref.shape[1]

    @pl.when(i % tiles_per_seq == 0)
    def _():
        carry_ref[...] = jnp.zeros_like(carry_ref)
        prev_ref[...] = jnp.zeros_like(prev_ref)

    n_groups = x_ref.shape[0] // INPROJ_GROUP
    for gi in range(n_groups):
        rows = slice(gi * INPROJ_GROUP, (gi + 1) * INPROJ_GROUP)
        _inproj_rows(rows, gi, x_ref, g_ref, w_ref, fb_ref, mu_ref, w0_ref, a0_ref, kk_ref, ka_ref, rk_ref,
                     wlora_ref, gup_ref, bd_ref, qkv_ref, crow_ref, mq_ref, gate_ref,
                     at_ref, rt_ref, bt_ref, kt_ref, bh_ref, kh_ref, v_ref, gam_ref, bonus_ref, g_out_ref,
                     carry_ref, prev_ref, w=w, mem_scale=mem_scale, cols=cols)


def _inproj_rows(rows, gi, x_ref, g_ref, w_ref, fb_ref, mu_ref, w0_ref, a0_ref, kk_ref, ka_ref, rk_ref,
                 wlora_ref, gup_ref, bd_ref, qkv_ref, crow_ref, mq_ref, gate_ref,
                 at_ref, rt_ref, bt_ref, kt_ref, bh_ref, kh_ref, v_ref, gam_ref, bonus_ref, g_out_ref,
                 carry_ref, prev_ref, *, w, mem_scale, cols):
    tm = INPROJ_GROUP
    chunks_per_group = tm // CHUNK

    def proj(name, lo=0, hi=None):
        start, stop = cols[name]
        return _dot(u, w_ref[:, start + lo:(stop if hi is None else start + hi)])

    u = _rms(x_ref[rows, :], g_ref[...]).astype(BF16)

    p = proj("rwkv")
    rowid = lax.broadcasted_iota(jnp.int32, p.shape, 0)
    shifted = jnp.where(rowid == 0, prev_ref[...], pltpu.roll(p, 1, axis=0))
    prev_ref[...] = p[tm - 1:tm, :]
    xs = p + (shifted - p) * mu_ref[...]
    r = xs[:, 0:w]
    k = xs[:, w:2 * w]
    v = xs[:, 2 * w:3 * w]
    lora_in = xs[:, 3 * w:3 * w + LANES]
    gd = xs[:, 3 * w + LANES:]
    lane = lax.broadcasted_iota(jnp.int32, lora_in.shape, 1)
    lora_in = jnp.where(lane < lora_in.shape[1] // 2, jnp.tanh(lora_in), lora_in)
    qkv_ref[rows, :] = proj("qkv").astype(BF16)
    lora = _dot(lora_in.astype(BF16), wlora_ref[...])
    g_out_ref[rows, :] = _dot(_sigmoid(gd).astype(BF16), gup_ref[...])
    w_log = -jnp.exp(_log_sigmoid(w0_ref[...] + lora[:, :w]) - 0.5)
    a = _sigmoid(a0_ref[...] + lora[:, w:])

    d = gate_ref.shape[1] // 3
    gate_ref[rows, 0:d] = _sigmoid(proj("gate", 0, d)).astype(BF16)

    bd = bd_ref[...]
    kk = k * kk_ref[...]
    kk = kk * lax.rsqrt(jnp.maximum(_dot_terms_rhs(kk * kk, bd, HEAD_SUM_TERMS), 1e-24))
    k2 = k * (1.0 + (a - 1.0) * ka_ref[...])
    bonus_ref[rows, :] = _dot_terms_rhs(r * k2 * rk_ref[...], bd, HEAD_SUM_TERMS) * v

    ti = lax.broadcasted_iota(jnp.int32, (tm, tm), 0)
    si = lax.broadcasted_iota(jnp.int32, (tm, tm), 1)
    same = (ti // CHUNK) == (si // CHUNK)
    lower = jnp.where(same & (si <= ti), 1.0, 0.0).astype(BF16)
    upper = jnp.where(same & (si > ti), 1.0, 0.0).astype(BF16)
    w_terms = _bf16_terms(w_log, DECAY_SUM_TERMS)
    cum = sum(_dot(lower, t) for t in w_terms)
    rem = sum(_dot(upper, t) for t in w_terms)
    ci = lax.broadcasted_iota(jnp.int32, (SUBLANES, tm), 0)
    sj = lax.broadcasted_iota(jnp.int32, (SUBLANES, tm), 1)
    sel = jnp.where(ci == sj // CHUNK, 1.0, 0.0).astype(BF16)
    gam = jnp.exp(sum(_dot(sel, t) for t in w_terms))
    gam_ref[0, gi * chunks_per_group:(gi + 1) * chunks_per_group, :] = gam[:chunks_per_group]

    gate_ref[rows, d:2 * d] = _sigmoid(proj("gate", d, 2 * d)).astype(BF16)

    e_prev = jnp.exp(cum - w_log)
    e_cum = jnp.exp(cum)
    e_neg = jnp.exp(-cum)
    e_rem = jnp.exp(rem)
    b = kk * a
    at_ref[rows, :] = (-kk * e_prev).astype(BF16)
    rt_ref[rows, :] = (r * e_cum).astype(BF16)
    bt_ref[rows, :] = (b * e_neg).astype(BF16)
    kt_ref[rows, :] = (k2 * e_neg).astype(BF16)
    bh_ref[rows, :] = (b * e_rem).astype(BF16)
    kh_ref[rows, :] = (k2 * e_rem).astype(BF16)
    v_ref[rows, :] = v.astype(BF16)

    gate_ref[rows, 2 * d:3 * d] = _sigmoid(proj("gate", 2 * d, 3 * d)).astype(BF16)
    mq_ref[rows, :] = (proj("memq") * mem_scale).astype(BF16)

    ls = _log_sigmoid(proj("ff") + fb_ref[...])
    tri = jnp.where(ti >= si, 1.0, 0.0).astype(BF16)
    c = _dot_terms_lhs(tri, ls, F32_TERMS) + carry_ref[...]
    carry_ref[...] = c[tm - 1:tm, :]
    crow_ref[:, rows] = jnp.transpose(c * LOG2E)[:crow_ref.shape[0], :]


def _inproj(x2, g, w_all, cols, fbias, mu, w0, a0, k_k, k_a, r_k, wlora, gup, bd, *, seq, mem_scale,
            tm=INPROJ_ROWS):
    t, d = x2.shape
    w = w0.shape[1]
    n_tiles = t // tm
    width = lambda name: cols[name][1] - cols[name][0]
    tok = lambda n, dt: jax.ShapeDtypeStruct((t, n), dt)
    n_sel = max(SUBLANES, tm // CHUNK)
    outs = ([tok(width("qkv"), BF16), jax.ShapeDtypeStruct((SUBLANES, t), F32), tok(width("memq"), BF16),
             tok(width("gate"), BF16)] + [tok(w, BF16)] * 7
            + [jax.ShapeDtypeStruct((n_tiles, n_sel, w), F32), tok(w, F32), tok(w, F32)])
    row = lambda n: pl.BlockSpec((tm, n), lambda i: (i, 0))
    consts = (g, w_all, fbias, mu, w0, a0, k_k, k_a, r_k, wlora, gup, bd)
    gam_index = 11
    results = list(pl.pallas_call(
        functools.partial(_inproj_kernel, tiles_per_seq=seq // tm, mem_scale=mem_scale, cols=cols),
        grid=(n_tiles,),
        in_specs=[row(d)] + [_const_spec(c.shape) for c in consts],
        out_specs=([row(width("qkv")), pl.BlockSpec((SUBLANES, tm), lambda i: (0, i)), row(width("memq")),
                    row(width("gate"))] + [row(w)] * 7
                   + [pl.BlockSpec((1, n_sel, w), lambda i: (i, 0, 0)), row(w), row(w)]),
        out_shape=outs,
        scratch_shapes=[pltpu.VMEM((1, LANES), F32), pltpu.VMEM((1, cols["rwkv"][1] - cols["rwkv"][0]), F32)],
        compiler_params=_params(1),
        name="inproj",
    )(x2, *consts))
    results[gam_index] = results[gam_index][:, :tm // CHUNK]
    return results


def _fox_kernel(q_ref, k_ref, v_ref, crow_ref, o_ref, *, bk, rb, ahead):
    hp = pl.program_id(1)
    seq = q_ref.shape[0]
    first_q = lax.broadcasted_iota(jnp.int32, (rb, LANES), 1) < HEAD_DIM
    lower = (lax.broadcasted_iota(jnp.int32, (rb, rb), 0) >= lax.broadcasted_iota(jnp.int32, (rb, rb), 1))
    sub = lax.broadcasted_iota(jnp.int32, crow_ref.shape, 0)
    crow_all = crow_ref[...]
    crow = [jnp.sum(jnp.where(sub == 2 * hp + e, crow_all, 0.0), axis=0, keepdims=True) for e in range(2)]

    units = []
    for r in range(seq // rb):
        stop = (r + 1) * rb
        units += [(r, k0, min(bk, stop - k0)) for k0 in range(0, stop, bk)]
    units.sort(key=lambda u: (u[0] * rb // bk, u[1], u[0]))

    scores, state = {}, {}

    def issue_scores(u):
        r, k0, width = u
        q2 = q_ref[r * rb:(r + 1) * rb, :]
        zero = jnp.zeros_like(q2)
        kt = k_ref[k0:k0 + width, :]
        scores[u, 0] = _dot_nt(jnp.where(first_q, q2, zero), kt)
        scores[u, 1] = _dot_nt(jnp.where(first_q, zero, q2), kt)

    def finish(u):
        r, k0, width = u
        vt = v_ref[k0:k0 + width, :]
        one = jnp.ones_like(vt)
        first_k = lax.broadcasted_iota(jnp.int32, vt.shape, 1) < HEAD_DIM
        on_diagonal = k0 + width == (r + 1) * rb
        for e in range(2):
            v_aug = jnp.where(first_k, vt, one) if e == 0 else jnp.where(first_k, one, vt)
            s = scores.pop((u, e)) - crow[e][:, k0:k0 + width]
            if on_diagonal:
                tail = jnp.where(lower, s[:, width - rb:], NEG_BIG)
                s = tail if width == rb else jnp.concatenate([s[:, :width - rb], tail], axis=1)
            m_tile = jnp.max(s, axis=-1, keepdims=True)
            if k0 == 0:
                m_new = m_tile
                acc = _dot(jnp.exp2(s - m_new).astype(BF16), v_aug)
            else:
                m_old, acc_old = state[r, e]
                m_new = jnp.maximum(m_old, m_tile)
                acc = jnp.exp2(m_old - m_new) * acc_old + _dot(jnp.exp2(s - m_new).astype(BF16), v_aug)
            state[r, e] = (m_new, acc)
        if on_diagonal:
            a0, a1 = state.pop((r, 0))[1], state.pop((r, 1))[1]
            den = jnp.where(first_q, pltpu.roll(a0, HEAD_DIM, axis=1), pltpu.roll(a1, HEAD_DIM, axis=1))
            o_ref[r * rb:(r + 1) * rb, :] = (jnp.where(first_q, a0, a1) / den).astype(o_ref.dtype)

    for u in units[:ahead]:
        issue_scores(u)
    for n, u in enumerate(units):
        if n + ahead < len(units):
            issue_scores(units[n + ahead])
        finish(u)


def _fox(qkv, crow, *, batch, seq, bk=512, rb=128, ahead=2):
    t = qkv.shape[0]
    n_pairs = qkv.shape[1] // (3 * LANES)
    return pl.pallas_call(
        functools.partial(_fox_kernel, bk=bk, rb=rb, ahead=ahead),
        grid=(batch, n_pairs),
        in_specs=[
            pl.BlockSpec((seq, LANES), lambda b, h: (b, h)),
            pl.BlockSpec((seq, LANES), lambda b, h: (b, n_pairs + h)),
            pl.BlockSpec((seq, LANES), lambda b, h: (b, 2 * n_pairs + h)),
            pl.BlockSpec((crow.shape[0], seq), lambda b, h: (0, b)),
        ],
        out_specs=pl.BlockSpec((seq, LANES), lambda b, h: (b, h)),
        out_shape=jax.ShapeDtypeStruct((t, n_pairs * LANES), BF16),
        compiler_params=_params(2),
        name="fox_attention",
    )(qkv, qkv, qkv, crow)


def _memkv_kernel(m_ref, g_ref, w_ref, o_ref):
    o_ref[...] = _dot(_rms(m_ref[...], g_ref[...]).astype(BF16), w_ref[...]).astype(BF16)


def _memkv(mem2, g, w, tm=512):
    t, d = mem2.shape
    return pl.pallas_call(
        _memkv_kernel,
        grid=(t // tm,),
        in_specs=[pl.BlockSpec((tm, d), lambda i: (i, 0)), _const_spec(g.shape), _const_spec(w.shape)],
        out_specs=pl.BlockSpec((tm, w.shape[1]), lambda i: (i, 0)),
        out_shape=jax.ShapeDtypeStruct((t, w.shape[1]), BF16),
        compiler_params=_params(1),
        name="mem_kv",
    )(mem2, g, w)


def _rwkv_scan_kernel(at_ref, rt_ref, bt_ref, kt_ref, bh_ref, kh_ref, v_ref, gam_ref, bonus_ref, g_ref,
                      gng_ref, gnb_ref, bdavg_ref, o_ref, state_ref, y_ref):
    @pl.when(pl.program_id(1) == 0)
    def _():
        state_ref[...] = jnp.zeros_like(state_ref)

    n_t, w = v_ref.shape
    n_pairs = w // LANES
    n_chunks = n_t // CHUNK
    lane = lax.broadcasted_iota(jnp.int32, (CHUNK, LANES), 1)
    first = lane < HEAD_DIM
    ri = lax.broadcasted_iota(jnp.int32, (PAIR, PAIR), 0)
    ci = lax.broadcasted_iota(jnp.int32, (PAIR, PAIR), 1)
    strict = ri > ci
    eye = jnp.where(ri == ci, 1.0, 0.0)
    incl2 = (lax.broadcasted_iota(jnp.int32, (PAIR, 2 * PAIR), 0)
             >= (lax.broadcasted_iota(jnp.int32, (PAIR, 2 * PAIR), 1) & (PAIR - 1)))

    def stacked(ref, c, p):
        x = ref[c * CHUNK:(c + 1) * CHUNK, p * LANES:(p + 1) * LANES]
        z = jnp.zeros_like(x)
        return jnp.concatenate([jnp.where(first, x, z), jnp.where(first, z, x)], axis=0)

    pairs = range(n_pairs)
    states = [state_ref[p] for p in pairs]
    p_mat, d_mat, m_t, c_t = {}, {}, {}, {}

    def chain_step(c):
        h_bf = [states[p].astype(BF16) for p in pairs]
        for p in pairs:
            y_s = _dot_nt(p_mat.pop((c, p)), h_bf[p]) + d_mat.pop((c, p))
            gam = gam_ref[0, c:c + 1, p * LANES:(p + 1) * LANES]
            states[p] = states[p] * gam + _dot(h_bf[p], m_t.pop((c, p))) + c_t.pop((c, p))
            y_ref[c * CHUNK:(c + 1) * CHUNK, p * LANES:(p + 1) * LANES] = y_s[:CHUNK] + y_s[CHUNK:]

    def chunk_stages(cps):
        gram, s_inv, q_pow, a_ak, a_rbk, x_rhs, sx = {}, {}, {}, {}, {}, {}, {}
        for cp in cps:
            lhs = jnp.concatenate([stacked(at_ref, *cp), stacked(rt_ref, *cp)], axis=0)
            rhs = jnp.concatenate([stacked(bt_ref, *cp), stacked(kt_ref, *cp)], axis=0)
            gram[cp] = _dot_nt(lhs, rhs)
        yield
        for cp in cps:
            g_all = gram.pop(cp)
            n_ab = jnp.where(strict, g_all[:PAIR, :PAIR], 0.0)
            a_ak[cp] = jnp.where(strict, g_all[:PAIR, PAIR:], 0.0).astype(BF16)
            a_rbk[cp] = jnp.where(incl2, g_all[PAIR:, :], 0.0).astype(BF16)
            s_inv[cp] = eye + n_ab
            q_pow[cp] = n_ab.astype(BF16)
        for cp in cps:
            q_pow[cp] = _dot(q_pow[cp], q_pow[cp]).astype(BF16)
            x_rhs[cp] = _dot(a_ak.pop(cp), stacked(v_ref, *cp))
        yield
        span = 2
        while 2 * span < CHUNK:
            for cp in cps:
                both = _dot(q_pow[cp], jnp.concatenate([s_inv[cp].astype(BF16), q_pow[cp]], axis=1))
                s_inv[cp] = s_inv[cp] + both[:, :PAIR]
                q_pow[cp] = both[:, PAIR:].astype(BF16)
            span *= 2
            yield
        for cp in cps:
            s_inv[cp] = (s_inv[cp] + _dot(q_pow.pop(cp), s_inv[cp].astype(BF16))).astype(BF16)
        yield
        for cp in cps:
            x = jnp.concatenate([stacked(at_ref, *cp), x_rhs.pop(cp).astype(BF16)], axis=1)
            sx[cp] = _dot(s_inv.pop(cp), x).astype(BF16)
        yield
        for cp in cps:
            zero = jnp.zeros((PAIR, PAIR), BF16)
            lower_rows = jnp.concatenate([zero, stacked(v_ref, *cp)], axis=1)
            full = jnp.concatenate([sx.pop(cp), lower_rows], axis=0)
            pd = _dot(a_rbk.pop(cp), full)
            p_mat[cp] = (pd[:, :PAIR] + stacked(rt_ref, *cp).astype(F32)).astype(BF16)
            d_mat[cp] = pd[:, PAIR:]
            mc = _dot_tn(full, jnp.concatenate([stacked(bh_ref, *cp), stacked(kh_ref, *cp)], axis=0))
            m_t[cp] = mc[:PAIR].astype(BF16)
            c_t[cp] = mc[PAIR:]
        yield

    group = min(n_chunks, SCAN_GROUP_CHUNKS)
    pending = []
    for c0 in range(0, n_chunks, group):
        chunks = range(c0, min(c0 + group, n_chunks))
        for _ in chunk_stages([(c, p) for c in chunks for p in pairs]):
            if pending:
                chain_step(pending.pop(0))
        pending += list(chunks)
    for c in pending:
        chain_step(c)
    for p in pairs:
        state_ref[p] = states[p]

    y = y_ref[...]
    bdavg = bdavg_ref[...]
    dev = y - _dot_terms_rhs(y, bdavg, HEAD_SUM_TERMS + 1)
    var = _dot_terms_rhs(dev * dev, bdavg, HEAD_SUM_TERMS)
    yn = dev * lax.rsqrt(var + GN_EPS) * gng_ref[...] + gnb_ref[...]
    o_ref[...] = ((yn + bonus_ref[...]) * g_ref[...]).astype(o_ref.dtype)


def _rwkv_scan(at, rt, bt, kt, bh, kh, v, gam, bonus, g, gn_g, gn_b, bdavg, *, batch, seq, tl=SCAN_ROWS):
    t, w = v.shape
    n_sel = tl // CHUNK
    gam = gam.reshape(t // tl, n_sel, w)
    nt = seq // tl
    tok = pl.BlockSpec((tl, w), lambda b, i: (b * nt + i, 0))
    return pl.pallas_call(
        _rwkv_scan_kernel,
        grid=(batch, nt),
        in_specs=[tok] * 7 + [pl.BlockSpec((1, n_sel, w), lambda b, i: (b * nt + i, 0, 0)), tok, tok,
                              _const_spec(gn_g.shape), _const_spec(gn_b.shape), _const_spec(bdavg.shape)],
        out_specs=tok,
        out_shape=jax.ShapeDtypeStruct((t, w), BF16),
        scratch_shapes=[pltpu.VMEM((w // LANES, LANES, LANES), F32), pltpu.VMEM((tl, w), F32)],
        compiler_params=_params(2),
        name="rwkv_scan",
    )(at, rt, bt, kt, bh, kh, v, gam, bonus, g, gn_g, gn_b, bdavg)


def _merge_kernel(x_ref, fox_ref, rw_ref, mq_ref, kv_ref, gate_ref, wf_ref, wr_ref, wm_ref, wo_ref, g_ref, o_ref):
    d = x_ref.shape[1]
    w = mq_ref.shape[1]
    hd = w // MEM_HEADS
    heads = range(MEM_HEADS)
    scores = [_dot_nt(mq_ref[:, h * hd:(h + 1) * hd], kv_ref[:, h * hd:(h + 1) * hd]) for h in heads]
    merged = (gate_ref[:, 0:d].astype(F32) * _dot(fox_ref[...], wf_ref[...])
              + gate_ref[:, d:2 * d].astype(F32) * _dot(rw_ref[...], wr_ref[...]))
    probs, norms = [], []
    for h in heads:
        p = jnp.exp(scores[h] - jnp.max(scores[h], axis=-1, keepdims=True))
        norms.append(jnp.sum(p, axis=-1, keepdims=True))
        probs.append(p.astype(BF16))
    mem_out = jnp.concatenate(
        [(_dot(probs[h], kv_ref[:, w + h * hd:w + (h + 1) * hd]) / norms[h]).astype(BF16) for h in heads], axis=1)
    merged = merged + gate_ref[:, 2 * d:3 * d].astype(F32) * _dot(mem_out, wm_ref[...])
    y = _dot(merged.astype(BF16), wo_ref[...])
    o_ref[...] = x_ref[...] + _rms(y, g_ref[...])


def _merge(x2, fox, rw, mq, memkv, gates, wf, wr, wm, wo, g, *, seq, mem_len, tm=1024):
    t, d = x2.shape
    assert seq % tm == 0
    tiles_per_seq = seq // tm
    row = lambda width: pl.BlockSpec((tm, width), lambda i: (i, 0))
    return pl.pallas_call(
        _merge_kernel,
        grid=(t // tm,),
        in_specs=[row(d), row(fox.shape[1]), row(rw.shape[1]), row(mq.shape[1]),
                  pl.BlockSpec((mem_len, memkv.shape[1]), lambda i: (i // tiles_per_seq, 0)), row(gates.shape[1]),
                  _const_spec(wf.shape), _const_spec(wr.shape), _const_spec(wm.shape), _const_spec(wo.shape),
                  _const_spec(g.shape)],
        out_specs=row(d),
        out_shape=jax.ShapeDtypeStruct((t, d), F32),
        compiler_params=_params(1),
        name="merge_out",
    )(x2, fox, rw, mq, memkv, gates, wf, wr, wm, wo, g)


def _ffn_kernel(h_ref, g1_ref, wg_ref, wu_ref, wd_ref, g2_ref, o_ref, *, ff_chunk):
    h = h_ref[...]
    u = _rms(h, g1_ref[...]).astype(BF16)
    d_ff = wg_ref.shape[1]
    acc = jnp.zeros(h.shape, F32)
    for lo in range(0, d_ff, ff_chunk):
        hi = min(lo + ff_chunk, d_ff)
        gt = _dot(u, wg_ref[:, lo:hi])
        up = _dot(u, wu_ref[:, lo:hi])
        act = (gt * _sigmoid(gt) * up).astype(BF16)
        acc = acc + _dot(act, wd_ref[lo:hi, :])
    o_ref[...] = h + _rms(acc, g2_ref[...])


def _ffn(h2, g1, wg, wu, wd, g2, tm=1024, ff_chunk=1024):
    t, d = h2.shape
    row = pl.BlockSpec((tm, d), lambda i: (i, 0))
    return pl.pallas_call(
        functools.partial(_ffn_kernel, ff_chunk=ff_chunk),
        grid=(t // tm,),
        in_specs=[row, _const_spec(g1.shape), _const_spec(wg.shape), _const_spec(wu.shape),
                  _const_spec(wd.shape), _const_spec(g2.shape)],
        out_specs=row,
        out_shape=jax.ShapeDtypeStruct((t, d), F32),
        compiler_params=_params(1),
        name="ffn",
    )(h2, g1, wg, wu, wd, g2)


def _block_diag_ones(width, block, value):
    idx = jnp.arange(width) // block
    return jnp.where(idx[:, None] == idx[None, :], value, 0.0).astype(BF16)


def kernel(x, mem, pre1_g, post1_g, pre2_g, post2_g, mem_norm_g, w_in, fox_f_bias, rwkv_mu, rwkv_w0,
           rwkv_w_up, rwkv_a0, rwkv_a_up, rwkv_g_up, rwkv_k_k, rwkv_k_a, rwkv_r_k, rwkv_gn_g, rwkv_gn_b,
           w_mem_kv, w_fox_out, w_rwkv_out, w_mem_out, w_o, w_ffn_gate, w_ffn_up, w_ffn_down):
    batch, seq, d = x.shape
    mem_len = mem.shape[1]
    depth = w_in.shape[0]
    fox_heads = fox_f_bias.shape[1]
    fox_w = fox_heads * HEAD_DIM
    rw_w = rwkv_w0.shape[1]
    rw_cols = rwkv_mu.shape[1]
    mem_w = w_mem_kv.shape[2] // 2
    dec_lora = rwkv_w_up.shape[1]
    aaa_lora = rwkv_a_up.shape[1]
    assert dec_lora + aaa_lora == LANES and dec_lora == aaa_lora
    assert fox_heads <= SUBLANES and fox_w % LANES == 0 and rw_w % LANES == 0
    row2 = lambda v: v.reshape(1, -1)

    h = x.reshape(batch * seq, d)
    mem2 = mem.reshape(batch * mem_len, d)
    bd_ones = _block_diag_ones(rw_w, HEAD_DIM, 1.0)
    bd_avg = _block_diag_ones(rw_w, HEAD_DIM, 1.0 / HEAD_DIM)
    for l in range(depth):
        wi = w_in[l]
        o_ff = 3 * fox_w
        o_rw = o_ff + fox_heads
        o_mq = o_rw + rw_cols
        o_gt = o_mq + mem_w
        scale = HEAD_DIM ** -0.5 * LOG2E
        sections = [("qkv", jnp.concatenate([wi[:, :fox_w] * scale, wi[:, fox_w:o_ff]], axis=1)),
                    ("ff", jnp.pad(wi[:, o_ff:o_rw], ((0, 0), (0, LANES - fox_heads)))),
                    ("rwkv", wi[:, o_rw:o_mq]), ("memq", wi[:, o_mq:o_gt]), ("gate", wi[:, o_gt:])]
        w_all = jnp.concatenate([sec for _, sec in sections], axis=1).astype(BF16)
        cols, start = {}, 0
        for name, sec in sections:
            assert sec.shape[1] % LANES == 0
            cols[name] = (start, start + sec.shape[1])
            start += sec.shape[1]
        fbias = jnp.pad(row2(fox_f_bias[l]), ((0, 0), (0, LANES - fox_heads)))
        wlora = jnp.zeros((LANES, 2 * rw_w), F32)
        wlora = wlora.at[:dec_lora, :rw_w].set(rwkv_w_up[l]).at[dec_lora:, rw_w:].set(rwkv_a_up[l]).astype(BF16)

        qkv, crow, mq, gates, *scan_in = _inproj(
            h, row2(pre1_g[l]), w_all, cols, fbias, row2(rwkv_mu[l]), row2(rwkv_w0[l]), row2(rwkv_a0[l]),
            row2(rwkv_k_k[l]), row2(rwkv_k_a[l]), row2(rwkv_r_k[l]), wlora, rwkv_g_up[l].astype(BF16), bd_ones,
            seq=seq, mem_scale=(mem_w // MEM_HEADS) ** -0.5)

        fox_out = _fox(qkv, crow, batch=batch, seq=seq)

        memkv = _memkv(mem2, row2(mem_norm_g[l]), w_mem_kv[l].astype(BF16))
        rwkv_out = _rwkv_scan(*scan_in, row2(rwkv_gn_g[l]), row2(rwkv_gn_b[l]), bd_avg, batch=batch, seq=seq)

        h = _merge(h, fox_out, rwkv_out, mq, memkv, gates, w_fox_out[l].astype(BF16),
                   w_rwkv_out[l].astype(BF16), w_mem_out[l].astype(BF16), w_o[l].astype(BF16),
                   row2(post1_g[l]), seq=seq, mem_len=mem_len)
        h = _ffn(h, row2(pre2_g[l]), w_ffn_gate[l].astype(BF16), w_ffn_up[l].astype(BF16),
                 w_ffn_down[l].astype(BF16), row2(post2_g[l]))
    return h.reshape(batch, seq, d).astype(x.dtype)
```

```python
import functools

import jax
import jax.numpy as jnp
from jax import lax
from jax.experimental import pallas as pl
from jax.experimental.pallas import tpu as pltpu

F32 = jnp.float32
BF16 = jnp.bfloat16

NORM_EPS = 1e-6
GN_EPS = 64e-5
HEAD_DIM = 64
LANES = 128
SUBLANES = 8
LOG2E = 1.4426950408889634
MEM_HEADS = 4
CHUNK = 64
PAIR = 2 * CHUNK
F32_TERMS = 3
DECAY_SUM_TERMS = 2
HEAD_SUM_TERMS = 1
INPROJ_GROUP = 256
INPROJ_ROWS = 512
SCAN_ROWS = 1024
SCAN_GROUP_CHUNKS = 4
NEG_BIG = -1e30
VMEM_LIMIT = 56 * 1024 * 1024


def _dot(a, b):
    return jnp.dot(a, b, preferred_element_type=F32)


def _dot_nt(a, b):
    return lax.dot_general(a, b, (((1,), (1,)), ((), ())), preferred_element_type=F32)


def _dot_tn(a, b):
    return lax.dot_general(a, b, (((0,), (0,)), ((), ())), preferred_element_type=F32)


def _bf16_terms(x, n):
    terms = []
    for _ in range(n - 1):
        t = x.astype(BF16)
        terms.append(t)
        x = x - t.astype(F32)
    return terms + [x.astype(BF16)]


def _dot_terms_lhs(m_bf16, x, n):
    return sum(_dot(m_bf16, t) for t in _bf16_terms(x, n))


def _dot_terms_rhs(x, m_bf16, n):
    return sum(_dot(t, m_bf16) for t in _bf16_terms(x, n))


def _rms(xf, g):
    return xf * lax.rsqrt(jnp.mean(xf * xf, axis=-1, keepdims=True) + NORM_EPS) * g


def _sigmoid(x):
    return 1.0 / (1.0 + jnp.exp(-x))


def _log_sigmoid(x):
    return jnp.minimum(x, 0.0) - jnp.log(1.0 + jnp.exp(-jnp.abs(x)))


def _const_spec(shape):
    nd = len(shape)
    return pl.BlockSpec(shape, lambda *_: (0,) * nd, pipeline_mode=pl.Buffered(1))


def _params(n_axes):
    return pltpu.CompilerParams(dimension_semantics=("arbitrary",) * n_axes,
                                vmem_limit_bytes=VMEM_LIMIT)


def _inproj_kernel(x_ref, g_ref, w_ref, fb_ref, mu_ref, w0_ref, a0_ref, kk_ref, ka_ref, rk_ref,
                   wlora_ref, gup_ref, bd_ref,
                   qkv_ref, crow_ref, mq_ref, gate_ref,
                   at_ref, rt_ref, bt_ref, kt_ref, bh_ref, kh_ref, v_ref, gam_ref, bonus_ref, g_out_ref,
                   carry_ref, prev_ref, *, tiles_per_seq, mem_scale, cols):
    i = pl.program_id(0)
    w = v_ref.shape[1]

    @pl.when(i % tiles_per_seq == 0)
    def _():
        carry_ref[...] = jnp.zeros_like(carry_ref)
        prev_ref[...] = jnp.zeros_like(prev_ref)

    n_groups = x_ref.shape[0] // INPROJ_GROUP
    for gi in range(n_groups):
        rows = slice(gi * INPROJ_GROUP, (gi + 1) * INPROJ_GROUP)
        _inproj_rows(rows, gi, x_ref, g_ref, w_ref, fb_ref, mu_ref, w0_ref, a0_ref, kk_ref, ka_ref, rk_ref,
                     wlora_ref, gup_ref, bd_ref, qkv_ref, crow_ref, mq_ref, gate_ref,
                     at_ref, rt_ref, bt_ref, kt_ref, bh_ref, kh_ref, v_ref, gam_ref, bonus_ref, g_out_ref,
                     carry_ref, prev_ref, w=w, mem_scale=mem_scale, cols=cols)


def _inproj_rows(rows, gi, x_ref, g_ref, w_ref, fb_ref, mu_ref, w0_ref, a0_ref, kk_ref, ka_ref, rk_ref,
                 wlora_ref, gup_ref, bd_ref, qkv_ref, crow_ref, mq_ref, gate_ref,
                 at_ref, rt_ref, bt_ref, kt_ref, bh_ref, kh_ref, v_ref, gam_ref, bonus_ref, g_out_ref,
                 carry_ref, prev_ref, *, w, mem_scale, cols):
    tm = INPROJ_GROUP
    chunks_per_group = tm // CHUNK

    def proj(name, lo=0, hi=None):
        start, stop = cols[name]
        return _dot(u, w_ref[:, start + lo:(stop if hi is None else start + hi)])

    u = _rms(x_ref[rows, :], g_ref[...]).astype(BF16)

    p = proj("rwkv")
    rowid = lax.broadcasted_iota(jnp.int32, p.shape, 0)
    shifted = jnp.where(rowid == 0, prev_ref[...], pltpu.roll(p, 1, axis=0))
    prev_ref[...] = p[tm - 1:tm, :]
    xs = p + (shifted - p) * mu_ref[...]
    r = xs[:, 0:w]
    k = xs[:, w:2 * w]
    v = xs[:, 2 * w:3 * w]
    lora_in = xs[:, 3 * w:3 * w + LANES]
    gd = xs[:, 3 * w + LANES:]
    lane = lax.broadcasted_iota(jnp.int32, lora_in.shape, 1)
    lora_in = jnp.where(lane < lora_in.shape[1] // 2, jnp.tanh(lora_in), lora_in)
    qkv_ref[rows, :] = proj("qkv").astype(BF16)
    lora = _dot(lora_in.astype(BF16), wlora_ref[...])
    g_out_ref[rows, :] = _dot(_sigmoid(gd).astype(BF16), gup_ref[...])
    w_log = -jnp.exp(_log_sigmoid(w0_ref[...] + lora[:, :w]) - 0.5)
    a = _sigmoid(a0_ref[...] + lora[:, w:])

    d = gate_ref.shape[1] // 3
    gate_ref[rows, 0:d] = _sigmoid(proj("gate", 0, d)).astype(BF16)

    bd = bd_ref[...]
    kk = k * kk_ref[...]
    kk = kk * lax.rsqrt(jnp.maximum(_dot_terms_rhs(kk * kk, bd, HEAD_SUM_TERMS), 1e-24))
    k2 = k * (1.0 + (a - 1.0) * ka_ref[...])
    bonus_ref[rows, :] = _dot_terms_rhs(r * k2 * rk_ref[...], bd, HEAD_SUM_TERMS) * v

    ti = lax.broadcasted_iota(jnp.int32, (tm, tm), 0)
    si = lax.broadcasted_iota(jnp.int32, (tm, tm), 1)
    same = (ti // CHUNK) == (si // CHUNK)
    lower = jnp.where(same & (si <= ti), 1.0, 0.0).astype(BF16)
    upper = jnp.where(same & (si > ti), 1.0, 0.0).astype(BF16)
    w_terms = _bf16_terms(w_log, DECAY_SUM_TERMS)
    cum = sum(_dot(lower, t) for t in w_terms)
    rem = sum(_dot(upper, t) for t in w_terms)
    ci = lax.broadcasted_iota(jnp.int32, (SUBLANES, tm), 0)
    sj = lax.broadcasted_iota(jnp.int32, (SUBLANES, tm), 1)
    sel = jnp.where(ci == sj // CHUNK, 1.0, 0.0).astype(BF16)
    gam = jnp.exp(sum(_dot(sel, t) for t in w_terms))
    gam_ref[0, gi * chunks_per_group:(gi + 1) * chunks_per_group, :] = gam[:chunks_per_group]

    gate_ref[rows, d:2 * d] = _sigmoid(proj("gate", d, 2 * d)).astype(BF16)

    e_prev = jnp.exp(cum - w_log)
    e_cum = jnp.exp(cum)
    e_neg = jnp.exp(-cum)
    e_rem = jnp.exp(rem)
    b = kk * a
    at_ref[rows, :] = (-kk * e_prev).astype(BF16)
    rt_ref[rows, :] = (r * e_cum).astype(BF16)
    bt_ref[rows, :] = (b * e_neg).astype(BF16)
    kt_ref[rows, :] = (k2 * e_neg).astype(BF16)
    bh_ref[rows, :] = (b * e_rem).astype(BF16)
    kh_ref[rows, :] = (k2 * e_rem).astype(BF16)
    v_ref[rows, :] = v.astype(BF16)

    gate_ref[rows, 2 * d:3 * d] = _sigmoid(proj("gate", 2 * d, 3 * d)).astype(BF16)
    mq_ref[rows, :] = (proj("memq") * mem_scale).astype(BF16)

    ls = _log_sigmoid(proj("ff") + fb_ref[...])
    tri = jnp.where(ti >= si, 1.0, 0.0).astype(BF16)
    c = _dot_terms_lhs(tri, ls, F32_TERMS) + carry_ref[...]
    carry_ref[...] = c[tm - 1:tm, :]
    crow_ref[:, rows] = jnp.transpose(c * LOG2E)[:crow_ref.shape[0], :]


def _inproj(x2, g, w_all, cols, fbias, mu, w0, a0, k_k, k_a, r_k, wlora, gup, bd, *, seq, mem_scale,
            tm=INPROJ_ROWS):
    t, d = x2.shape
    w = w0.shape[1]
    n_tiles = t // tm
    width = lambda name: cols[name][1] - cols[name][0]
    tok = lambda n, dt: jax.ShapeDtypeStruct((t, n), dt)
    n_sel = max(SUBLANES, tm // CHUNK)
    outs = ([tok(width("qkv"), BF16), jax.ShapeDtypeStruct((SUBLANES, t), F32), tok(width("memq"), BF16),
             tok(width("gate"), BF16)] + [tok(w, BF16)] * 7
            + [jax.ShapeDtypeStruct((n_tiles, n_sel, w), F32), tok(w, F32), tok(w, F32)])
    row = lambda n: pl.BlockSpec((tm, n), lambda i: (i, 0))
    consts = (g, w_all, fbias, mu, w0, a0, k_k, k_a, r_k, wlora, gup, bd)
    gam_index = 11
    results = list(pl.pallas_call(
        functools.partial(_inproj_kernel, tiles_per_seq=seq // tm, mem_scale=mem_scale, cols=cols),
        grid=(n_tiles,),
        in_specs=[row(d)] + [_const_spec(c.shape) for c in consts],
        out_specs=([row(width("qkv")), pl.BlockSpec((SUBLANES, tm), lambda i: (0, i)), row(width("memq")),
                    row(width("gate"))] + [row(w)] * 7
                   + [pl.BlockSpec((1, n_sel, w), lambda i: (i, 0, 0)), row(w), row(w)]),
        out_shape=outs,
        scratch_shapes=[pltpu.VMEM((1, LANES), F32), pltpu.VMEM((1, cols["rwkv"][1] - cols["rwkv"][0]), F32)],
        compiler_params=_params(1),
        name="inproj",
    )(x2, *consts))
    results[gam_index] = results[gam_index][:, :tm // CHUNK]
    return results


def _fox_kernel(q_ref, k_ref, v_ref, crow_ref, o_ref, *, bk, rb, ahead):
    seq = q_ref.shape[0]
    n_local = q_ref.shape[1] // LANES
    first_head = 2 * n_local * pl.program_id(1)
    first_q = lax.broadcasted_iota(jnp.int32, (rb, LANES), 1) < HEAD_DIM
    lower = (lax.broadcasted_iota(jnp.int32, (rb, rb), 0) >= lax.broadcasted_iota(jnp.int32, (rb, rb), 1))
    sub = lax.broadcasted_iota(jnp.int32, crow_ref.shape, 0)
    crow_all = crow_ref[...]
    crow = [jnp.sum(jnp.where(sub == first_head + h, crow_all, 0.0), axis=0, keepdims=True)
            for h in range(2 * n_local)]

    units = []
    for r in range(seq // rb):
        stop = (r + 1) * rb
        units += [(r, k0, min(bk, stop - k0), pp) for k0 in range(0, stop, bk) for pp in range(n_local)]
    units.sort(key=lambda u: (u[0] * rb // bk, u[1], u[0], u[3]))

    scores, state = {}, {}

    def issue_scores(u):
        r, k0, width, pp = u
        lanes = slice(pp * LANES, (pp + 1) * LANES)
        q2 = q_ref[r * rb:(r + 1) * rb, lanes]
        zero = jnp.zeros_like(q2)
        kt = k_ref[k0:k0 + width, lanes]
        scores[u, 0] = _dot_nt(jnp.where(first_q, q2, zero), kt)
        scores[u, 1] = _dot_nt(jnp.where(first_q, zero, q2), kt)

    def finish(u):
        r, k0, width, pp = u
        lanes = slice(pp * LANES, (pp + 1) * LANES)
        vt = v_ref[k0:k0 + width, lanes]
        one = jnp.ones_like(vt)
        first_k = lax.broadcasted_iota(jnp.int32, vt.shape, 1) < HEAD_DIM
        on_diagonal = k0 + width == (r + 1) * rb
        for e in range(2):
            v_aug = jnp.where(first_k, vt, one) if e == 0 else jnp.where(first_k, one, vt)
            s = scores.pop((u, e)) - crow[2 * pp + e][:, k0:k0 + width]
            if on_diagonal:
                tail = jnp.where(lower, s[:, width - rb:], NEG_BIG)
                s = tail if width == rb else jnp.concatenate([s[:, :width - rb], tail], axis=1)
            m_tile = jnp.max(s, axis=-1, keepdims=True)
            if k0 == 0:
                m_new = m_tile
                acc = _dot(jnp.exp2(s - m_new).astype(BF16), v_aug)
            else:
                m_old, acc_old = state[pp, r, e]
                m_new = jnp.maximum(m_old, m_tile)
                acc = jnp.exp2(m_old - m_new) * acc_old + _dot(jnp.exp2(s - m_new).astype(BF16), v_aug)
            state[pp, r, e] = (m_new, acc)
        if on_diagonal:
            a0, a1 = state.pop((pp, r, 0))[1], state.pop((pp, r, 1))[1]
            den = jnp.where(first_q, pltpu.roll(a0, HEAD_DIM, axis=1), pltpu.roll(a1, HEAD_DIM, axis=1))
            o_ref[r * rb:(r + 1) * rb, lanes] = (jnp.where(first_q, a0, a1) / den).astype(o_ref.dtype)

    for u in units[:ahead]:
        issue_scores(u)
    for n, u in enumerate(units):
        if n + ahead < len(units):
            issue_scores(units[n + ahead])
        finish(u)


def _fox(qkv, crow, *, batch, seq, bk=512, rb=128, ahead=2, pairs=2):
    t = qkv.shape[0]
    n_groups = qkv.shape[1] // (3 * LANES * pairs)
    width = pairs * LANES
    return pl.pallas_call(
        functools.partial(_fox_kernel, bk=bk, rb=rb, ahead=ahead),
        grid=(batch, n_groups),
        in_specs=[
            pl.BlockSpec((seq, width), lambda b, h: (b, h)),
            pl.BlockSpec((seq, width), lambda b, h: (b, n_groups + h)),
            pl.BlockSpec((seq, width), lambda b, h: (b, 2 * n_groups + h)),
            pl.BlockSpec((crow.shape[0], seq), lambda b, h: (0, b)),
        ],
        out_specs=pl.BlockSpec((seq, width), lambda b, h: (b, h)),
        out_shape=jax.ShapeDtypeStruct((t, n_groups * width), BF16),
        compiler_params=_params(2),
        name="fox_attention",
    )(qkv, qkv, qkv, crow)


def _memkv_kernel(m_ref, g_ref, w_ref, o_ref):
    o_ref[...] = _dot(_rms(m_ref[...], g_ref[...]).astype(BF16), w_ref[...]).astype(BF16)


def _memkv(mem2, g, w, tm=512):
    t, d = mem2.shape
    return pl.pallas_call(
        _memkv_kernel,
        grid=(t // tm,),
        in_specs=[pl.BlockSpec((tm, d), lambda i: (i, 0)), _const_spec(g.shape), _const_spec(w.shape)],
        out_specs=pl.BlockSpec((tm, w.shape[1]), lambda i: (i, 0)),
        out_shape=jax.ShapeDtypeStruct((t, w.shape[1]), BF16),
        compiler_params=_params(1),
        name="mem_kv",
    )(mem2, g, w)


def _rwkv_scan_kernel(at_ref, rt_ref, bt_ref, kt_ref, bh_ref, kh_ref, v_ref, gam_ref, bonus_ref, g_ref,
                      gng_ref, gnb_ref, bdavg_ref, o_ref, state_ref, y_ref):
    @pl.when(pl.program_id(1) == 0)
    def _():
        state_ref[...] = jnp.zeros_like(state_ref)

    n_t, w = v_ref.shape
    n_pairs = w // LANES
    n_chunks = n_t // CHUNK
    lane = lax.broadcasted_iota(jnp.int32, (CHUNK, LANES), 1)
    first = lane < HEAD_DIM
    ri = lax.broadcasted_iota(jnp.int32, (PAIR, PAIR), 0)
    ci = lax.broadcasted_iota(jnp.int32, (PAIR, PAIR), 1)
    strict = ri > ci
    eye = jnp.where(ri == ci, 1.0, 0.0)
    incl2 = (lax.broadcasted_iota(jnp.int32, (PAIR, 2 * PAIR), 0)
             >= (lax.broadcasted_iota(jnp.int32, (PAIR, 2 * PAIR), 1) & (PAIR - 1)))

    def stacked(ref, c, p):
        x = ref[c * CHUNK:(c + 1) * CHUNK, p * LANES:(p + 1) * LANES]
        z = jnp.zeros_like(x)
        return jnp.concatenate([jnp.where(first, x, z), jnp.where(first, z, x)], axis=0)

    pairs = range(n_pairs)
    states = [state_ref[p] for p in pairs]
    p_mat, d_mat, m_t, c_t = {}, {}, {}, {}

    def chain_step(c):
        h_bf = [states[p].astype(BF16) for p in pairs]
        for p in pairs:
            y_s = _dot_nt(p_mat.pop((c, p)), h_bf[p]) + d_mat.pop((c, p))
            gam = gam_ref[0, c:c + 1, p * LANES:(p + 1) * LANES]
            states[p] = states[p] * gam + _dot(h_bf[p], m_t.pop((c, p))) + c_t.pop((c, p))
            y_ref[c * CHUNK:(c + 1) * CHUNK, p * LANES:(p + 1) * LANES] = y_s[:CHUNK] + y_s[CHUNK:]

    def chunk_stages(cps):
        gram, s_inv, q_pow, a_ak, a_rbk, x_rhs, sx = {}, {}, {}, {}, {}, {}, {}
        for cp in cps:
            lhs = jnp.concatenate([stacked(at_ref, *cp), stacked(rt_ref, *cp)], axis=0)
            rhs = jnp.concatenate([stacked(bt_ref, *cp), stacked(kt_ref, *cp)], axis=0)
            gram[cp] = _dot_nt(lhs, rhs)
        yield
        for cp in cps:
            g_all = gram.pop(cp)
            n_ab = jnp.where(strict, g_all[:PAIR, :PAIR], 0.0)
            a_ak[cp] = jnp.where(strict, g_all[:PAIR, PAIR:], 0.0).astype(BF16)
            a_rbk[cp] = jnp.where(incl2, g_all[PAIR:, :], 0.0).astype(BF16)
            s_inv[cp] = eye + n_ab
            q_pow[cp] = n_ab.astype(BF16)
        for cp in cps:
            q_pow[cp] = _dot(q_pow[cp], q_pow[cp]).astype(BF16)
            x_rhs[cp] = _dot(a_ak.pop(cp), stacked(v_ref, *cp))
        yield
        span = 2
        while 2 * span < CHUNK:
            for cp in cps:
                both = _dot(q_pow[cp], jnp.concatenate([s_inv[cp].astype(BF16), q_pow[cp]], axis=1))
                s_inv[cp] = s_inv[cp] + both[:, :PAIR]
                q_pow[cp] = both[:, PAIR:].astype(BF16)
            span *= 2
            yield
        for cp in cps:
            s_inv[cp] = (s_inv[cp] + _dot(q_pow.pop(cp), s_inv[cp].astype(BF16))).astype(BF16)
        yield
        for cp in cps:
            x = jnp.concatenate([stacked(at_ref, *cp), x_rhs.pop(cp).astype(BF16)], axis=1)
            sx[cp] = _dot(s_inv.pop(cp), x).astype(BF16)
        yield
        for cp in cps:
            zero = jnp.zeros((PAIR, PAIR), BF16)
            lower_rows = jnp.concatenate([zero, stacked(v_ref, *cp)], axis=1)
            full = jnp.concatenate([sx.pop(cp), lower_rows], axis=0)
            pd = _dot(a_rbk.pop(cp), full)
            p_mat[cp] = (pd[:, :PAIR] + stacked(rt_ref, *cp).astype(F32)).astype(BF16)
            d_mat[cp] = pd[:, PAIR:]
            mc = _dot_tn(full, jnp.concatenate([stacked(bh_ref, *cp), stacked(kh_ref, *cp)], axis=0))
            m_t[cp] = mc[:PAIR].astype(BF16)
            c_t[cp] = mc[PAIR:]
        yield

    group = min(n_chunks, SCAN_GROUP_CHUNKS)
    pending = []
    for c0 in range(0, n_chunks, group):
        chunks = range(c0, min(c0 + group, n_chunks))
        for _ in chunk_stages([(c, p) for c in chunks for p in pairs]):
            if pending:
                chain_step(pending.pop(0))
        pending += list(chunks)
    for c in pending:
        chain_step(c)
    for p in pairs:
        state_ref[p] = states[p]

    y = y_ref[...]
    bdavg = bdavg_ref[...]
    dev = y - _dot_terms_rhs(y, bdavg, HEAD_SUM_TERMS + 1)
    var = _dot_terms_rhs(dev * dev, bdavg, HEAD_SUM_TERMS)
    yn = dev * lax.rsqrt(var + GN_EPS) * gng_ref[...] + gnb_ref[...]
    o_ref[...] = ((yn + bonus_ref[...]) * g_ref[...]).astype(o_ref.dtype)


def _rwkv_scan(at, rt, bt, kt, bh, kh, v, gam, bonus, g, gn_g, gn_b, bdavg, *, batch, seq, tl=SCAN_ROWS):
    t, w = v.shape
    n_sel = tl // CHUNK
    gam = gam.reshape(t // tl, n_sel, w)
    nt = seq // tl
    tok = pl.BlockSpec((tl, w), lambda b, i: (b * nt + i, 0))
    return pl.pallas_call(
        _rwkv_scan_kernel,
        grid=(batch, nt),
        in_specs=[tok] * 7 + [pl.BlockSpec((1, n_sel, w), lambda b, i: (b * nt + i, 0, 0)), tok, tok,
                              _const_spec(gn_g.shape), _const_spec(gn_b.shape), _const_spec(bdavg.shape)],
        out_specs=tok,
        out_shape=jax.ShapeDtypeStruct((t, w), BF16),
        scratch_shapes=[pltpu.VMEM((w // LANES, LANES, LANES), F32), pltpu.VMEM((tl, w), F32)],
        compiler_params=_params(2),
        name="rwkv_scan",
    )(at, rt, bt, kt, bh, kh, v, gam, bonus, g, gn_g, gn_b, bdavg)


def _merge_kernel(x_ref, fox_ref, rw_ref, mq_ref, kv_ref, gate_ref, wf_ref, wr_ref, wm_ref, wo_ref, g_ref, o_ref):
    d = x_ref.shape[1]
    w = mq_ref.shape[1]
    hd = w // MEM_HEADS
    heads = range(MEM_HEADS)
    scores = [_dot_nt(mq_ref[:, h * hd:(h + 1) * hd], kv_ref[:, h * hd:(h + 1) * hd]) for h in heads]
    merged = (gate_ref[:, 0:d].astype(F32) * _dot(fox_ref[...], wf_ref[...])
              + gate_ref[:, d:2 * d].astype(F32) * _dot(rw_ref[...], wr_ref[...]))
    probs, norms = [], []
    for h in heads:
        p = jnp.exp(scores[h] - jnp.max(scores[h], axis=-1, keepdims=True))
        norms.append(jnp.sum(p, axis=-1, keepdims=True))
        probs.append(p.astype(BF16))
    mem_out = jnp.concatenate(
        [(_dot(probs[h], kv_ref[:, w + h * hd:w + (h + 1) * hd]) / norms[h]).astype(BF16) for h in heads], axis=1)
    merged = merged + gate_ref[:, 2 * d:3 * d].astype(F32) * _dot(mem_out, wm_ref[...])
    y = _dot(merged.astype(BF16), wo_ref[...])
    o_ref[...] = x_ref[...] + _rms(y, g_ref[...])


def _merge(x2, fox, rw, mq, memkv, gates, wf, wr, wm, wo, g, *, seq, mem_len, tm=1024):
    t, d = x2.shape
    assert seq % tm == 0
    tiles_per_seq = seq // tm
    row = lambda width: pl.BlockSpec((tm, width), lambda i: (i, 0))
    return pl.pallas_call(
        _merge_kernel,
        grid=(t // tm,),
        in_specs=[row(d), row(fox.shape[1]), row(rw.shape[1]), row(mq.shape[1]),
                  pl.BlockSpec((mem_len, memkv.shape[1]), lambda i: (i // tiles_per_seq, 0)), row(gates.shape[1]),
                  _const_spec(wf.shape), _const_spec(wr.shape), _const_spec(wm.shape), _const_spec(wo.shape),
                  _const_spec(g.shape)],
        out_specs=row(d),
        out_shape=jax.ShapeDtypeStruct((t, d), F32),
        compiler_params=_params(1),
        name="merge_out",
    )(x2, fox, rw, mq, memkv, gates, wf, wr, wm, wo, g)


def _ffn_kernel(h_ref, g1_ref, wg_ref, wu_ref, wd_ref, g2_ref, o_ref, *, ff_chunk):
    h = h_ref[...]
    u = _rms(h, g1_ref[...]).astype(BF16)
    d_ff = wg_ref.shape[1]
    acc = jnp.zeros(h.shape, F32)
    for lo in range(0, d_ff, ff_chunk):
        hi = min(lo + ff_chunk, d_ff)
        gt = _dot(u, wg_ref[:, lo:hi])
        up = _dot(u, wu_ref[:, lo:hi])
        act = (gt * _sigmoid(gt) * up).astype(BF16)
        acc = acc + _dot(act, wd_ref[lo:hi, :])
    o_ref[...] = h + _rms(acc, g2_ref[...])


def _ffn(h2, g1, wg, wu, wd, g2, tm=1024, ff_chunk=1024):
    t, d = h2.shape
    row = pl.BlockSpec((tm, d), lambda i: (i, 0))
    return pl.pallas_call(
        functools.partial(_ffn_kernel, ff_chunk=ff_chunk),
        grid=(t // tm,),
        in_specs=[row, _const_spec(g1.shape), _const_spec(wg.shape), _const_spec(wu.shape),
                  _const_spec(wd.shape), _const_spec(g2.shape)],
        out_specs=row,
        out_shape=jax.ShapeDtypeStruct((t, d), F32),
        compiler_params=_params(1),
        name="ffn",
    )(h2, g1, wg, wu, wd, g2)


def _block_diag_ones(width, block, value):
    idx = jnp.arange(width) // block
    return jnp.where(idx[:, None] == idx[None, :], value, 0.0).astype(BF16)


def kernel(x, mem, pre1_g, post1_g, pre2_g, post2_g, mem_norm_g, w_in, fox_f_bias, rwkv_mu, rwkv_w0,
           rwkv_w_up, rwkv_a0, rwkv_a_up, rwkv_g_up, rwkv_k_k, rwkv_k_a, rwkv_r_k, rwkv_gn_g, rwkv_gn_b,
           w_mem_kv, w_fox_out, w_rwkv_out, w_mem_out, w_o, w_ffn_gate, w_ffn_up, w_ffn_down):
    batch, seq, d = x.shape
    mem_len = mem.shape[1]
    depth = w_in.shape[0]
    fox_heads = fox_f_bias.shape[1]
    fox_w = fox_heads * HEAD_DIM
    rw_w = rwkv_w0.shape[1]
    rw_cols = rwkv_mu.shape[1]
    mem_w = w_mem_kv.shape[2] // 2
    dec_lora = rwkv_w_up.shape[1]
    aaa_lora = rwkv_a_up.shape[1]
    assert dec_lora + aaa_lora == LANES and dec_lora == aaa_lora
    assert fox_heads <= SUBLANES and fox_w % LANES == 0 and rw_w % LANES == 0
    row2 = lambda v: v.reshape(1, -1)

    h = x.reshape(batch * seq, d)
    mem2 = mem.reshape(batch * mem_len, d)
    bd_ones = _block_diag_ones(rw_w, HEAD_DIM, 1.0)
    bd_avg = _block_diag_ones(rw_w, HEAD_DIM, 1.0 / HEAD_DIM)
    for l in range(depth):
        wi = w_in[l]
        o_ff = 3 * fox_w
        o_rw = o_ff + fox_heads
        o_mq = o_rw + rw_cols
        o_gt = o_mq + mem_w
        scale = HEAD_DIM ** -0.5 * LOG2E
        sections = [("qkv", jnp.concatenate([wi[:, :fox_w] * scale, wi[:, fox_w:o_ff]], axis=1)),
                    ("ff", jnp.pad(wi[:, o_ff:o_rw], ((0, 0), (0, LANES - fox_heads)))),
                    ("rwkv", wi[:, o_rw:o_mq]), ("memq", wi[:, o_mq:o_gt]), ("gate", wi[:, o_gt:])]
        w_all = jnp.concatenate([sec for _, sec in sections], axis=1).astype(BF16)
        cols, start = {}, 0
        for name, sec in sections:
            assert sec.shape[1] % LANES == 0
            cols[name] = (start, start + sec.shape[1])
            start += sec.shape[1]
        fbias = jnp.pad(row2(fox_f_bias[l]), ((0, 0), (0, LANES - fox_heads)))
        wlora = jnp.zeros((LANES, 2 * rw_w), F32)
        wlora = wlora.at[:dec_lora, :rw_w].set(rwkv_w_up[l]).at[dec_lora:, rw_w:].set(rwkv_a_up[l]).astype(BF16)

        qkv, crow, mq, gates, *scan_in = _inproj(
            h, row2(pre1_g[l]), w_all, cols, fbias, row2(rwkv_mu[l]), row2(rwkv_w0[l]), row2(rwkv_a0[l]),
            row2(rwkv_k_k[l]), row2(rwkv_k_a[l]), row2(rwkv_r_k[l]), wlora, rwkv_g_up[l].astype(BF16), bd_ones,
            seq=seq, mem_scale=(mem_w // MEM_HEADS) ** -0.5)

        fox_out = _fox(qkv, crow, batch=batch, seq=seq)

        memkv = _memkv(mem2, row2(mem_norm_g[l]), w_mem_kv[l].astype(BF16))
        rwkv_out = _rwkv_scan(*scan_in, row2(rwkv_gn_g[l]), row2(rwkv_gn_b[l]), bd_avg, batch=batch, seq=seq)

        h = _merge(h, fox_out, rwkv_out, mq, memkv, gates, w_fox_out[l].astype(BF16),
                   w_rwkv_out[l].astype(BF16), w_mem_out[l].astype(BF16), w_o[l].astype(BF16),
                   row2(post1_g[l]), seq=seq, mem_len=mem_len)
        h = _ffn(h, row2(pre2_g[l]), w_ffn_gate[l].astype(BF16), w_ffn_up[l].astype(BF16),
                 w_ffn_down[l].astype(BF16), row2(post2_g[l]))
    return h.reshape(batch, seq, d).astype(x.dtype)
```

```python
import functools

import jax
import jax.numpy as jnp
from jax import lax
from jax.experimental import pallas as pl
from jax.experimental.pallas import tpu as pltpu

F32 = jnp.float32
BF16 = jnp.bfloat16

NORM_EPS = 1e-6
GN_EPS = 64e-5
HEAD_DIM = 64
LANES = 128
SUBLANES = 8
LOG2E = 1.4426950408889634
MEM_HEADS = 4
CHUNK = 64
PAIR = 2 * CHUNK
F32_TERMS = 3
DECAY_SUM_TERMS = 2
HEAD_SUM_TERMS = 1
NEG_BIG = -1e30
VMEM_LIMIT = 56 * 1024 * 1024

INPROJ_GROUP = 256
INPROJ_ROWS = 512
SCAN_ROWS = 1024
SCAN_GROUP_CHUNKS = 4
FOX_KEY_TILE = 512
FOX_ROW_BLOCK = 128
FOX_SCORES_AHEAD = 2
FOX_PAIRS_PER_STEP = 2
MEMKV_ROWS = 512
MERGE_ROWS = 1024
FFN_ROWS = 1024
FFN_CHUNK = 1024


def _dot(a, b):
    return jnp.dot(a, b, preferred_element_type=F32)


def _dot_nt(a, b):
    return lax.dot_general(a, b, (((1,), (1,)), ((), ())), preferred_element_type=F32)


def _dot_tn(a, b):
    return lax.dot_general(a, b, (((0,), (0,)), ((), ())), preferred_element_type=F32)


def _bf16_terms(x, n):
    terms = []
    for _ in range(n - 1):
        t = x.astype(BF16)
        terms.append(t)
        x = x - t.astype(F32)
    return terms + [x.astype(BF16)]


def _dot_terms_lhs(m_bf16, x, n):
    return sum(_dot(m_bf16, t) for t in _bf16_terms(x, n))


def _dot_terms_rhs(x, m_bf16, n):
    return sum(_dot(t, m_bf16) for t in _bf16_terms(x, n))


def _rms(xf, g):
    return xf * lax.rsqrt(jnp.mean(xf * xf, axis=-1, keepdims=True) + NORM_EPS) * g


def _sigmoid(x):
    return 1.0 / (1.0 + jnp.exp(-x))


def _log_sigmoid(x):
    return jnp.minimum(x, 0.0) - jnp.log(1.0 + jnp.exp(-jnp.abs(x)))


def _const_spec(shape):
    nd = len(shape)
    return pl.BlockSpec(shape, lambda *_: (0,) * nd, pipeline_mode=pl.Buffered(1))


def _params(n_axes):
    return pltpu.CompilerParams(dimension_semantics=("arbitrary",) * n_axes,
                                vmem_limit_bytes=VMEM_LIMIT)


def _inproj_kernel(x_ref, g_ref, w_ref, fb_ref, mu_ref, w0_ref, a0_ref, kk_ref, ka_ref, rk_ref,
                   wlora_ref, gup_ref, bd_ref,
                   qkv_ref, crow_ref, mq_ref, gate_ref,
                   at_ref, rt_ref, bt_ref, kt_ref, bh_ref, kh_ref, v_ref, gam_ref, bonus_ref, g_out_ref,
                   carry_ref, prev_ref, *, tiles_per_seq, mem_scale, cols):
    i = pl.program_id(0)
    w = v_ref.shape[1]

    @pl.when(i % tiles_per_seq == 0)
    def _():
        carry_ref[...] = jnp.zeros_like(carry_ref)
        prev_ref[...] = jnp.zeros_like(prev_ref)

    n_groups = x_ref.shape[0] // INPROJ_GROUP
    for gi in range(n_groups):
        rows = slice(gi * INPROJ_GROUP, (gi + 1) * INPROJ_GROUP)
        _inproj_rows(rows, gi, x_ref, g_ref, w_ref, fb_ref, mu_ref, w0_ref, a0_ref, kk_ref, ka_ref, rk_ref,
                     wlora_ref, gup_ref, bd_ref, qkv_ref, crow_ref, mq_ref, gate_ref,
                     at_ref, rt_ref, bt_ref, kt_ref, bh_ref, kh_ref, v_ref, gam_ref, bonus_ref, g_out_ref,
                     carry_ref, prev_ref, w=w, mem_scale=mem_scale, cols=cols)


def _inproj_rows(rows, gi, x_ref, g_ref, w_ref, fb_ref, mu_ref, w0_ref, a0_ref, kk_ref, ka_ref, rk_ref,
                 wlora_ref, gup_ref, bd_ref, qkv_ref, crow_ref, mq_ref, gate_ref,
                 at_ref, rt_ref, bt_ref, kt_ref, bh_ref, kh_ref, v_ref, gam_ref, bonus_ref, g_out_ref,
                 carry_ref, prev_ref, *, w, mem_scale, cols):
    tm = INPROJ_GROUP
    chunks_per_group = tm // CHUNK

    def proj(name, lo=0, hi=None):
        start, stop = cols[name]
        return _dot(u, w_ref[:, start + lo:(stop if hi is None else start + hi)])

    u = _rms(x_ref[rows, :], g_ref[...]).astype(BF16)

    p = proj("rwkv")
    rowid = lax.broadcasted_iota(jnp.int32, p.shape, 0)
    shifted = jnp.where(rowid == 0, prev_ref[...], pltpu.roll(p, 1, axis=0))
    prev_ref[...] = p[tm - 1:tm, :]
    xs = p + (shifted - p) * mu_ref[...]
    r = xs[:, 0:w]
    k = xs[:, w:2 * w]
    v = xs[:, 2 * w:3 * w]
    lora_in = xs[:, 3 * w:3 * w + LANES]
    gd = xs[:, 3 * w + LANES:]
    lane = lax.broadcasted_iota(jnp.int32, lora_in.shape, 1)
    lora_in = jnp.where(lane < lora_in.shape[1] // 2, jnp.tanh(lora_in), lora_in)
    qkv_ref[rows, :] = proj("qkv").astype(BF16)
    lora = _dot(lora_in.astype(BF16), wlora_ref[...])
    g_out_ref[rows, :] = _dot(_sigmoid(gd).astype(BF16), gup_ref[...])
    w_log = -jnp.exp(_log_sigmoid(w0_ref[...] + lora[:, :w]) - 0.5)
    a = _sigmoid(a0_ref[...] + lora[:, w:])

    d = gate_ref.shape[1] // 3
    gate_ref[rows, 0:d] = _sigmoid(proj("gate", 0, d)).astype(BF16)

    bd = bd_ref[...]
    kk = k * kk_ref[...]
    kk = kk * lax.rsqrt(jnp.maximum(_dot_terms_rhs(kk * kk, bd, HEAD_SUM_TERMS), 1e-24))
    k2 = k * (1.0 + (a - 1.0) * ka_ref[...])
    bonus_ref[rows, :] = _dot_terms_rhs(r * k2 * rk_ref[...], bd, HEAD_SUM_TERMS) * v

    ti = lax.broadcasted_iota(jnp.int32, (tm, tm), 0)
    si = lax.broadcasted_iota(jnp.int32, (tm, tm), 1)
    same = (ti // CHUNK) == (si // CHUNK)
    lower = jnp.where(same & (si <= ti), 1.0, 0.0).astype(BF16)
    upper = jnp.where(same & (si > ti), 1.0, 0.0).astype(BF16)
    w_terms = _bf16_terms(w_log, DECAY_SUM_TERMS)
    cum = sum(_dot(lower, t) for t in w_terms)
    rem = sum(_dot(upper, t) for t in w_terms)
    ci = lax.broadcasted_iota(jnp.int32, (SUBLANES, tm), 0)
    sj = lax.broadcasted_iota(jnp.int32, (SUBLANES, tm), 1)
    sel = jnp.where(ci == sj // CHUNK, 1.0, 0.0).astype(BF16)
    gam = jnp.exp(sum(_dot(sel, t) for t in w_terms))
    gam_ref[0, gi * chunks_per_group:(gi + 1) * chunks_per_group, :] = gam[:chunks_per_group]

    gate_ref[rows, d:2 * d] = _sigmoid(proj("gate", d, 2 * d)).astype(BF16)

    e_prev = jnp.exp(cum - w_log)
    e_cum = jnp.exp(cum)
    e_neg = jnp.exp(-cum)
    e_rem = jnp.exp(rem)
    b = kk * a
    at_ref[rows, :] = (-kk * e_prev).astype(BF16)
    rt_ref[rows, :] = (r * e_cum).astype(BF16)
    bt_ref[rows, :] = (b * e_neg).astype(BF16)
    kt_ref[rows, :] = (k2 * e_neg).astype(BF16)
    bh_ref[rows, :] = (b * e_rem).astype(BF16)
    kh_ref[rows, :] = (k2 * e_rem).astype(BF16)
    v_ref[rows, :] = v.astype(BF16)

    gate_ref[rows, 2 * d:3 * d] = _sigmoid(proj("gate", 2 * d, 3 * d)).astype(BF16)
    mq_ref[rows, :] = (proj("memq") * mem_scale).astype(BF16)

    ls = _log_sigmoid(proj("ff") + fb_ref[...])
    tri = jnp.where(ti >= si, 1.0, 0.0).astype(BF16)
    c = _dot_terms_lhs(tri, ls, F32_TERMS) + carry_ref[...]
    carry_ref[...] = c[tm - 1:tm, :]
    crow_ref[:, rows] = jnp.transpose(c * LOG2E)[:crow_ref.shape[0], :]


def _inproj(x2, g, w_all, cols, fbias, mu, w0, a0, k_k, k_a, r_k, wlora, gup, bd, *, seq, mem_scale,
            tm=INPROJ_ROWS):
    t, d = x2.shape
    w = w0.shape[1]
    assert seq % tm == 0 and tm % INPROJ_GROUP == 0 and INPROJ_GROUP % CHUNK == 0
    n_tiles = t // tm
    width = lambda name: cols[name][1] - cols[name][0]
    tok = lambda n, dt: jax.ShapeDtypeStruct((t, n), dt)
    n_sel = max(SUBLANES, tm // CHUNK)
    outs = ([tok(width("qkv"), BF16), jax.ShapeDtypeStruct((SUBLANES, t), F32), tok(width("memq"), BF16),
             tok(width("gate"), BF16)] + [tok(w, BF16)] * 7
            + [jax.ShapeDtypeStruct((n_tiles, n_sel, w), F32), tok(w, F32), tok(w, F32)])
    row = lambda n: pl.BlockSpec((tm, n), lambda i: (i, 0))
    consts = (g, w_all, fbias, mu, w0, a0, k_k, k_a, r_k, wlora, gup, bd)
    gam_index = 11
    results = list(pl.pallas_call(
        functools.partial(_inproj_kernel, tiles_per_seq=seq // tm, mem_scale=mem_scale, cols=cols),
        grid=(n_tiles,),
        in_specs=[row(d)] + [_const_spec(c.shape) for c in consts],
        out_specs=([row(width("qkv")), pl.BlockSpec((SUBLANES, tm), lambda i: (0, i)), row(width("memq")),
                    row(width("gate"))] + [row(w)] * 7
                   + [pl.BlockSpec((1, n_sel, w), lambda i: (i, 0, 0)), row(w), row(w)]),
        out_shape=outs,
        scratch_shapes=[pltpu.VMEM((1, LANES), F32), pltpu.VMEM((1, cols["rwkv"][1] - cols["rwkv"][0]), F32)],
        compiler_params=_params(1),
        name="inproj",
    )(x2, *consts))
    results[gam_index] = results[gam_index][:, :tm // CHUNK]
    return results


def _fox_kernel(q_ref, k_ref, v_ref, crow_ref, o_ref, *, bk, rb, ahead):
    seq = q_ref.shape[0]
    n_local = q_ref.shape[1] // LANES
    first_head = 2 * n_local * pl.program_id(1)
    first_q = lax.broadcasted_iota(jnp.int32, (rb, LANES), 1) < HEAD_DIM
    lower = (lax.broadcasted_iota(jnp.int32, (rb, rb), 0) >= lax.broadcasted_iota(jnp.int32, (rb, rb), 1))
    sub = lax.broadcasted_iota(jnp.int32, crow_ref.shape, 0)
    crow_all = crow_ref[...]
    crow = [jnp.sum(jnp.where(sub == first_head + h, crow_all, 0.0), axis=0, keepdims=True)
            for h in range(2 * n_local)]

    units = []
    for r in range(seq // rb):
        stop = (r + 1) * rb
        units += [(r, k0, min(bk, stop - k0), pp) for k0 in range(0, stop, bk) for pp in range(n_local)]
    units.sort(key=lambda u: (u[0] * rb // bk, u[1], u[0], u[3]))

    scores, state = {}, {}

    def issue_scores(u):
        r, k0, width, pp = u
        lanes = slice(pp * LANES, (pp + 1) * LANES)
        q2 = q_ref[r * rb:(r + 1) * rb, lanes]
        zero = jnp.zeros_like(q2)
        kt = k_ref[k0:k0 + width, lanes]
        scores[u, 0] = _dot_nt(jnp.where(first_q, q2, zero), kt)
        scores[u, 1] = _dot_nt(jnp.where(first_q, zero, q2), kt)

    def finish(u):
        r, k0, width, pp = u
        lanes = slice(pp * LANES, (pp + 1) * LANES)
        vt = v_ref[k0:k0 + width, lanes]
        one = jnp.ones_like(vt)
        first_k = lax.broadcasted_iota(jnp.int32, vt.shape, 1) < HEAD_DIM
        on_diagonal = k0 + width == (r + 1) * rb
        for e in range(2):
            v_aug = jnp.where(first_k, vt, one) if e == 0 else jnp.where(first_k, one, vt)
            s = scores.pop((u, e)) - crow[2 * pp + e][:, k0:k0 + width]
            if on_diagonal:
                tail = jnp.where(lower, s[:, width - rb:], NEG_BIG)
                s = tail if width == rb else jnp.concatenate([s[:, :width - rb], tail], axis=1)
            m_tile = jnp.max(s, axis=-1, keepdims=True)
            if k0 == 0:
                m_new = m_tile
                acc = _dot(jnp.exp2(s - m_new).astype(BF16), v_aug)
            else:
                m_old, acc_old = state[pp, r, e]
                m_new = jnp.maximum(m_old, m_tile)
                acc = jnp.exp2(m_old - m_new) * acc_old + _dot(jnp.exp2(s - m_new).astype(BF16), v_aug)
            state[pp, r, e] = (m_new, acc)
        if on_diagonal:
            a0, a1 = state.pop((pp, r, 0))[1], state.pop((pp, r, 1))[1]
            den = jnp.where(first_q, pltpu.roll(a0, HEAD_DIM, axis=1), pltpu.roll(a1, HEAD_DIM, axis=1))
            o_ref[r * rb:(r + 1) * rb, lanes] = (jnp.where(first_q, a0, a1) / den).astype(o_ref.dtype)

    for u in units[:ahead]:
        issue_scores(u)
    for n, u in enumerate(units):
        if n + ahead < len(units):
            issue_scores(units[n + ahead])
        finish(u)


def _fox(qkv, crow, *, batch, seq, bk=FOX_KEY_TILE, rb=FOX_ROW_BLOCK, ahead=FOX_SCORES_AHEAD,
         pairs=FOX_PAIRS_PER_STEP):
    t = qkv.shape[0]
    n_groups = qkv.shape[1] // (3 * LANES * pairs)
    width = pairs * LANES
    return pl.pallas_call(
        functools.partial(_fox_kernel, bk=bk, rb=rb, ahead=ahead),
        grid=(batch, n_groups),
        in_specs=[
            pl.BlockSpec((seq, width), lambda b, h: (b, h)),
            pl.BlockSpec((seq, width), lambda b, h: (b, n_groups + h)),
            pl.BlockSpec((seq, width), lambda b, h: (b, 2 * n_groups + h)),
            pl.BlockSpec((crow.shape[0], seq), lambda b, h: (0, b)),
        ],
        out_specs=pl.BlockSpec((seq, width), lambda b, h: (b, h)),
        out_shape=jax.ShapeDtypeStruct((t, n_groups * width), BF16),
        compiler_params=_params(2),
        name="fox_attention",
    )(qkv, qkv, qkv, crow)


def _memkv_kernel(m_ref, g_ref, w_ref, o_ref):
    o_ref[...] = _dot(_rms(m_ref[...], g_ref[...]).astype(BF16), w_ref[...]).astype(BF16)


def _memkv(mem2, g, w, tm=MEMKV_ROWS):
    t, d = mem2.shape
    assert t % tm == 0
    return pl.pallas_call(
        _memkv_kernel,
        grid=(t // tm,),
        in_specs=[pl.BlockSpec((tm, d), lambda i: (i, 0)), _const_spec(g.shape), _const_spec(w.shape)],
        out_specs=pl.BlockSpec((tm, w.shape[1]), lambda i: (i, 0)),
        out_shape=jax.ShapeDtypeStruct((t, w.shape[1]), BF16),
        compiler_params=_params(1),
        name="mem_kv",
    )(mem2, g, w)


def _rwkv_scan_kernel(at_ref, rt_ref, bt_ref, kt_ref, bh_ref, kh_ref, v_ref, gam_ref, bonus_ref, g_ref,
                      gng_ref, gnb_ref, bdavg_ref, o_ref, state_ref, y_ref):
    @pl.when(pl.program_id(1) == 0)
    def _():
        state_ref[...] = jnp.zeros_like(state_ref)

    n_t, w = v_ref.shape
    n_pairs = w // LANES
    n_chunks = n_t // CHUNK
    lane = lax.broadcasted_iota(jnp.int32, (CHUNK, LANES), 1)
    first = lane < HEAD_DIM
    ri = lax.broadcasted_iota(jnp.int32, (PAIR, PAIR), 0)
    ci = lax.broadcasted_iota(jnp.int32, (PAIR, PAIR), 1)
    strict = ri > ci
    eye = jnp.where(ri == ci, 1.0, 0.0)
    incl2 = (lax.broadcasted_iota(jnp.int32, (PAIR, 2 * PAIR), 0)
             >= (lax.broadcasted_iota(jnp.int32, (PAIR, 2 * PAIR), 1) & (PAIR - 1)))

    def stacked(ref, c, p):
        x = ref[c * CHUNK:(c + 1) * CHUNK, p * LANES:(p + 1) * LANES]
        z = jnp.zeros_like(x)
        return jnp.concatenate([jnp.where(first, x, z), jnp.where(first, z, x)], axis=0)

    pairs = range(n_pairs)
    states = [state_ref[p] for p in pairs]
    p_mat, d_mat, m_t, c_t = {}, {}, {}, {}

    def chain_step(c):
        h_bf = [states[p].astype(BF16) for p in pairs]
        for p in pairs:
            y_s = _dot_nt(p_mat.pop((c, p)), h_bf[p]) + d_mat.pop((c, p))
            gam = gam_ref[0, c:c + 1, p * LANES:(p + 1) * LANES]
            states[p] = states[p] * gam + _dot(h_bf[p], m_t.pop((c, p))) + c_t.pop((c, p))
            y_ref[c * CHUNK:(c + 1) * CHUNK, p * LANES:(p + 1) * LANES] = y_s[:CHUNK] + y_s[CHUNK:]

    def chunk_stages(cps):
        gram, s_inv, q_pow, a_ak, a_rbk, x_rhs, sx = {}, {}, {}, {}, {}, {}, {}
        for cp in cps:
            lhs = jnp.concatenate([stacked(at_ref, *cp), stacked(rt_ref, *cp)], axis=0)
            rhs = jnp.concatenate([stacked(bt_ref, *cp), stacked(kt_ref, *cp)], axis=0)
            gram[cp] = _dot_nt(lhs, rhs)
        yield
        for cp in cps:
            g_all = gram.pop(cp)
            n_ab = jnp.where(strict, g_all[:PAIR, :PAIR], 0.0)
            a_ak[cp] = jnp.where(strict, g_all[:PAIR, PAIR:], 0.0).astype(BF16)
            a_rbk[cp] = jnp.where(incl2, g_all[PAIR:, :], 0.0).astype(BF16)
            s_inv[cp] = eye + n_ab
            q_pow[cp] = n_ab.astype(BF16)
        for cp in cps:
            q_pow[cp] = _dot(q_pow[cp], q_pow[cp]).astype(BF16)
            x_rhs[cp] = _dot(a_ak.pop(cp), stacked(v_ref, *cp))
        yield
        span = 2
        while 2 * span < CHUNK:
            for cp in cps:
                both = _dot(q_pow[cp], jnp.concatenate([s_inv[cp].astype(BF16), q_pow[cp]], axis=1))
                s_inv[cp] = s_inv[cp] + both[:, :PAIR]
                q_pow[cp] = both[:, PAIR:].astype(BF16)
            span *= 2
            yield
        for cp in cps:
            s_inv[cp] = (s_inv[cp] + _dot(q_pow.pop(cp), s_inv[cp].astype(BF16))).astype(BF16)
        yield
        for cp in cps:
            x = jnp.concatenate([stacked(at_ref, *cp), x_rhs.pop(cp).astype(BF16)], axis=1)
            sx[cp] = _dot(s_inv.pop(cp), x).astype(BF16)
        yield
        for cp in cps:
            zero = jnp.zeros((PAIR, PAIR), BF16)
            lower_rows = jnp.concatenate([zero, stacked(v_ref, *cp)], axis=1)
            full = jnp.concatenate([sx.pop(cp), lower_rows], axis=0)
            pd = _dot(a_rbk.pop(cp), full)
            p_mat[cp] = (pd[:, :PAIR] + stacked(rt_ref, *cp).astype(F32)).astype(BF16)
            d_mat[cp] = pd[:, PAIR:]
            mc = _dot_tn(full, jnp.concatenate([stacked(bh_ref, *cp), stacked(kh_ref, *cp)], axis=0))
            m_t[cp] = mc[:PAIR].astype(BF16)
            c_t[cp] = mc[PAIR:]
        yield

    group = min(n_chunks, SCAN_GROUP_CHUNKS)
    pending = []
    for c0 in range(0, n_chunks, group):
        chunks = range(c0, min(c0 + group, n_chunks))
        for _ in chunk_stages([(c, p) for c in chunks for p in pairs]):
            if pending:
                chain_step(pending.pop(0))
        pending += list(chunks)
    for c in pending:
        chain_step(c)
    for p in pairs:
        state_ref[p] = states[p]

    y = y_ref[...]
    bdavg = bdavg_ref[...]
    dev = y - _dot_terms_rhs(y, bdavg, HEAD_SUM_TERMS + 1)
    var = _dot_terms_rhs(dev * dev, bdavg, HEAD_SUM_TERMS)
    yn = dev * lax.rsqrt(var + GN_EPS) * gng_ref[...] + gnb_ref[...]
    o_ref[...] = ((yn + bonus_ref[...]) * g_ref[...]).astype(o_ref.dtype)


def _rwkv_scan(at, rt, bt, kt, bh, kh, v, gam, bonus, g, gn_g, gn_b, bdavg, *, batch, seq, tl=SCAN_ROWS):
    t, w = v.shape
    assert seq % tl == 0 and tl % CHUNK == 0
    n_sel = tl // CHUNK
    gam = gam.reshape(t // tl, n_sel, w)
    nt = seq // tl
    tok = pl.BlockSpec((tl, w), lambda b, i: (b * nt + i, 0))
    return pl.pallas_call(
        _rwkv_scan_kernel,
        grid=(batch, nt),
        in_specs=[tok] * 7 + [pl.BlockSpec((1, n_sel, w), lambda b, i: (b * nt + i, 0, 0)), tok, tok,
                              _const_spec(gn_g.shape), _const_spec(gn_b.shape), _const_spec(bdavg.shape)],
        out_specs=tok,
        out_shape=jax.ShapeDtypeStruct((t, w), BF16),
        scratch_shapes=[pltpu.VMEM((w // LANES, LANES, LANES), F32), pltpu.VMEM((tl, w), F32)],
        compiler_params=_params(2),
        name="rwkv_scan",
    )(at, rt, bt, kt, bh, kh, v, gam, bonus, g, gn_g, gn_b, bdavg)


def _merge_kernel(x_ref, fox_ref, rw_ref, mq_ref, kv_ref, gate_ref, wf_ref, wr_ref, wm_ref, wo_ref, g_ref, o_ref):
    d = x_ref.shape[1]
    w = mq_ref.shape[1]
    hd = w // MEM_HEADS
    heads = range(MEM_HEADS)
    scores = [_dot_nt(mq_ref[:, h * hd:(h + 1) * hd], kv_ref[:, h * hd:(h + 1) * hd]) for h in heads]
    merged = (gate_ref[:, 0:d].astype(F32) * _dot(fox_ref[...], wf_ref[...])
              + gate_ref[:, d:2 * d].astype(F32) * _dot(rw_ref[...], wr_ref[...]))
    probs, norms = [], []
    for h in heads:
        p = jnp.exp(scores[h] - jnp.max(scores[h], axis=-1, keepdims=True))
        norms.append(jnp.sum(p, axis=-1, keepdims=True))
        probs.append(p.astype(BF16))
    mem_out = jnp.concatenate(
        [(_dot(probs[h], kv_ref[:, w + h * hd:w + (h + 1) * hd]) / norms[h]).astype(BF16) for h in heads], axis=1)
    merged = merged + gate_ref[:, 2 * d:3 * d].astype(F32) * _dot(mem_out, wm_ref[...])
    y = _dot(merged.astype(BF16), wo_ref[...])
    o_ref[...] = x_ref[...] + _rms(y, g_ref[...])


def _merge(x2, fox, rw, mq, memkv, gates, wf, wr, wm, wo, g, *, seq, mem_len, tm=MERGE_ROWS):
    t, d = x2.shape
    assert seq % tm == 0
    tiles_per_seq = seq // tm
    row = lambda width: pl.BlockSpec((tm, width), lambda i: (i, 0))
    return pl.pallas_call(
        _merge_kernel,
        grid=(t // tm,),
        in_specs=[row(d), row(fox.shape[1]), row(rw.shape[1]), row(mq.shape[1]),
                  pl.BlockSpec((mem_len, memkv.shape[1]), lambda i: (i // tiles_per_seq, 0)), row(gates.shape[1]),
                  _const_spec(wf.shape), _const_spec(wr.shape), _const_spec(wm.shape), _const_spec(wo.shape),
                  _const_spec(g.shape)],
        out_specs=row(d),
        out_shape=jax.ShapeDtypeStruct((t, d), F32),
        compiler_params=_params(1),
        name="merge_out",
    )(x2, fox, rw, mq, memkv, gates, wf, wr, wm, wo, g)


def _ffn_kernel(h_ref, g1_ref, wg_ref, wu_ref, wd_ref, g2_ref, o_ref, *, ff_chunk):
    h = h_ref[...]
    u = _rms(h, g1_ref[...]).astype(BF16)
    d_ff = wg_ref.shape[1]
    acc = jnp.zeros(h.shape, F32)
    for lo in range(0, d_ff, ff_chunk):
        hi = min(lo + ff_chunk, d_ff)
        gt = _dot(u, wg_ref[:, lo:hi])
        up = _dot(u, wu_ref[:, lo:hi])
        act = (gt * _sigmoid(gt) * up).astype(BF16)
        acc = acc + _dot(act, wd_ref[lo:hi, :])
    o_ref[...] = h + _rms(acc, g2_ref[...])


def _ffn(h2, g1, wg, wu, wd, g2, tm=FFN_ROWS, ff_chunk=FFN_CHUNK):
    t, d = h2.shape
    assert t % tm == 0
    row = pl.BlockSpec((tm, d), lambda i: (i, 0))
    return pl.pallas_call(
        functools.partial(_ffn_kernel, ff_chunk=ff_chunk),
        grid=(t // tm,),
        in_specs=[row, _const_spec(g1.shape), _const_spec(wg.shape), _const_spec(wu.shape),
                  _const_spec(wd.shape), _const_spec(g2.shape)],
        out_specs=row,
        out_shape=jax.ShapeDtypeStruct((t, d), F32),
        compiler_params=_params(1),
        name="ffn",
    )(h2, g1, wg, wu, wd, g2)


def _block_diag_ones(width, block, value):
    idx = jnp.arange(width) // block
    return jnp.where(idx[:, None] == idx[None, :], value, 0.0).astype(BF16)


def kernel(x, mem, pre1_g, post1_g, pre2_g, post2_g, mem_norm_g, w_in, fox_f_bias, rwkv_mu, rwkv_w0,
           rwkv_w_up, rwkv_a0, rwkv_a_up, rwkv_g_up, rwkv_k_k, rwkv_k_a, rwkv_r_k, rwkv_gn_g, rwkv_gn_b,
           w_mem_kv, w_fox_out, w_rwkv_out, w_mem_out, w_o, w_ffn_gate, w_ffn_up, w_ffn_down):
    batch, seq, d = x.shape
    mem_len = mem.shape[1]
    depth = w_in.shape[0]
    fox_heads = fox_f_bias.shape[1]
    fox_w = fox_heads * HEAD_DIM
    rw_w = rwkv_w0.shape[1]
    rw_cols = rwkv_mu.shape[1]
    mem_w = w_mem_kv.shape[2] // 2
    dec_lora = rwkv_w_up.shape[1]
    aaa_lora = rwkv_a_up.shape[1]
    assert dec_lora + aaa_lora == LANES and dec_lora == aaa_lora
    assert fox_heads <= SUBLANES and fox_w % LANES == 0 and rw_w % LANES == 0
    row2 = lambda v: v.reshape(1, -1)

    h = x.reshape(batch * seq, d)
    mem2 = mem.reshape(batch * mem_len, d)
    bd_ones = _block_diag_ones(rw_w, HEAD_DIM, 1.0)
    bd_avg = _block_diag_ones(rw_w, HEAD_DIM, 1.0 / HEAD_DIM)
    for l in range(depth):
        wi = w_in[l]
        o_ff = 3 * fox_w
        o_rw = o_ff + fox_heads
        o_mq = o_rw + rw_cols
        o_gt = o_mq + mem_w
        scale = HEAD_DIM ** -0.5 * LOG2E
        sections = [("qkv", jnp.concatenate([wi[:, :fox_w] * scale, wi[:, fox_w:o_ff]], axis=1)),
                    ("ff", jnp.pad(wi[:, o_ff:o_rw], ((0, 0), (0, LANES - fox_heads)))),
                    ("rwkv", wi[:, o_rw:o_mq]), ("memq", wi[:, o_mq:o_gt]), ("gate", wi[:, o_gt:])]
        w_all = jnp.concatenate([sec for _, sec in sections], axis=1).astype(BF16)
        cols, start = {}, 0
        for name, sec in sections:
            assert sec.shape[1] % LANES == 0
            cols[name] = (start, start + sec.shape[1])
            start += sec.shape[1]
        fbias = jnp.pad(row2(fox_f_bias[l]), ((0, 0), (0, LANES - fox_heads)))
        wlora = jnp.zeros((LANES, 2 * rw_w), F32)
        wlora = wlora.at[:dec_lora, :rw_w].set(rwkv_w_up[l]).at[dec_lora:, rw_w:].set(rwkv_a_up[l]).astype(BF16)

        qkv, crow, mq, gates, *scan_in = _inproj(
            h, row2(pre1_g[l]), w_all, cols, fbias, row2(rwkv_mu[l]), row2(rwkv_w0[l]), row2(rwkv_a0[l]),
            row2(rwkv_k_k[l]), row2(rwkv_k_a[l]), row2(rwkv_r_k[l]), wlora, rwkv_g_up[l].astype(BF16), bd_ones,
            seq=seq, mem_scale=(mem_w // MEM_HEADS) ** -0.5)

        fox_out = _fox(qkv, crow, batch=batch, seq=seq)

        memkv = _memkv(mem2, row2(mem_norm_g[l]), w_mem_kv[l].astype(BF16))
        rwkv_out = _rwkv_scan(*scan_in, row2(rwkv_gn_g[l]), row2(rwkv_gn_b[l]), bd_avg, batch=batch, seq=seq)

        h = _merge(h, fox_out, rwkv_out, mq, memkv, gates, w_fox_out[l].astype(BF16),
                   w_rwkv_out[l].astype(BF16), w_mem_out[l].astype(BF16), w_o[l].astype(BF16),
                   row2(post1_g[l]), seq=seq, mem_len=mem_len)
        h = _ffn(h, row2(pre2_g[l]), w_ffn_gate[l].astype(BF16), w_ffn_up[l].astype(BF16),
                 w_ffn_down[l].astype(BF16), row2(post2_g[l]))
    return h.reshape(batch, seq, d).astype(x.dtype)
```

```python
import functools

import jax
import jax.numpy as jnp
from jax import lax
from jax.experimental import pallas as pl
from jax.experimental.pallas import tpu as pltpu

F32 = jnp.float32
BF16 = jnp.bfloat16

NORM_EPS = 1e-6
GN_EPS = 64e-5
HEAD_DIM = 64
LANES = 128
SUBLANES = 8
LOG2E = 1.4426950408889634
MEM_HEADS = 4
CHUNK = 64
PAIR = 2 * CHUNK
F32_TERMS = 3
DECAY_SUM_TERMS = 2
HEAD_SUM_TERMS = 1
NEG_BIG = -1e30
VMEM_LIMIT = 56 * 1024 * 1024

INPROJ_GROUP = 256
INPROJ_ROWS = 512
SCAN_ROWS = 1024
SCAN_GROUP_CHUNKS = 4
FOX_KEY_TILE = 512
FOX_ROW_BLOCK = 128
FOX_SCORES_AHEAD = 2
FOX_PAIRS_PER_STEP = 2
MEMKV_ROWS = 512
MERGE_ROWS = 1024
FFN_ROWS = 1024
FFN_CHUNK = 1024


def _dot(a, b):
    return jnp.dot(a, b, preferred_element_type=F32)


def _dot_nt(a, b):
    return lax.dot_general(a, b, (((1,), (1,)), ((), ())), preferred_element_type=F32)


def _dot_tn(a, b):
    return lax.dot_general(a, b, (((0,), (0,)), ((), ())), preferred_element_type=F32)


def _bf16_terms(x, n):
    terms = []
    for _ in range(n - 1):
        t = x.astype(BF16)
        terms.append(t)
        x = x - t.astype(F32)
    return terms + [x.astype(BF16)]


def _dot_terms_lhs(m_bf16, x, n):
    return sum(_dot(m_bf16, t) for t in _bf16_terms(x, n))


def _dot_terms_rhs(x, m_bf16, n):
    return sum(_dot(t, m_bf16) for t in _bf16_terms(x, n))


def _rms(xf, g):
    return xf * lax.rsqrt(jnp.mean(xf * xf, axis=-1, keepdims=True) + NORM_EPS) * g


def _sigmoid(x):
    return 1.0 / (1.0 + jnp.exp(-x))


def _log_sigmoid(x):
    return jnp.minimum(x, 0.0) - jnp.log(1.0 + jnp.exp(-jnp.abs(x)))


def _const_spec(shape):
    nd = len(shape)
    return pl.BlockSpec(shape, lambda *_: (0,) * nd, pipeline_mode=pl.Buffered(1))


def _params(n_axes):
    return pltpu.CompilerParams(dimension_semantics=("arbitrary",) * n_axes,
                                vmem_limit_bytes=VMEM_LIMIT)


def _inproj_kernel(x_ref, g_ref, w_ref, fb_ref, mu_ref, w0_ref, a0_ref, kk_ref, ka_ref, rk_ref,
                   wlora_ref, gup_ref, bd_ref,
                   qkv_ref, crow_ref, mq_ref, gate_ref,
                   at_ref, rt_ref, bt_ref, kt_ref, bh_ref, kh_ref, v_ref, gam_ref, bonus_ref, g_out_ref,
                   carry_ref, prev_ref, *, tiles_per_seq, mem_scale, cols):
    i = pl.program_id(0)
    w = v_ref.shape[1]

    @pl.when(i % tiles_per_seq == 0)
    def _():
        carry_ref[...] = jnp.zeros_like(carry_ref)
        prev_ref[...] = jnp.zeros_like(prev_ref)

    n_groups = x_ref.shape[0] // INPROJ_GROUP
    programs = [
        _inproj_rows(slice(gi * INPROJ_GROUP, (gi + 1) * INPROJ_GROUP), gi, x_ref, g_ref, w_ref, fb_ref, mu_ref,
                     w0_ref, a0_ref, kk_ref, ka_ref, rk_ref, wlora_ref, gup_ref, bd_ref, qkv_ref, crow_ref,
                     mq_ref, gate_ref, at_ref, rt_ref, bt_ref, kt_ref, bh_ref, kh_ref, v_ref, gam_ref, bonus_ref,
                     g_out_ref, carry_ref, prev_ref, w=w, mem_scale=mem_scale, cols=cols)
        for gi in range(n_groups)]
    while programs:
        for prog in list(programs):
            if next(prog, "done") == "done":
                programs.remove(prog)


def _inproj_rows(rows, gi, x_ref, g_ref, w_ref, fb_ref, mu_ref, w0_ref, a0_ref, kk_ref, ka_ref, rk_ref,
                 wlora_ref, gup_ref, bd_ref, qkv_ref, crow_ref, mq_ref, gate_ref,
                 at_ref, rt_ref, bt_ref, kt_ref, bh_ref, kh_ref, v_ref, gam_ref, bonus_ref, g_out_ref,
                 carry_ref, prev_ref, *, w, mem_scale, cols):
    tm = INPROJ_GROUP
    chunks_per_group = tm // CHUNK

    def proj(name, lo=0, hi=None):
        start, stop = cols[name]
        return _dot(u, w_ref[:, start + lo:(stop if hi is None else start + hi)])

    u = _rms(x_ref[rows, :], g_ref[...]).astype(BF16)

    p = proj("rwkv")
    rowid = lax.broadcasted_iota(jnp.int32, p.shape, 0)
    shifted = jnp.where(rowid == 0, prev_ref[...], pltpu.roll(p, 1, axis=0))
    prev_ref[...] = p[tm - 1:tm, :]
    xs = p + (shifted - p) * mu_ref[...]
    yield
    r = xs[:, 0:w]
    k = xs[:, w:2 * w]
    v = xs[:, 2 * w:3 * w]
    lora_in = xs[:, 3 * w:3 * w + LANES]
    gd = xs[:, 3 * w + LANES:]
    lane = lax.broadcasted_iota(jnp.int32, lora_in.shape, 1)
    lora_in = jnp.where(lane < lora_in.shape[1] // 2, jnp.tanh(lora_in), lora_in)
    qkv_ref[rows, :] = proj("qkv").astype(BF16)
    yield
    lora = _dot(lora_in.astype(BF16), wlora_ref[...])
    g_out_ref[rows, :] = _dot(_sigmoid(gd).astype(BF16), gup_ref[...])
    w_log = -jnp.exp(_log_sigmoid(w0_ref[...] + lora[:, :w]) - 0.5)
    a = _sigmoid(a0_ref[...] + lora[:, w:])
    yield

    d = gate_ref.shape[1] // 3
    gate_ref[rows, 0:d] = _sigmoid(proj("gate", 0, d)).astype(BF16)
    yield

    bd = bd_ref[...]
    kk = k * kk_ref[...]
    kk = kk * lax.rsqrt(jnp.maximum(_dot_terms_rhs(kk * kk, bd, HEAD_SUM_TERMS), 1e-24))
    k2 = k * (1.0 + (a - 1.0) * ka_ref[...])
    bonus_ref[rows, :] = _dot_terms_rhs(r * k2 * rk_ref[...], bd, HEAD_SUM_TERMS) * v
    yield

    ti = lax.broadcasted_iota(jnp.int32, (tm, tm), 0)
    si = lax.broadcasted_iota(jnp.int32, (tm, tm), 1)
    same = (ti // CHUNK) == (si // CHUNK)
    lower = jnp.where(same & (si <= ti), 1.0, 0.0).astype(BF16)
    upper = jnp.where(same & (si > ti), 1.0, 0.0).astype(BF16)
    w_terms = _bf16_terms(w_log, DECAY_SUM_TERMS)
    cum = sum(_dot(lower, t) for t in w_terms)
    rem = sum(_dot(upper, t) for t in w_terms)
    ci = lax.broadcasted_iota(jnp.int32, (SUBLANES, tm), 0)
    sj = lax.broadcasted_iota(jnp.int32, (SUBLANES, tm), 1)
    sel = jnp.where(ci == sj // CHUNK, 1.0, 0.0).astype(BF16)
    gam = jnp.exp(sum(_dot(sel, t) for t in w_terms))
    gam_ref[0, gi * chunks_per_group:(gi + 1) * chunks_per_group, :] = gam[:chunks_per_group]
    yield

    gate_ref[rows, d:2 * d] = _sigmoid(proj("gate", d, 2 * d)).astype(BF16)
    yield

    e_prev = jnp.exp(cum - w_log)
    e_cum = jnp.exp(cum)
    e_neg = jnp.exp(-cum)
    e_rem = jnp.exp(rem)
    b = kk * a
    at_ref[rows, :] = (-kk * e_prev).astype(BF16)
    rt_ref[rows, :] = (r * e_cum).astype(BF16)
    bt_ref[rows, :] = (b * e_neg).astype(BF16)
    kt_ref[rows, :] = (k2 * e_neg).astype(BF16)
    bh_ref[rows, :] = (b * e_rem).astype(BF16)
    kh_ref[rows, :] = (k2 * e_rem).astype(BF16)
    v_ref[rows, :] = v.astype(BF16)
    yield

    gate_ref[rows, 2 * d:3 * d] = _sigmoid(proj("gate", 2 * d, 3 * d)).astype(BF16)
    yield
    mq_ref[rows, :] = (proj("memq") * mem_scale).astype(BF16)
    yield

    ls = _log_sigmoid(proj("ff") + fb_ref[...])
    tri = jnp.where(ti >= si, 1.0, 0.0).astype(BF16)
    c = _dot_terms_lhs(tri, ls, F32_TERMS) + carry_ref[...]
    carry_ref[...] = c[tm - 1:tm, :]
    crow_ref[:, rows] = jnp.transpose(c * LOG2E)[:crow_ref.shape[0], :]


def _inproj(x2, g, w_all, cols, fbias, mu, w0, a0, k_k, k_a, r_k, wlora, gup, bd, *, seq, mem_scale,
            tm=INPROJ_ROWS):
    t, d = x2.shape
    w = w0.shape[1]
    assert seq % tm == 0 and tm % INPROJ_GROUP == 0 and INPROJ_GROUP % CHUNK == 0
    n_tiles = t // tm
    width = lambda name: cols[name][1] - cols[name][0]
    tok = lambda n, dt: jax.ShapeDtypeStruct((t, n), dt)
    n_sel = max(SUBLANES, tm // CHUNK)
    outs = ([tok(width("qkv"), BF16), jax.ShapeDtypeStruct((SUBLANES, t), F32), tok(width("memq"), BF16),
             tok(width("gate"), BF16)] + [tok(w, BF16)] * 7
            + [jax.ShapeDtypeStruct((n_tiles, n_sel, w), F32), tok(w, F32), tok(w, F32)])
    row = lambda n: pl.BlockSpec((tm, n), lambda i: (i, 0))
    consts = (g, w_all, fbias, mu, w0, a0, k_k, k_a, r_k, wlora, gup, bd)
    gam_index = 11
    results = list(pl.pallas_call(
        functools.partial(_inproj_kernel, tiles_per_seq=seq // tm, mem_scale=mem_scale, cols=cols),
        grid=(n_tiles,),
        in_specs=[row(d)] + [_const_spec(c.shape) for c in consts],
        out_specs=([row(width("qkv")), pl.BlockSpec((SUBLANES, tm), lambda i: (0, i)), row(width("memq")),
                    row(width("gate"))] + [row(w)] * 7
                   + [pl.BlockSpec((1, n_sel, w), lambda i: (i, 0, 0)), row(w), row(w)]),
        out_shape=outs,
        scratch_shapes=[pltpu.VMEM((1, LANES), F32), pltpu.VMEM((1, cols["rwkv"][1] - cols["rwkv"][0]), F32)],
        compiler_params=_params(1),
        name="inproj",
    )(x2, *consts))
    results[gam_index] = results[gam_index][:, :tm // CHUNK]
    return results


def _fox_kernel(q_ref, k_ref, v_ref, crow_ref, o_ref, *, bk, rb, ahead):
    seq = q_ref.shape[0]
    n_local = q_ref.shape[1] // LANES
    first_head = 2 * n_local * pl.program_id(1)
    first_q = lax.broadcasted_iota(jnp.int32, (rb, LANES), 1) < HEAD_DIM
    lower = (lax.broadcasted_iota(jnp.int32, (rb, rb), 0) >= lax.broadcasted_iota(jnp.int32, (rb, rb), 1))
    sub = lax.broadcasted_iota(jnp.int32, crow_ref.shape, 0)
    crow_all = crow_ref[...]
    crow = [jnp.sum(jnp.where(sub == first_head + h, crow_all, 0.0), axis=0, keepdims=True)
            for h in range(2 * n_local)]

    units = []
    for r in range(seq // rb):
        stop = (r + 1) * rb
        units += [(r, k0, min(bk, stop - k0), pp) for k0 in range(0, stop, bk) for pp in range(n_local)]
    units.sort(key=lambda u: (u[0] * rb // bk, u[1], u[0], u[3]))

    scores, state = {}, {}

    def issue_scores(u):
        r, k0, width, pp = u
        lanes = slice(pp * LANES, (pp + 1) * LANES)
        q2 = q_ref[r * rb:(r + 1) * rb, lanes]
        zero = jnp.zeros_like(q2)
        kt = k_ref[k0:k0 + width, lanes]
        scores[u, 0] = _dot_nt(jnp.where(first_q, q2, zero), kt)
        scores[u, 1] = _dot_nt(jnp.where(first_q, zero, q2), kt)

    def finish(u):
        r, k0, width, pp = u
        lanes = slice(pp * LANES, (pp + 1) * LANES)
        vt = v_ref[k0:k0 + width, lanes]
        one = jnp.ones_like(vt)
        first_k = lax.broadcasted_iota(jnp.int32, vt.shape, 1) < HEAD_DIM
        on_diagonal = k0 + width == (r + 1) * rb
        for e in range(2):
            v_aug = jnp.where(first_k, vt, one) if e == 0 else jnp.where(first_k, one, vt)
            s = scores.pop((u, e)) - crow[2 * pp + e][:, k0:k0 + width]
            if on_diagonal:
                tail = jnp.where(lower, s[:, width - rb:], NEG_BIG)
                s = tail if width == rb else jnp.concatenate([s[:, :width - rb], tail], axis=1)
            m_tile = jnp.max(s, axis=-1, keepdims=True)
            if k0 == 0:
                m_new = m_tile
                acc = _dot(jnp.exp2(s - m_new).astype(BF16), v_aug)
            else:
                m_old, acc_old = state[pp, r, e]
                m_new = jnp.maximum(m_old, m_tile)
                acc = jnp.exp2(m_old - m_new) * acc_old + _dot(jnp.exp2(s - m_new).astype(BF16), v_aug)
            state[pp, r, e] = (m_new, acc)
        if on_diagonal:
            a0, a1 = state.pop((pp, r, 0))[1], state.pop((pp, r, 1))[1]
            den = jnp.where(first_q, pltpu.roll(a0, HEAD_DIM, axis=1), pltpu.roll(a1, HEAD_DIM, axis=1))
            o_ref[r * rb:(r + 1) * rb, lanes] = (jnp.where(first_q, a0, a1) / den).astype(o_ref.dtype)

    for u in units[:ahead]:
        issue_scores(u)
    for n, u in enumerate(units):
        if n + ahead < len(units):
            issue_scores(units[n + ahead])
        finish(u)


def _fox(qkv, crow, *, batch, seq, bk=FOX_KEY_TILE, rb=FOX_ROW_BLOCK, ahead=FOX_SCORES_AHEAD,
         pairs=FOX_PAIRS_PER_STEP):
    t = qkv.shape[0]
    n_groups = qkv.shape[1] // (3 * LANES * pairs)
    width = pairs * LANES
    return pl.pallas_call(
        functools.partial(_fox_kernel, bk=bk, rb=rb, ahead=ahead),
        grid=(batch, n_groups),
        in_specs=[
            pl.BlockSpec((seq, width), lambda b, h: (b, h)),
            pl.BlockSpec((seq, width), lambda b, h: (b, n_groups + h)),
            pl.BlockSpec((seq, width), lambda b, h: (b, 2 * n_groups + h)),
            pl.BlockSpec((crow.shape[0], seq), lambda b, h: (0, b)),
        ],
        out_specs=pl.BlockSpec((seq, width), lambda b, h: (b, h)),
        out_shape=jax.ShapeDtypeStruct((t, n_groups * width), BF16),
        compiler_params=_params(2),
        name="fox_attention",
    )(qkv, qkv, qkv, crow)


def _memkv_kernel(m_ref, g_ref, w_ref, o_ref):
    o_ref[...] = _dot(_rms(m_ref[...], g_ref[...]).astype(BF16), w_ref[...]).astype(BF16)


def _memkv(mem2, g, w, tm=MEMKV_ROWS):
    t, d = mem2.shape
    assert t % tm == 0
    return pl.pallas_call(
        _memkv_kernel,
        grid=(t // tm,),
        in_specs=[pl.BlockSpec((tm, d), lambda i: (i, 0)), _const_spec(g.shape), _const_spec(w.shape)],
        out_specs=pl.BlockSpec((tm, w.shape[1]), lambda i: (i, 0)),
        out_shape=jax.ShapeDtypeStruct((t, w.shape[1]), BF16),
        compiler_params=_params(1),
        name="mem_kv",
    )(mem2, g, w)


def _rwkv_scan_kernel(at_ref, rt_ref, bt_ref, kt_ref, bh_ref, kh_ref, v_ref, gam_ref, bonus_ref, g_ref,
                      gng_ref, gnb_ref, bdavg_ref, o_ref, state_ref, y_ref):
    @pl.when(pl.program_id(1) == 0)
    def _():
        state_ref[...] = jnp.zeros_like(state_ref)

    n_t, w = v_ref.shape
    n_pairs = w // LANES
    n_chunks = n_t // CHUNK
    lane = lax.broadcasted_iota(jnp.int32, (CHUNK, LANES), 1)
    first = lane < HEAD_DIM
    ri = lax.broadcasted_iota(jnp.int32, (PAIR, PAIR), 0)
    ci = lax.broadcasted_iota(jnp.int32, (PAIR, PAIR), 1)
    strict = ri > ci
    eye = jnp.where(ri == ci, 1.0, 0.0)
    incl2 = (lax.broadcasted_iota(jnp.int32, (PAIR, 2 * PAIR), 0)
             >= (lax.broadcasted_iota(jnp.int32, (PAIR, 2 * PAIR), 1) & (PAIR - 1)))

    def stacked(ref, c, p):
        x = ref[c * CHUNK:(c + 1) * CHUNK, p * LANES:(p + 1) * LANES]
        z = jnp.zeros_like(x)
        return jnp.concatenate([jnp.where(first, x, z), jnp.where(first, z, x)], axis=0)

    pairs = range(n_pairs)
    states = [state_ref[p] for p in pairs]
    p_mat, d_mat, m_t, c_t = {}, {}, {}, {}

    def chain_step(c):
        h_bf = [states[p].astype(BF16) for p in pairs]
        for p in pairs:
            y_s = _dot_nt(p_mat.pop((c, p)), h_bf[p]) + d_mat.pop((c, p))
            gam = gam_ref[0, c:c + 1, p * LANES:(p + 1) * LANES]
            states[p] = states[p] * gam + _dot(h_bf[p], m_t.pop((c, p))) + c_t.pop((c, p))
            y_ref[c * CHUNK:(c + 1) * CHUNK, p * LANES:(p + 1) * LANES] = y_s[:CHUNK] + y_s[CHUNK:]

    def chunk_stages(cps):
        gram, s_inv, q_pow, a_ak, a_rbk, x_rhs, sx = {}, {}, {}, {}, {}, {}, {}
        for cp in cps:
            lhs = jnp.concatenate([stacked(at_ref, *cp), stacked(rt_ref, *cp)], axis=0)
            rhs = jnp.concatenate([stacked(bt_ref, *cp), stacked(kt_ref, *cp)], axis=0)
            gram[cp] = _dot_nt(lhs, rhs)
        yield
        for cp in cps:
            g_all = gram.pop(cp)
            n_ab = jnp.where(strict, g_all[:PAIR, :PAIR], 0.0)
            a_ak[cp] = jnp.where(strict, g_all[:PAIR, PAIR:], 0.0).astype(BF16)
            a_rbk[cp] = jnp.where(incl2, g_all[PAIR:, :], 0.0).astype(BF16)
            s_inv[cp] = eye + n_ab
            q_pow[cp] = n_ab.astype(BF16)
        for cp in cps:
            q_pow[cp] = _dot(q_pow[cp], q_pow[cp]).astype(BF16)
            x_rhs[cp] = _dot(a_ak.pop(cp), stacked(v_ref, *cp))
        yield
        span = 2
        while 2 * span < CHUNK:
            for cp in cps:
                both = _dot(q_pow[cp], jnp.concatenate([s_inv[cp].astype(BF16), q_pow[cp]], axis=1))
                s_inv[cp] = s_inv[cp] + both[:, :PAIR]
                q_pow[cp] = both[:, PAIR:].astype(BF16)
            span *= 2
            yield
        for cp in cps:
            s_inv[cp] = (s_inv[cp] + _dot(q_pow.pop(cp), s_inv[cp].astype(BF16))).astype(BF16)
        yield
        for cp in cps:
            x = jnp.concatenate([stacked(at_ref, *cp), x_rhs.pop(cp).astype(BF16)], axis=1)
            sx[cp] = _dot(s_inv.pop(cp), x).astype(BF16)
        yield
        for cp in cps:
            zero = jnp.zeros((PAIR, PAIR), BF16)
            lower_rows = jnp.concatenate([zero, stacked(v_ref, *cp)], axis=1)
            full = jnp.concatenate([sx.pop(cp), lower_rows], axis=0)
            pd = _dot(a_rbk.pop(cp), full)
            p_mat[cp] = (pd[:, :PAIR] + stacked(rt_ref, *cp).astype(F32)).astype(BF16)
            d_mat[cp] = pd[:, PAIR:]
            mc = _dot_tn(full, jnp.concatenate([stacked(bh_ref, *cp), stacked(kh_ref, *cp)], axis=0))
            m_t[cp] = mc[:PAIR].astype(BF16)
            c_t[cp] = mc[PAIR:]
        yield

    group = min(n_chunks, SCAN_GROUP_CHUNKS)
    pending = []
    for c0 in range(0, n_chunks, group):
        chunks = range(c0, min(c0 + group, n_chunks))
        for _ in chunk_stages([(c, p) for c in chunks for p in pairs]):
            if pending:
                chain_step(pending.pop(0))
        pending += list(chunks)
    for c in pending:
        chain_step(c)
    for p in pairs:
        state_ref[p] = states[p]

    y = y_ref[...]
    bdavg = bdavg_ref[...]
    dev = y - _dot_terms_rhs(y, bdavg, HEAD_SUM_TERMS + 1)
    var = _dot_terms_rhs(dev * dev, bdavg, HEAD_SUM_TERMS)
    yn = dev * lax.rsqrt(var + GN_EPS) * gng_ref[...] + gnb_ref[...]
    o_ref[...] = ((yn + bonus_ref[...]) * g_ref[...]).astype(o_ref.dtype)


def _rwkv_scan(at, rt, bt, kt, bh, kh, v, gam, bonus, g, gn_g, gn_b, bdavg, *, batch, seq, tl=SCAN_ROWS):
    t, w = v.shape
    assert seq % tl == 0 and tl % CHUNK == 0
    n_sel = tl // CHUNK
    gam = gam.reshape(t // tl, n_sel, w)
    nt = seq // tl
    tok = pl.BlockSpec((tl, w), lambda b, i: (b * nt + i, 0))
    return pl.pallas_call(
        _rwkv_scan_kernel,
        grid=(batch, nt),
        in_specs=[tok] * 7 + [pl.BlockSpec((1, n_sel, w), lambda b, i: (b * nt + i, 0, 0)), tok, tok,
                              _const_spec(gn_g.shape), _const_spec(gn_b.shape), _const_spec(bdavg.shape)],
        out_specs=tok,
        out_shape=jax.ShapeDtypeStruct((t, w), BF16),
        scratch_shapes=[pltpu.VMEM((w // LANES, LANES, LANES), F32), pltpu.VMEM((tl, w), F32)],
        compiler_params=_params(2),
        name="rwkv_scan",
    )(at, rt, bt, kt, bh, kh, v, gam, bonus, g, gn_g, gn_b, bdavg)


def _merge_kernel(x_ref, fox_ref, rw_ref, mq_ref, kv_ref, gate_ref, wf_ref, wr_ref, wm_ref, wo_ref, g_ref, o_ref):
    d = x_ref.shape[1]
    w = mq_ref.shape[1]
    hd = w // MEM_HEADS
    heads = range(MEM_HEADS)
    scores = [_dot_nt(mq_ref[:, h * hd:(h + 1) * hd], kv_ref[:, h * hd:(h + 1) * hd]) for h in heads]
    merged = (gate_ref[:, 0:d].astype(F32) * _dot(fox_ref[...], wf_ref[...])
              + gate_ref[:, d:2 * d].astype(F32) * _dot(rw_ref[...], wr_ref[...]))
    probs, norms = [], []
    for h in heads:
        p = jnp.exp(scores[h] - jnp.max(scores[h], axis=-1, keepdims=True))
        norms.append(jnp.sum(p, axis=-1, keepdims=True))
        probs.append(p.astype(BF16))
    mem_out = jnp.concatenate(
        [(_dot(probs[h], kv_ref[:, w + h * hd:w + (h + 1) * hd]) / norms[h]).astype(BF16) for h in heads], axis=1)
    merged = merged + gate_ref[:, 2 * d:3 * d].astype(F32) * _dot(mem_out, wm_ref[...])
    y = _dot(merged.astype(BF16), wo_ref[...])
    o_ref[...] = x_ref[...] + _rms(y, g_ref[...])


def _merge(x2, fox, rw, mq, memkv, gates, wf, wr, wm, wo, g, *, seq, mem_len, tm=MERGE_ROWS):
    t, d = x2.shape
    assert seq % tm == 0
    tiles_per_seq = seq // tm
    row = lambda width: pl.BlockSpec((tm, width), lambda i: (i, 0))
    return pl.pallas_call(
        _merge_kernel,
        grid=(t // tm,),
        in_specs=[row(d), row(fox.shape[1]), row(rw.shape[1]), row(mq.shape[1]),
                  pl.BlockSpec((mem_len, memkv.shape[1]), lambda i: (i // tiles_per_seq, 0)), row(gates.shape[1]),
                  _const_spec(wf.shape), _const_spec(wr.shape), _const_spec(wm.shape), _const_spec(wo.shape),
                  _const_spec(g.shape)],
        out_specs=row(d),
        out_shape=jax.ShapeDtypeStruct((t, d), F32),
        compiler_params=_params(1),
        name="merge_out",
    )(x2, fox, rw, mq, memkv, gates, wf, wr, wm, wo, g)


def _ffn_kernel(h_ref, g1_ref, wg_ref, wu_ref, wd_ref, g2_ref, o_ref, *, ff_chunk):
    h = h_ref[...]
    u = _rms(h, g1_ref[...]).astype(BF16)
    d_ff = wg_ref.shape[1]
    acc = jnp.zeros(h.shape, F32)
    for lo in range(0, d_ff, ff_chunk):
        hi = min(lo + ff_chunk, d_ff)
        gt = _dot(u, wg_ref[:, lo:hi])
        up = _dot(u, wu_ref[:, lo:hi])
        act = (gt * _sigmoid(gt) * up).astype(BF16)
        acc = acc + _dot(act, wd_ref[lo:hi, :])
    o_ref[...] = h + _rms(acc, g2_ref[...])


def _ffn(h2, g1, wg, wu, wd, g2, tm=FFN_ROWS, ff_chunk=FFN_CHUNK):
    t, d = h2.shape
    assert t % tm == 0
    row = pl.BlockSpec((tm, d), lambda i: (i, 0))
    return pl.pallas_call(
        functools.partial(_ffn_kernel, ff_chunk=ff_chunk),
        grid=(t // tm,),
        in_specs=[row, _const_spec(g1.shape), _const_spec(wg.shape), _const_spec(wu.shape),
                  _const_spec(wd.shape), _const_spec(g2.shape)],
        out_specs=row,
        out_shape=jax.ShapeDtypeStruct((t, d), F32),
        compiler_params=_params(1),
        name="ffn",
    )(h2, g1, wg, wu, wd, g2)


def _block_diag_ones(width, block, value):
    idx = jnp.arange(width) // block
    return jnp.where(idx[:, None] == idx[None, :], value, 0.0).astype(BF16)


def kernel(x, mem, pre1_g, post1_g, pre2_g, post2_g, mem_norm_g, w_in, fox_f_bias, rwkv_mu, rwkv_w0,
           rwkv_w_up, rwkv_a0, rwkv_a_up, rwkv_g_up, rwkv_k_k, rwkv_k_a, rwkv_r_k, rwkv_gn_g, rwkv_gn_b,
           w_mem_kv, w_fox_out, w_rwkv_out, w_mem_out, w_o, w_ffn_gate, w_ffn_up, w_ffn_down):
    batch, seq, d = x.shape
    mem_len = mem.shape[1]
    depth = w_in.shape[0]
    fox_heads = fox_f_bias.shape[1]
    fox_w = fox_heads * HEAD_DIM
    rw_w = rwkv_w0.shape[1]
    rw_cols = rwkv_mu.shape[1]
    mem_w = w_mem_kv.shape[2] // 2
    dec_lora = rwkv_w_up.shape[1]
    aaa_lora = rwkv_a_up.shape[1]
    assert dec_lora + aaa_lora == LANES and dec_lora == aaa_lora
    assert fox_heads <= SUBLANES and fox_w % LANES == 0 and rw_w % LANES == 0
    row2 = lambda v: v.reshape(1, -1)

    h = x.reshape(batch * seq, d)
    mem2 = mem.reshape(batch * mem_len, d)
    bd_ones = _block_diag_ones(rw_w, HEAD_DIM, 1.0)
    bd_avg = _block_diag_ones(rw_w, HEAD_DIM, 1.0 / HEAD_DIM)
    for l in range(depth):
        wi = w_in[l]
        o_ff = 3 * fox_w
        o_rw = o_ff + fox_heads
        o_mq = o_rw + rw_cols
        o_gt = o_mq + mem_w
        scale = HEAD_DIM ** -0.5 * LOG2E
        sections = [("qkv", jnp.concatenate([wi[:, :fox_w] * scale, wi[:, fox_w:o_ff]], axis=1)),
                    ("ff", jnp.pad(wi[:, o_ff:o_rw], ((0, 0), (0, LANES - fox_heads)))),
                    ("rwkv", wi[:, o_rw:o_mq]), ("memq", wi[:, o_mq:o_gt]), ("gate", wi[:, o_gt:])]
        w_all = jnp.concatenate([sec for _, sec in sections], axis=1).astype(BF16)
        cols, start = {}, 0
        for name, sec in sections:
            assert sec.shape[1] % LANES == 0
            cols[name] = (start, start + sec.shape[1])
            start += sec.shape[1]
        fbias = jnp.pad(row2(fox_f_bias[l]), ((0, 0), (0, LANES - fox_heads)))
        wlora = jnp.zeros((LANES, 2 * rw_w), F32)
        wlora = wlora.at[:dec_lora, :rw_w].set(rwkv_w_up[l]).at[dec_lora:, rw_w:].set(rwkv_a_up[l]).astype(BF16)

        qkv, crow, mq, gates, *scan_in = _inproj(
            h, row2(pre1_g[l]), w_all, cols, fbias, row2(rwkv_mu[l]), row2(rwkv_w0[l]), row2(rwkv_a0[l]),
            row2(rwkv_k_k[l]), row2(rwkv_k_a[l]), row2(rwkv_r_k[l]), wlora, rwkv_g_up[l].astype(BF16), bd_ones,
            seq=seq, mem_scale=(mem_w // MEM_HEADS) ** -0.5)

        fox_out = _fox(qkv, crow, batch=batch, seq=seq)

        memkv = _memkv(mem2, row2(mem_norm_g[l]), w_mem_kv[l].astype(BF16))
        rwkv_out = _rwkv_scan(*scan_in, row2(rwkv_gn_g[l]), row2(rwkv_gn_b[l]), bd_avg, batch=batch, seq=seq)

        h = _merge(h, fox_out, rwkv_out, mq, memkv, gates, w_fox_out[l].astype(BF16),
                   w_rwkv_out[l].astype(BF16), w_mem_out[l].astype(BF16), w_o[l].astype(BF16),
                   row2(post1_g[l]), seq=seq, mem_len=mem_len)
        h = _ffn(h, row2(pre2_g[l]), w_ffn_gate[l].astype(BF16), w_ffn_up[l].astype(BF16),
                 w_ffn_down[l].astype(BF16), row2(post2_g[l]))
    return h.reshape(batch, seq, d).astype(x.dtype)
```

```python
import functools

import jax
import jax.numpy as jnp
from jax import lax
from jax.experimental import pallas as pl
from jax.experimental.pallas import tpu as pltpu

F32 = jnp.float32
BF16 = jnp.bfloat16

NORM_EPS = 1e-6
GN_EPS = 64e-5
HEAD_DIM = 64
LANES = 128
SUBLANES = 8
LOG2E = 1.4426950408889634
MEM_HEADS = 4
CHUNK = 64
PAIR = 2 * CHUNK
F32_TERMS = 3
DECAY_SUM_TERMS = 2
HEAD_SUM_TERMS = 1
NEG_BIG = -1e30
VMEM_LIMIT = 56 * 1024 * 1024

INPROJ_GROUP = 256
INPROJ_ROWS = 512
SCAN_ROWS = 1024
SCAN_GROUP_CHUNKS = 4
FOX_KEY_TILE = 512
FOX_ROW_BLOCK = 128
FOX_SCORES_AHEAD = 2
FOX_PAIRS_PER_STEP = 2
MEMKV_ROWS = 512
MERGE_ROWS = 1024
FFN_ROWS = 1024
FFN_CHUNK = 1024


def _dot(a, b):
    return jnp.dot(a, b, preferred_element_type=F32)


def _dot_nt(a, b):
    return lax.dot_general(a, b, (((1,), (1,)), ((), ())), preferred_element_type=F32)


def _dot_tn(a, b):
    return lax.dot_general(a, b, (((0,), (0,)), ((), ())), preferred_element_type=F32)


def _bf16_terms(x, n):
    terms = []
    for _ in range(n - 1):
        t = x.astype(BF16)
        terms.append(t)
        x = x - t.astype(F32)
    return terms + [x.astype(BF16)]


def _dot_terms_lhs(m_bf16, x, n):
    return sum(_dot(m_bf16, t) for t in _bf16_terms(x, n))


def _dot_terms_rhs(x, m_bf16, n):
    return sum(_dot(t, m_bf16) for t in _bf16_terms(x, n))


def _rms(xf, g):
    return xf * lax.rsqrt(jnp.mean(xf * xf, axis=-1, keepdims=True) + NORM_EPS) * g


def _sigmoid(x):
    return 1.0 / (1.0 + jnp.exp(-x))


def _log_sigmoid(x):
    return jnp.minimum(x, 0.0) - jnp.log(1.0 + jnp.exp(-jnp.abs(x)))


def _const_spec(shape):
    nd = len(shape)
    return pl.BlockSpec(shape, lambda *_: (0,) * nd, pipeline_mode=pl.Buffered(1))


def _params(n_axes):
    return pltpu.CompilerParams(dimension_semantics=("arbitrary",) * n_axes,
                                vmem_limit_bytes=VMEM_LIMIT)


def _inproj_kernel(x_ref, g_ref, w_ref, fb_ref, mu_ref, w0_ref, a0_ref, kk_ref, ka_ref, rk_ref,
                   wlora_ref, gup_ref, bd_ref,
                   qkv_ref, crow_ref, mq_ref, gate_ref,
                   at_ref, rt_ref, bt_ref, kt_ref, bh_ref, kh_ref, v_ref, gam_ref, bonus_ref, g_out_ref,
                   carry_ref, prev_ref, *, tiles_per_seq, mem_scale, cols):
    i = pl.program_id(0)
    w = v_ref.shape[1]

    @pl.when(i % tiles_per_seq == 0)
    def _():
        carry_ref[...] = jnp.zeros_like(carry_ref)
        prev_ref[...] = jnp.zeros_like(prev_ref)

    n_groups = x_ref.shape[0] // INPROJ_GROUP
    for gi in range(n_groups):
        rows = slice(gi * INPROJ_GROUP, (gi + 1) * INPROJ_GROUP)
        _inproj_rows(rows, gi, x_ref, g_ref, w_ref, fb_ref, mu_ref, w0_ref, a0_ref, kk_ref, ka_ref, rk_ref,
                     wlora_ref, gup_ref, bd_ref, qkv_ref, crow_ref, mq_ref, gate_ref,
                     at_ref, rt_ref, bt_ref, kt_ref, bh_ref, kh_ref, v_ref, gam_ref, bonus_ref, g_out_ref,
                     carry_ref, prev_ref, w=w, mem_scale=mem_scale, cols=cols)


def _inproj_rows(rows, gi, x_ref, g_ref, w_ref, fb_ref, mu_ref, w0_ref, a0_ref, kk_ref, ka_ref, rk_ref,
                 wlora_ref, gup_ref, bd_ref, qkv_ref, crow_ref, mq_ref, gate_ref,
                 at_ref, rt_ref, bt_ref, kt_ref, bh_ref, kh_ref, v_ref, gam_ref, bonus_ref, g_out_ref,
                 carry_ref, prev_ref, *, w, mem_scale, cols):
    tm = INPROJ_GROUP
    chunks_per_group = tm // CHUNK

    def proj(name, lo=0, hi=None):
        start, stop = cols[name]
        return _dot(u, w_ref[:, start + lo:(stop if hi is None else start + hi)])

    u = _rms(x_ref[rows, :], g_ref[...]).astype(BF16)

    p = proj("rwkv")
    rowid = lax.broadcasted_iota(jnp.int32, p.shape, 0)
    shifted = jnp.where(rowid == 0, prev_ref[...], pltpu.roll(p, 1, axis=0))
    prev_ref[...] = p[tm - 1:tm, :]
    xs = p + (shifted - p) * mu_ref[...]
    r = xs[:, 0:w]
    k = xs[:, w:2 * w]
    v = xs[:, 2 * w:3 * w]
    lora_in = xs[:, 3 * w:3 * w + LANES]
    gd = xs[:, 3 * w + LANES:]
    lane = lax.broadcasted_iota(jnp.int32, lora_in.shape, 1)
    lora_in = jnp.where(lane < lora_in.shape[1] // 2, jnp.tanh(lora_in), lora_in)
    qkv_ref[rows, :] = proj("qkv").astype(BF16)
    lora = _dot(lora_in.astype(BF16), wlora_ref[...])
    g_out_ref[rows, :] = _dot(_sigmoid(gd).astype(BF16), gup_ref[...])
    w_log = -jnp.exp(_log_sigmoid(w0_ref[...] + lora[:, :w]) - 0.5)
    a = _sigmoid(a0_ref[...] + lora[:, w:])

    d = gate_ref.shape[1] // 3
    gate_ref[rows, 0:d] = _sigmoid(proj("gate", 0, d)).astype(BF16)

    bd = bd_ref[...]
    kk = k * kk_ref[...]
    kk = kk * lax.rsqrt(jnp.maximum(_dot_terms_rhs(kk * kk, bd, HEAD_SUM_TERMS), 1e-24))
    k2 = k * (1.0 + (a - 1.0) * ka_ref[...])
    bonus_ref[rows, :] = _dot_terms_rhs(r * k2 * rk_ref[...], bd, HEAD_SUM_TERMS) * v

    ti = lax.broadcasted_iota(jnp.int32, (tm, tm), 0)
    si = lax.broadcasted_iota(jnp.int32, (tm, tm), 1)
    same = (ti // CHUNK) == (si // CHUNK)
    lower = jnp.where(same & (si <= ti), 1.0, 0.0).astype(BF16)
    upper = jnp.where(same & (si > ti), 1.0, 0.0).astype(BF16)
    w_terms = _bf16_terms(w_log, DECAY_SUM_TERMS)
    cum = sum(_dot(lower, t) for t in w_terms)
    rem = sum(_dot(upper, t) for t in w_terms)
    ci = lax.broadcasted_iota(jnp.int32, (SUBLANES, tm), 0)
    sj = lax.broadcasted_iota(jnp.int32, (SUBLANES, tm), 1)
    sel = jnp.where(ci == sj // CHUNK, 1.0, 0.0).astype(BF16)
    gam = jnp.exp(sum(_dot(sel, t) for t in w_terms))
    gam_ref[0, gi * chunks_per_group:(gi + 1) * chunks_per_group, :] = gam[:chunks_per_group]

    gate_ref[rows, d:2 * d] = _sigmoid(proj("gate", d, 2 * d)).astype(BF16)

    e_prev = jnp.exp(cum - w_log)
    e_cum = jnp.exp(cum)
    e_neg = jnp.exp(-cum)
    e_rem = jnp.exp(rem)
    b = kk * a
    at_ref[rows, :] = (-kk * e_prev).astype(BF16)
    rt_ref[rows, :] = (r * e_cum).astype(BF16)
    bt_ref[rows, :] = (b * e_neg).astype(BF16)
    kt_ref[rows, :] = (k2 * e_neg).astype(BF16)
    bh_ref[rows, :] = (b * e_rem).astype(BF16)
    kh_ref[rows, :] = (k2 * e_rem).astype(BF16)
    v_ref[rows, :] = v.astype(BF16)

    gate_ref[rows, 2 * d:3 * d] = _sigmoid(proj("gate", 2 * d, 3 * d)).astype(BF16)
    mq_ref[rows, :] = (proj("memq") * mem_scale).astype(BF16)

    ls = _log_sigmoid(proj("ff") + fb_ref[...])
    tri = jnp.where(ti >= si, 1.0, 0.0).astype(BF16)
    c = _dot_terms_lhs(tri, ls, F32_TERMS) + carry_ref[...]
    carry_ref[...] = c[tm - 1:tm, :]
    crow_ref[:, rows] = jnp.transpose(c * LOG2E)[:crow_ref.shape[0], :]


def _inproj(x2, g, w_all, cols, fbias, mu, w0, a0, k_k, k_a, r_k, wlora, gup, bd, *, seq, mem_scale,
            tm=INPROJ_ROWS):
    t, d = x2.shape
    w = w0.shape[1]
    assert seq % tm == 0 and tm % INPROJ_GROUP == 0 and INPROJ_GROUP % CHUNK == 0
    n_tiles = t // tm
    width = lambda name: cols[name][1] - cols[name][0]
    tok = lambda n, dt: jax.ShapeDtypeStruct((t, n), dt)
    n_sel = max(SUBLANES, tm // CHUNK)
    outs = ([tok(width("qkv"), BF16), jax.ShapeDtypeStruct((SUBLANES, t), F32), tok(width("memq"), BF16),
             tok(width("gate"), BF16)] + [tok(w, BF16)] * 7
            + [jax.ShapeDtypeStruct((n_tiles, n_sel, w), F32), tok(w, F32), tok(w, F32)])
    row = lambda n: pl.BlockSpec((tm, n), lambda i: (i, 0))
    consts = (g, w_all, fbias, mu, w0, a0, k_k, k_a, r_k, wlora, gup, bd)
    gam_index = 11
    results = list(pl.pallas_call(
        functools.partial(_inproj_kernel, tiles_per_seq=seq // tm, mem_scale=mem_scale, cols=cols),
        grid=(n_tiles,),
        in_specs=[row(d)] + [_const_spec(c.shape) for c in consts],
        out_specs=([row(width("qkv")), pl.BlockSpec((SUBLANES, tm), lambda i: (0, i)), row(width("memq")),
                    row(width("gate"))] + [row(w)] * 7
                   + [pl.BlockSpec((1, n_sel, w), lambda i: (i, 0, 0)), row(w), row(w)]),
        out_shape=outs,
        scratch_shapes=[pltpu.VMEM((1, LANES), F32), pltpu.VMEM((1, cols["rwkv"][1] - cols["rwkv"][0]), F32)],
        compiler_params=_params(1),
        name="inproj",
    )(x2, *consts))
    results[gam_index] = results[gam_index][:, :tm // CHUNK]
    return results


def _fox_kernel(q_ref, k_ref, v_ref, crow_ref, o_ref, *, bk, rb, ahead):
    seq = q_ref.shape[0]
    n_local = q_ref.shape[1] // LANES
    first_head = 2 * n_local * pl.program_id(1)
    first_q = lax.broadcasted_iota(jnp.int32, (rb, LANES), 1) < HEAD_DIM
    lower = (lax.broadcasted_iota(jnp.int32, (rb, rb), 0) >= lax.broadcasted_iota(jnp.int32, (rb, rb), 1))
    sub = lax.broadcasted_iota(jnp.int32, crow_ref.shape, 0)
    crow_all = crow_ref[...]
    crow = [jnp.sum(jnp.where(sub == first_head + h, crow_all, 0.0), axis=0, keepdims=True)
            for h in range(2 * n_local)]

    units = []
    for r in range(seq // rb):
        stop = (r + 1) * rb
        units += [(r, k0, min(bk, stop - k0), pp) for k0 in range(0, stop, bk) for pp in range(n_local)]
    units.sort(key=lambda u: (u[0] * rb // bk, u[1], u[0], u[3]))

    scores, state = {}, {}

    def issue_scores(u):
        r, k0, width, pp = u
        lanes = slice(pp * LANES, (pp + 1) * LANES)
        q2 = q_ref[r * rb:(r + 1) * rb, lanes]
        zero = jnp.zeros_like(q2)
        kt = k_ref[k0:k0 + width, lanes]
        scores[u, 0] = _dot_nt(jnp.where(first_q, q2, zero), kt)
        scores[u, 1] = _dot_nt(jnp.where(first_q, zero, q2), kt)

    def finish(u):
        r, k0, width, pp = u
        lanes = slice(pp * LANES, (pp + 1) * LANES)
        vt = v_ref[k0:k0 + width, lanes]
        one = jnp.ones_like(vt)
        first_k = lax.broadcasted_iota(jnp.int32, vt.shape, 1) < HEAD_DIM
        on_diagonal = k0 + width == (r + 1) * rb
        for e in range(2):
            v_aug = jnp.where(first_k, vt, one) if e == 0 else jnp.where(first_k, one, vt)
            s = scores.pop((u, e)) - crow[2 * pp + e][:, k0:k0 + width]
            if on_diagonal:
                tail = jnp.where(lower, s[:, width - rb:], NEG_BIG)
                s = tail if width == rb else jnp.concatenate([s[:, :width - rb], tail], axis=1)
            m_tile = jnp.max(s, axis=-1, keepdims=True)
            if k0 == 0:
                m_new = m_tile
                acc = _dot(jnp.exp2(s - m_new).astype(BF16), v_aug)
            else:
                m_old, acc_old = state[pp, r, e]
                m_new = jnp.maximum(m_old, m_tile)
                acc = jnp.exp2(m_old - m_new) * acc_old + _dot(jnp.exp2(s - m_new).astype(BF16), v_aug)
            state[pp, r, e] = (m_new, acc)
        if on_diagonal:
            a0, a1 = state.pop((pp, r, 0))[1], state.pop((pp, r, 1))[1]
            den = jnp.where(first_q, pltpu.roll(a0, HEAD_DIM, axis=1), pltpu.roll(a1, HEAD_DIM, axis=1))
            o_ref[r * rb:(r + 1) * rb, lanes] = (jnp.where(first_q, a0, a1) / den).astype(o_ref.dtype)

    for u in units[:ahead]:
        issue_scores(u)
    for n, u in enumerate(units):
        if n + ahead < len(units):
            issue_scores(units[n + ahead])
        finish(u)


def _fox(qkv, crow, *, batch, seq, bk=FOX_KEY_TILE, rb=FOX_ROW_BLOCK, ahead=FOX_SCORES_AHEAD,
         pairs=FOX_PAIRS_PER_STEP):
    t = qkv.shape[0]
    n_groups = qkv.shape[1] // (3 * LANES * pairs)
    width = pairs * LANES
    return pl.pallas_call(
        functools.partial(_fox_kernel, bk=bk, rb=rb, ahead=ahead),
        grid=(batch, n_groups),
        in_specs=[
            pl.BlockSpec((seq, width), lambda b, h: (b, h)),
            pl.BlockSpec((seq, width), lambda b, h: (b, n_groups + h)),
            pl.BlockSpec((seq, width), lambda b, h: (b, 2 * n_groups + h)),
            pl.BlockSpec((crow.shape[0], seq), lambda b, h: (0, b)),
        ],
        out_specs=pl.BlockSpec((seq, width), lambda b, h: (b, h)),
        out_shape=jax.ShapeDtypeStruct((t, n_groups * width), BF16),
        compiler_params=_params(2),
        name="fox_attention",
    )(qkv, qkv, qkv, crow)


def _memkv_kernel(m_ref, g_ref, w_ref, o_ref):
    o_ref[...] = _dot(_rms(m_ref[...], g_ref[...]).astype(BF16), w_ref[...]).astype(BF16)


def _memkv(mem2, g, w, tm=MEMKV_ROWS):
    t, d = mem2.shape
    assert t % tm == 0
    return pl.pallas_call(
        _memkv_kernel,
        grid=(t // tm,),
        in_specs=[pl.BlockSpec((tm, d), lambda i: (i, 0)), _const_spec(g.shape), _const_spec(w.shape)],
        out_specs=pl.BlockSpec((tm, w.shape[1]), lambda i: (i, 0)),
        out_shape=jax.ShapeDtypeStruct((t, w.shape[1]), BF16),
        compiler_params=_params(1),
        name="mem_kv",
    )(mem2, g, w)


def _rwkv_scan_kernel(at_ref, rt_ref, bt_ref, kt_ref, bh_ref, kh_ref, v_ref, gam_ref, bonus_ref, g_ref,
                      gng_ref, gnb_ref, bdavg_ref, o_ref, state_ref, y_ref):
    @pl.when(pl.program_id(1) == 0)
    def _():
        state_ref[...] = jnp.zeros_like(state_ref)

    n_t, w = v_ref.shape
    n_pairs = w // LANES
    n_chunks = n_t // CHUNK
    lane = lax.broadcasted_iota(jnp.int32, (CHUNK, LANES), 1)
    first = lane < HEAD_DIM
    ri = lax.broadcasted_iota(jnp.int32, (PAIR, PAIR), 0)
    ci = lax.broadcasted_iota(jnp.int32, (PAIR, PAIR), 1)
    strict = ri > ci
    eye = jnp.where(ri == ci, 1.0, 0.0)
    incl2 = (lax.broadcasted_iota(jnp.int32, (PAIR, 2 * PAIR), 0)
             >= (lax.broadcasted_iota(jnp.int32, (PAIR, 2 * PAIR), 1) & (PAIR - 1)))

    def stacked(ref, c, p):
        x = ref[c * CHUNK:(c + 1) * CHUNK, p * LANES:(p + 1) * LANES]
        z = jnp.zeros_like(x)
        return jnp.concatenate([jnp.where(first, x, z), jnp.where(first, z, x)], axis=0)

    pairs = range(n_pairs)
    states = [state_ref[p] for p in pairs]
    p_mat, d_mat, m_t, c_t = {}, {}, {}, {}

    def chain_step(c):
        h_bf = [states[p].astype(BF16) for p in pairs]
        for p in pairs:
            y_s = _dot_nt(p_mat.pop((c, p)), h_bf[p]) + d_mat.pop((c, p))
            gam = gam_ref[0, c:c + 1, p * LANES:(p + 1) * LANES]
            states[p] = states[p] * gam + _dot(h_bf[p], m_t.pop((c, p))) + c_t.pop((c, p))
            y_ref[c * CHUNK:(c + 1) * CHUNK, p * LANES:(p + 1) * LANES] = y_s[:CHUNK] + y_s[CHUNK:]

    def chunk_stages(cps):
        gram, s_inv, q_pow, a_ak, a_rbk, x_rhs, sx = {}, {}, {}, {}, {}, {}, {}
        for cp in cps:
            lhs = jnp.concatenate([stacked(at_ref, *cp), stacked(rt_ref, *cp)], axis=0)
            rhs = jnp.concatenate([stacked(bt_ref, *cp), stacked(kt_ref, *cp)], axis=0)
            gram[cp] = _dot_nt(lhs, rhs)
        yield
        for cp in cps:
            g_all = gram.pop(cp)
            n_ab = jnp.where(strict, g_all[:PAIR, :PAIR], 0.0)
            a_ak[cp] = jnp.where(strict, g_all[:PAIR, PAIR:], 0.0).astype(BF16)
            a_rbk[cp] = jnp.where(incl2, g_all[PAIR:, :], 0.0).astype(BF16)
            s_inv[cp] = eye + n_ab
            q_pow[cp] = n_ab.astype(BF16)
        for cp in cps:
            q_pow[cp] = _dot(q_pow[cp], q_pow[cp]).astype(BF16)
            x_rhs[cp] = _dot(a_ak.pop(cp), stacked(v_ref, *cp))
        yield
        span = 2
        while 2 * span < CHUNK:
            for cp in cps:
                both = _dot(q_pow[cp], jnp.concatenate([s_inv[cp].astype(BF16), q_pow[cp]], axis=1))
                s_inv[cp] = s_inv[cp] + both[:, :PAIR]
                q_pow[cp] = both[:, PAIR:].astype(BF16)
            span *= 2
            yield
        for cp in cps:
            s_inv[cp] = (s_inv[cp] + _dot(q_pow.pop(cp), s_inv[cp].astype(BF16))).astype(BF16)
        yield
        for cp in cps:
            x = jnp.concatenate([stacked(at_ref, *cp), x_rhs.pop(cp).astype(BF16)], axis=1)
            sx[cp] = _dot(s_inv.pop(cp), x).astype(BF16)
        yield
        for cp in cps:
            zero = jnp.zeros((PAIR, PAIR), BF16)
            lower_rows = jnp.concatenate([zero, stacked(v_ref, *cp)], axis=1)
            full = jnp.concatenate([sx.pop(cp), lower_rows], axis=0)
            pd = _dot(a_rbk.pop(cp), full)
            p_mat[cp] = (pd[:, :PAIR] + stacked(rt_ref, *cp).astype(F32)).astype(BF16)
            d_mat[cp] = pd[:, PAIR:]
            mc = _dot_tn(full, jnp.concatenate([stacked(bh_ref, *cp), stacked(kh_ref, *cp)], axis=0))
            m_t[cp] = mc[:PAIR].astype(BF16)
            c_t[cp] = mc[PAIR:]
        yield

    group = min(n_chunks, SCAN_GROUP_CHUNKS)
    pending = []
    for c0 in range(0, n_chunks, group):
        chunks = range(c0, min(c0 + group, n_chunks))
        for _ in chunk_stages([(c, p) for c in chunks for p in pairs]):
            if pending:
                chain_step(pending.pop(0))
        pending += list(chunks)
    for c in pending:
        chain_step(c)
    for p in pairs:
        state_ref[p] = states[p]

    y = y_ref[...]
    bdavg = bdavg_ref[...]
    dev = y - _dot_terms_rhs(y, bdavg, HEAD_SUM_TERMS + 1)
    var = _dot_terms_rhs(dev * dev, bdavg, HEAD_SUM_TERMS)
    yn = dev * lax.rsqrt(var + GN_EPS) * gng_ref[...] + gnb_ref[...]
    o_ref[...] = ((yn + bonus_ref[...]) * g_ref[...]).astype(o_ref.dtype)


def _rwkv_scan(at, rt, bt, kt, bh, kh, v, gam, bonus, g, gn_g, gn_b, bdavg, *, batch, seq, tl=SCAN_ROWS):
    t, w = v.shape
    assert seq % tl == 0 and tl % CHUNK == 0
    n_sel = tl // CHUNK
    gam = gam.reshape(t // tl, n_sel, w)
    nt = seq // tl
    tok = pl.BlockSpec((tl, w), lambda b, i: (b * nt + i, 0))
    return pl.pallas_call(
        _rwkv_scan_kernel,
        grid=(batch, nt),
        in_specs=[tok] * 7 + [pl.BlockSpec((1, n_sel, w), lambda b, i: (b * nt + i, 0, 0)), tok, tok,
                              _const_spec(gn_g.shape), _const_spec(gn_b.shape), _const_spec(bdavg.shape)],
        out_specs=tok,
        out_shape=jax.ShapeDtypeStruct((t, w), BF16),
        scratch_shapes=[pltpu.VMEM((w // LANES, LANES, LANES), F32), pltpu.VMEM((tl, w), F32)],
        compiler_params=_params(2),
        name="rwkv_scan",
    )(at, rt, bt, kt, bh, kh, v, gam, bonus, g, gn_g, gn_b, bdavg)


def _merge_kernel(x_ref, fox_ref, rw_ref, mq_ref, kv_ref, gate_ref, wf_ref, wr_ref, wm_ref, wo_ref, g_ref, o_ref):
    d = x_ref.shape[1]
    w = mq_ref.shape[1]
    hd = w // MEM_HEADS
    heads = range(MEM_HEADS)
    scores = [_dot_nt(mq_ref[:, h * hd:(h + 1) * hd], kv_ref[:, h * hd:(h + 1) * hd]) for h in heads]
    merged = (gate_ref[:, 0:d].astype(F32) * _dot(fox_ref[...], wf_ref[...])
              + gate_ref[:, d:2 * d].astype(F32) * _dot(rw_ref[...], wr_ref[...]))
    probs, norms = [], []
    for h in heads:
        p = jnp.exp(scores[h] - jnp.max(scores[h], axis=-1, keepdims=True))
        norms.append(jnp.sum(p, axis=-1, keepdims=True))
        probs.append(p.astype(BF16))
    mem_out = jnp.concatenate(
        [(_dot(probs[h], kv_ref[:, w + h * hd:w + (h + 1) * hd]) / norms[h]).astype(BF16) for h in heads], axis=1)
    merged = merged + gate_ref[:, 2 * d:3 * d].astype(F32) * _dot(mem_out, wm_ref[...])
    y = _dot(merged.astype(BF16), wo_ref[...])
    o_ref[...] = x_ref[...] + _rms(y, g_ref[...])


def _merge(x2, fox, rw, mq, memkv, gates, wf, wr, wm, wo, g, *, seq, mem_len, tm=MERGE_ROWS):
    t, d = x2.shape
    assert seq % tm == 0
    tiles_per_seq = seq // tm
    row = lambda width: pl.BlockSpec((tm, width), lambda i: (i, 0))
    return pl.pallas_call(
        _merge_kernel,
        grid=(t // tm,),
        in_specs=[row(d), row(fox.shape[1]), row(rw.shape[1]), row(mq.shape[1]),
                  pl.BlockSpec((mem_len, memkv.shape[1]), lambda i: (i // tiles_per_seq, 0)), row(gates.shape[1]),
                  _const_spec(wf.shape), _const_spec(wr.shape), _const_spec(wm.shape), _const_spec(wo.shape),
                  _const_spec(g.shape)],
        out_specs=row(d),
        out_shape=jax.ShapeDtypeStruct((t, d), F32),
        compiler_params=_params(1),
        name="merge_out",
    )(x2, fox, rw, mq, memkv, gates, wf, wr, wm, wo, g)


def _ffn_kernel(h_ref, g1_ref, wg_ref, wu_ref, wd_ref, g2_ref, o_ref, *, ff_chunk):
    h = h_ref[...]
    u = _rms(h, g1_ref[...]).astype(BF16)
    d_ff = wg_ref.shape[1]
    acc = jnp.zeros(h.shape, F32)
    for lo in range(0, d_ff, ff_chunk):
        hi = min(lo + ff_chunk, d_ff)
        gt = _dot(u, wg_ref[:, lo:hi])
        up = _dot(u, wu_ref[:, lo:hi])
        act = (gt * _sigmoid(gt) * up).astype(BF16)
        acc = acc + _dot(act, wd_ref[lo:hi, :])
    o_ref[...] = h + _rms(acc, g2_ref[...])


def _ffn(h2, g1, wg, wu, wd, g2, tm=FFN_ROWS, ff_chunk=FFN_CHUNK):
    t, d = h2.shape
    assert t % tm == 0
    row = pl.BlockSpec((tm, d), lambda i: (i, 0))
    return pl.pallas_call(
        functools.partial(_ffn_kernel, ff_chunk=ff_chunk),
        grid=(t // tm,),
        in_specs=[row, _const_spec(g1.shape), _const_spec(wg.shape), _const_spec(wu.shape),
                  _const_spec(wd.shape), _const_spec(g2.shape)],
        out_specs=row,
        out_shape=jax.ShapeDtypeStruct((t, d), F32),
        compiler_params=_params(1),
        name="ffn",
    )(h2, g1, wg, wu, wd, g2)


def _block_diag_ones(width, block, value):
    idx = jnp.arange(width) // block
    return jnp.where(idx[:, None] == idx[None, :], value, 0.0).astype(BF16)


def kernel(x, mem, pre1_g, post1_g, pre2_g, post2_g, mem_norm_g, w_in, fox_f_bias, rwkv_mu, rwkv_w0,
           rwkv_w_up, rwkv_a0, rwkv_a_up, rwkv_g_up, rwkv_k_k, rwkv_k_a, rwkv_r_k, rwkv_gn_g, rwkv_gn_b,
           w_mem_kv, w_fox_out, w_rwkv_out, w_mem_out, w_o, w_ffn_gate, w_ffn_up, w_ffn_down):
    batch, seq, d = x.shape
    mem_len = mem.shape[1]
    depth = w_in.shape[0]
    fox_heads = fox_f_bias.shape[1]
    fox_w = fox_heads * HEAD_DIM
    rw_w = rwkv_w0.shape[1]
    rw_cols = rwkv_mu.shape[1]
    mem_w = w_mem_kv.shape[2] // 2
    dec_lora = rwkv_w_up.shape[1]
    aaa_lora = rwkv_a_up.shape[1]
    assert dec_lora + aaa_lora == LANES and dec_lora == aaa_lora
    assert fox_heads <= SUBLANES and fox_w % LANES == 0 and rw_w % LANES == 0
    row2 = lambda v: v.reshape(1, -1)

    h = x.reshape(batch * seq, d)
    mem2 = mem.reshape(batch * mem_len, d)
    bd_ones = _block_diag_ones(rw_w, HEAD_DIM, 1.0)
    bd_avg = _block_diag_ones(rw_w, HEAD_DIM, 1.0 / HEAD_DIM)
    for l in range(depth):
        wi = w_in[l]
        o_ff = 3 * fox_w
        o_rw = o_ff + fox_heads
        o_mq = o_rw + rw_cols
        o_gt = o_mq + mem_w
        scale = HEAD_DIM ** -0.5 * LOG2E
        sections = [("qkv", jnp.concatenate([wi[:, :fox_w] * scale, wi[:, fox_w:o_ff]], axis=1)),
                    ("ff", jnp.pad(wi[:, o_ff:o_rw], ((0, 0), (0, LANES - fox_heads)))),
                    ("rwkv", wi[:, o_rw:o_mq]), ("memq", wi[:, o_mq:o_gt]), ("gate", wi[:, o_gt:])]
        w_all = jnp.concatenate([sec for _, sec in sections], axis=1).astype(BF16)
        cols, start = {}, 0
        for name, sec in sections:
            assert sec.shape[1] % LANES == 0
            cols[name] = (start, start + sec.shape[1])
            start += sec.shape[1]
        fbias = jnp.pad(row2(fox_f_bias[l]), ((0, 0), (0, LANES - fox_heads)))
        wlora = jnp.zeros((LANES, 2 * rw_w), F32)
        wlora = wlora.at[:dec_lora, :rw_w].set(rwkv_w_up[l]).at[dec_lora:, rw_w:].set(rwkv_a_up[l]).astype(BF16)

        qkv, crow, mq, gates, *scan_in = _inproj(
            h, row2(pre1_g[l]), w_all, cols, fbias, row2(rwkv_mu[l]), row2(rwkv_w0[l]), row2(rwkv_a0[l]),
            row2(rwkv_k_k[l]), row2(rwkv_k_a[l]), row2(rwkv_r_k[l]), wlora, rwkv_g_up[l].astype(BF16), bd_ones,
            seq=seq, mem_scale=(mem_w // MEM_HEADS) ** -0.5)

        fox_out = _fox(qkv, crow, batch=batch, seq=seq)

        memkv = _memkv(mem2, row2(mem_norm_g[l]), w_mem_kv[l].astype(BF16))
        rwkv_out = _rwkv_scan(*scan_in, row2(rwkv_gn_g[l]), row2(rwkv_gn_b[l]), bd_avg, batch=batch, seq=seq)

        h = _merge(h, fox_out, rwkv_out, mq, memkv, gates, w_fox_out[l].astype(BF16),
                   w_rwkv_out[l].astype(BF16), w_mem_out[l].astype(BF16), w_o[l].astype(BF16),
                   row2(post1_g[l]), seq=seq, mem_len=mem_len)
        h = _ffn(h, row2(pre2_g[l]), w_ffn_gate[l].astype(BF16), w_ffn_up[l].astype(BF16),
                 w_ffn_down[l].astype(BF16), row2(post2_g[l]))
    return h.reshape(batch, seq, d).astype(x.dtype)
```

```python
import functools

import jax
import jax.numpy as jnp
from jax import lax
from jax.experimental import pallas as pl
from jax.experimental.pallas import tpu as pltpu

F32 = jnp.float32
BF16 = jnp.bfloat16

NORM_EPS = 1e-6
GN_EPS = 64e-5
HEAD_DIM = 64
LANES = 128
SUBLANES = 8
LOG2E = 1.4426950408889634
MEM_HEADS = 4
CHUNK = 64
PAIR = 2 * CHUNK
F32_TERMS = 3
DECAY_SUM_TERMS = 2
HEAD_SUM_TERMS = 1
NEG_BIG = -1e30
VMEM_LIMIT = 56 * 1024 * 1024

INPROJ_GROUP = 256
INPROJ_ROWS = 512
SCAN_ROWS = 1024
SCAN_GROUP_CHUNKS = 4
FOX_KEY_TILE = 512
FOX_ROW_BLOCK = 128
FOX_SCORES_AHEAD = 4
FOX_PAIRS_PER_STEP = 2
MEMKV_ROWS = 512
MERGE_ROWS = 1024
FFN_ROWS = 1024
FFN_CHUNK = 1024


def _dot(a, b):
    return jnp.dot(a, b, preferred_element_type=F32)


def _dot_nt(a, b):
    return lax.dot_general(a, b, (((1,), (1,)), ((), ())), preferred_element_type=F32)


def _dot_tn(a, b):
    return lax.dot_general(a, b, (((0,), (0,)), ((), ())), preferred_element_type=F32)


def _bf16_terms(x, n):
    terms = []
    for _ in range(n - 1):
        t = x.astype(BF16)
        terms.append(t)
        x = x - t.astype(F32)
    return terms + [x.astype(BF16)]


def _dot_terms_lhs(m_bf16, x, n):
    return sum(_dot(m_bf16, t) for t in _bf16_terms(x, n))


def _dot_terms_rhs(x, m_bf16, n):
    return sum(_dot(t, m_bf16) for t in _bf16_terms(x, n))


def _rms(xf, g):
    return xf * lax.rsqrt(jnp.mean(xf * xf, axis=-1, keepdims=True) + NORM_EPS) * g


def _sigmoid(x):
    return 1.0 / (1.0 + jnp.exp(-x))


def _log_sigmoid(x):
    return jnp.minimum(x, 0.0) - jnp.log(1.0 + jnp.exp(-jnp.abs(x)))


def _const_spec(shape):
    nd = len(shape)
    return pl.BlockSpec(shape, lambda *_: (0,) * nd, pipeline_mode=pl.Buffered(1))


def _params(n_axes):
    return pltpu.CompilerParams(dimension_semantics=("arbitrary",) * n_axes,
                                vmem_limit_bytes=VMEM_LIMIT)


def _inproj_kernel(x_ref, g_ref, w_ref, fb_ref, mu_ref, w0_ref, a0_ref, kk_ref, ka_ref, rk_ref,
                   wlora_ref, gup_ref, bd_ref,
                   qkv_ref, crow_ref, mq_ref, gate_ref,
                   at_ref, rt_ref, bt_ref, kt_ref, bh_ref, kh_ref, v_ref, gam_ref, bonus_ref, g_out_ref,
                   carry_ref, prev_ref, *, tiles_per_seq, mem_scale, cols):
    i = pl.program_id(0)
    w = v_ref.shape[1]

    @pl.when(i % tiles_per_seq == 0)
    def _():
        carry_ref[...] = jnp.zeros_like(carry_ref)
        prev_ref[...] = jnp.zeros_like(prev_ref)

    n_groups = x_ref.shape[0] // INPROJ_GROUP
    for gi in range(n_groups):
        rows = slice(gi * INPROJ_GROUP, (gi + 1) * INPROJ_GROUP)
        _inproj_rows(rows, gi, x_ref, g_ref, w_ref, fb_ref, mu_ref, w0_ref, a0_ref, kk_ref, ka_ref, rk_ref,
                     wlora_ref, gup_ref, bd_ref, qkv_ref, crow_ref, mq_ref, gate_ref,
                     at_ref, rt_ref, bt_ref, kt_ref, bh_ref, kh_ref, v_ref, gam_ref, bonus_ref, g_out_ref,
                     carry_ref, prev_ref, w=w, mem_scale=mem_scale, cols=cols)


def _inproj_rows(rows, gi, x_ref, g_ref, w_ref, fb_ref, mu_ref, w0_ref, a0_ref, kk_ref, ka_ref, rk_ref,
                 wlora_ref, gup_ref, bd_ref, qkv_ref, crow_ref, mq_ref, gate_ref,
                 at_ref, rt_ref, bt_ref, kt_ref, bh_ref, kh_ref, v_ref, gam_ref, bonus_ref, g_out_ref,
                 carry_ref, prev_ref, *, w, mem_scale, cols):
    tm = INPROJ_GROUP
    chunks_per_group = tm // CHUNK

    def proj(name, lo=0, hi=None):
        start, stop = cols[name]
        return _dot(u, w_ref[:, start + lo:(stop if hi is None else start + hi)])

    u = _rms(x_ref[rows, :], g_ref[...]).astype(BF16)

    p = proj("rwkv")
    rowid = lax.broadcasted_iota(jnp.int32, p.shape, 0)
    shifted = jnp.where(rowid == 0, prev_ref[...], pltpu.roll(p, 1, axis=0))
    prev_ref[...] = p[tm - 1:tm, :]
    xs = p + (shifted - p) * mu_ref[...]
    r = xs[:, 0:w]
    k = xs[:, w:2 * w]
    v = xs[:, 2 * w:3 * w]
    lora_in = xs[:, 3 * w:3 * w + LANES]
    gd = xs[:, 3 * w + LANES:]
    lane = lax.broadcasted_iota(jnp.int32, lora_in.shape, 1)
    lora_in = jnp.where(lane < lora_in.shape[1] // 2, jnp.tanh(lora_in), lora_in)
    qkv_ref[rows, :] = proj("qkv").astype(BF16)
    lora = _dot(lora_in.astype(BF16), wlora_ref[...])
    g_out_ref[rows, :] = _dot(_sigmoid(gd).astype(BF16), gup_ref[...])
    w_log = -jnp.exp(_log_sigmoid(w0_ref[...] + lora[:, :w]) - 0.5)
    a = _sigmoid(a0_ref[...] + lora[:, w:])

    d = gate_ref.shape[1] // 3
    gate_ref[rows, 0:d] = _sigmoid(proj("gate", 0, d)).astype(BF16)

    bd = bd_ref[...]
    kk = k * kk_ref[...]
    kk = kk * lax.rsqrt(jnp.maximum(_dot_terms_rhs(kk * kk, bd, HEAD_SUM_TERMS), 1e-24))
    k2 = k * (1.0 + (a - 1.0) * ka_ref[...])
    bonus_ref[rows, :] = _dot_terms_rhs(r * k2 * rk_ref[...], bd, HEAD_SUM_TERMS) * v

    ti = lax.broadcasted_iota(jnp.int32, (tm, tm), 0)
    si = lax.broadcasted_iota(jnp.int32, (tm, tm), 1)
    same = (ti // CHUNK) == (si // CHUNK)
    lower = jnp.where(same & (si <= ti), 1.0, 0.0).astype(BF16)
    upper = jnp.where(same & (si > ti), 1.0, 0.0).astype(BF16)
    w_terms = _bf16_terms(w_log, DECAY_SUM_TERMS)
    cum = sum(_dot(lower, t) for t in w_terms)
    rem = sum(_dot(upper, t) for t in w_terms)
    ci = lax.broadcasted_iota(jnp.int32, (SUBLANES, tm), 0)
    sj = lax.broadcasted_iota(jnp.int32, (SUBLANES, tm), 1)
    sel = jnp.where(ci == sj // CHUNK, 1.0, 0.0).astype(BF16)
    gam = jnp.exp(sum(_dot(sel, t) for t in w_terms))
    gam_ref[0, gi * chunks_per_group:(gi + 1) * chunks_per_group, :] = gam[:chunks_per_group]

    gate_ref[rows, d:2 * d] = _sigmoid(proj("gate", d, 2 * d)).astype(BF16)

    e_prev = jnp.exp(cum - w_log)
    e_cum = jnp.exp(cum)
    e_neg = jnp.exp(-cum)
    e_rem = jnp.exp(rem)
    b = kk * a
    at_ref[rows, :] = (-kk * e_prev).astype(BF16)
    rt_ref[rows, :] = (r * e_cum).astype(BF16)
    bt_ref[rows, :] = (b * e_neg).astype(BF16)
    kt_ref[rows, :] = (k2 * e_neg).astype(BF16)
    bh_ref[rows, :] = (b * e_rem).astype(BF16)
    kh_ref[rows, :] = (k2 * e_rem).astype(BF16)
    v_ref[rows, :] = v.astype(BF16)

    gate_ref[rows, 2 * d:3 * d] = _sigmoid(proj("gate", 2 * d, 3 * d)).astype(BF16)
    mq_ref[rows, :] = (proj("memq") * mem_scale).astype(BF16)

    ls = _log_sigmoid(proj("ff") + fb_ref[...])
    tri = jnp.where(ti >= si, 1.0, 0.0).astype(BF16)
    c = _dot_terms_lhs(tri, ls, F32_TERMS) + carry_ref[...]
    carry_ref[...] = c[tm - 1:tm, :]
    crow_ref[:, rows] = jnp.transpose(c * LOG2E)[:crow_ref.shape[0], :]


def _inproj(x2, g, w_all, cols, fbias, mu, w0, a0, k_k, k_a, r_k, wlora, gup, bd, *, seq, mem_scale,
            tm=INPROJ_ROWS):
    t, d = x2.shape
    w = w0.shape[1]
    assert seq % tm == 0 and tm % INPROJ_GROUP == 0 and INPROJ_GROUP % CHUNK == 0
    n_tiles = t // tm
    width = lambda name: cols[name][1] - cols[name][0]
    tok = lambda n, dt: jax.ShapeDtypeStruct((t, n), dt)
    n_sel = max(SUBLANES, tm // CHUNK)
    outs = ([tok(width("qkv"), BF16), jax.ShapeDtypeStruct((SUBLANES, t), F32), tok(width("memq"), BF16),
             tok(width("gate"), BF16)] + [tok(w, BF16)] * 7
            + [jax.ShapeDtypeStruct((n_tiles, n_sel, w), F32), tok(w, F32), tok(w, F32)])
    row = lambda n: pl.BlockSpec((tm, n), lambda i: (i, 0))
    consts = (g, w_all, fbias, mu, w0, a0, k_k, k_a, r_k, wlora, gup, bd)
    gam_index = 11
    results = list(pl.pallas_call(
        functools.partial(_inproj_kernel, tiles_per_seq=seq // tm, mem_scale=mem_scale, cols=cols),
        grid=(n_tiles,),
        in_specs=[row(d)] + [_const_spec(c.shape) for c in consts],
        out_specs=([row(width("qkv")), pl.BlockSpec((SUBLANES, tm), lambda i: (0, i)), row(width("memq")),
                    row(width("gate"))] + [row(w)] * 7
                   + [pl.BlockSpec((1, n_sel, w), lambda i: (i, 0, 0)), row(w), row(w)]),
        out_shape=outs,
        scratch_shapes=[pltpu.VMEM((1, LANES), F32), pltpu.VMEM((1, cols["rwkv"][1] - cols["rwkv"][0]), F32)],
        compiler_params=_params(1),
        name="inproj",
    )(x2, *consts))
    results[gam_index] = results[gam_index][:, :tm // CHUNK]
    return results


def _fox_kernel(q_ref, k_ref, v_ref, crow_ref, o_ref, *, bk, rb, ahead):
    seq = q_ref.shape[0]
    n_local = q_ref.shape[1] // LANES
    first_head = 2 * n_local * pl.program_id(1)
    first_q = lax.broadcasted_iota(jnp.int32, (rb, LANES), 1) < HEAD_DIM
    lower = (lax.broadcasted_iota(jnp.int32, (rb, rb), 0) >= lax.broadcasted_iota(jnp.int32, (rb, rb), 1))
    sub = lax.broadcasted_iota(jnp.int32, crow_ref.shape, 0)
    crow_all = crow_ref[...]
    crow = [jnp.sum(jnp.where(sub == first_head + h, crow_all, 0.0), axis=0, keepdims=True)
            for h in range(2 * n_local)]

    units = []
    for r in range(seq // rb):
        stop = (r + 1) * rb
        units += [(r, k0, min(bk, stop - k0), pp) for k0 in range(0, stop, bk) for pp in range(n_local)]
    units.sort(key=lambda u: (u[0] * rb // bk, u[1], u[0], u[3]))

    scores, state = {}, {}

    def issue_scores(u):
        r, k0, width, pp = u
        lanes = slice(pp * LANES, (pp + 1) * LANES)
        q2 = q_ref[r * rb:(r + 1) * rb, lanes]
        zero = jnp.zeros_like(q2)
        kt = k_ref[k0:k0 + width, lanes]
        scores[u, 0] = _dot_nt(jnp.where(first_q, q2, zero), kt)
        scores[u, 1] = _dot_nt(jnp.where(first_q, zero, q2), kt)

    def finish(u):
        r, k0, width, pp = u
        lanes = slice(pp * LANES, (pp + 1) * LANES)
        vt = v_ref[k0:k0 + width, lanes]
        one = jnp.ones_like(vt)
        first_k = lax.broadcasted_iota(jnp.int32, vt.shape, 1) < HEAD_DIM
        on_diagonal = k0 + width == (r + 1) * rb
        for e in range(2):
            v_aug = jnp.where(first_k, vt, one) if e == 0 else jnp.where(first_k, one, vt)
            s = scores.pop((u, e)) - crow[2 * pp + e][:, k0:k0 + width]
            if on_diagonal:
                tail = jnp.where(lower, s[:, width - rb:], NEG_BIG)
                s = tail if width == rb else jnp.concatenate([s[:, :width - rb], tail], axis=1)
            m_tile = jnp.max(s, axis=-1, keepdims=True)
            if k0 == 0:
                m_new = m_tile
                acc = _dot(jnp.exp2(s - m_new).astype(BF16), v_aug)
            else:
                m_old, acc_old = state[pp, r, e]
                m_new = jnp.maximum(m_old, m_tile)
                acc = jnp.exp2(m_old - m_new) * acc_old + _dot(jnp.exp2(s - m_new).astype(BF16), v_aug)
            state[pp, r, e] = (m_new, acc)
        if on_diagonal:
            a0, a1 = state.pop((pp, r, 0))[1], state.pop((pp, r, 1))[1]
            den = jnp.where(first_q, pltpu.roll(a0, HEAD_DIM, axis=1), pltpu.roll(a1, HEAD_DIM, axis=1))
            o_ref[r * rb:(r + 1) * rb, lanes] = (jnp.where(first_q, a0, a1) / den).astype(o_ref.dtype)

    for u in units[:ahead]:
        issue_scores(u)
    for n, u in enumerate(units):
        if n + ahead < len(units):
            issue_scores(units[n + ahead])
        finish(u)


def _fox(qkv, crow, *, batch, seq, bk=FOX_KEY_TILE, rb=FOX_ROW_BLOCK, ahead=FOX_SCORES_AHEAD,
         pairs=FOX_PAIRS_PER_STEP):
    t = qkv.shape[0]
    n_groups = qkv.shape[1] // (3 * LANES * pairs)
    width = pairs * LANES
    return pl.pallas_call(
        functools.partial(_fox_kernel, bk=bk, rb=rb, ahead=ahead),
        grid=(batch, n_groups),
        in_specs=[
            pl.BlockSpec((seq, width), lambda b, h: (b, h)),
            pl.BlockSpec((seq, width), lambda b, h: (b, n_groups + h)),
            pl.BlockSpec((seq, width), lambda b, h: (b, 2 * n_groups + h)),
            pl.BlockSpec((crow.shape[0], seq), lambda b, h: (0, b)),
        ],
        out_specs=pl.BlockSpec((seq, width), lambda b, h: (b, h)),
        out_shape=jax.ShapeDtypeStruct((t, n_groups * width), BF16),
        compiler_params=_params(2),
        name="fox_attention",
    )(qkv, qkv, qkv, crow)


def _memkv_kernel(m_ref, g_ref, w_ref, o_ref):
    o_ref[...] = _dot(_rms(m_ref[...], g_ref[...]).astype(BF16), w_ref[...]).astype(BF16)


def _memkv(mem2, g, w, tm=MEMKV_ROWS):
    t, d = mem2.shape
    assert t % tm == 0
    return pl.pallas_call(
        _memkv_kernel,
        grid=(t // tm,),
        in_specs=[pl.BlockSpec((tm, d), lambda i: (i, 0)), _const_spec(g.shape), _const_spec(w.shape)],
        out_specs=pl.BlockSpec((tm, w.shape[1]), lambda i: (i, 0)),
        out_shape=jax.ShapeDtypeStruct((t, w.shape[1]), BF16),
        compiler_params=_params(1),
        name="mem_kv",
    )(mem2, g, w)


def _rwkv_scan_kernel(at_ref, rt_ref, bt_ref, kt_ref, bh_ref, kh_ref, v_ref, gam_ref, bonus_ref, g_ref,
                      gng_ref, gnb_ref, bdavg_ref, o_ref, state_ref, y_ref):
    @pl.when(pl.program_id(1) == 0)
    def _():
        state_ref[...] = jnp.zeros_like(state_ref)

    n_t, w = v_ref.shape
    n_pairs = w // LANES
    n_chunks = n_t // CHUNK
    lane = lax.broadcasted_iota(jnp.int32, (CHUNK, LANES), 1)
    first = lane < HEAD_DIM
    ri = lax.broadcasted_iota(jnp.int32, (PAIR, PAIR), 0)
    ci = lax.broadcasted_iota(jnp.int32, (PAIR, PAIR), 1)
    strict = ri > ci
    eye = jnp.where(ri == ci, 1.0, 0.0)
    incl2 = (lax.broadcasted_iota(jnp.int32, (PAIR, 2 * PAIR), 0)
             >= (lax.broadcasted_iota(jnp.int32, (PAIR, 2 * PAIR), 1) & (PAIR - 1)))

    def stacked(ref, c, p):
        x = ref[c * CHUNK:(c + 1) * CHUNK, p * LANES:(p + 1) * LANES]
        z = jnp.zeros_like(x)
        return jnp.concatenate([jnp.where(first, x, z), jnp.where(first, z, x)], axis=0)

    pairs = range(n_pairs)
    states = [state_ref[p] for p in pairs]
    p_mat, d_mat, m_t, c_t = {}, {}, {}, {}

    def chain_step(c):
        h_bf = [states[p].astype(BF16) for p in pairs]
        for p in pairs:
            y_s = _dot_nt(p_mat.pop((c, p)), h_bf[p]) + d_mat.pop((c, p))
            gam = gam_ref[0, c:c + 1, p * LANES:(p + 1) * LANES]
            states[p] = states[p] * gam + _dot(h_bf[p], m_t.pop((c, p))) + c_t.pop((c, p))
            y_ref[c * CHUNK:(c + 1) * CHUNK, p * LANES:(p + 1) * LANES] = y_s[:CHUNK] + y_s[CHUNK:]

    def chunk_stages(cps):
        gram, s_inv, q_pow, a_ak, a_rbk, x_rhs, sx = {}, {}, {}, {}, {}, {}, {}
        for cp in cps:
            lhs = jnp.concatenate([stacked(at_ref, *cp), stacked(rt_ref, *cp)], axis=0)
            rhs = jnp.concatenate([stacked(bt_ref, *cp), stacked(kt_ref, *cp)], axis=0)
            gram[cp] = _dot_nt(lhs, rhs)
        yield
        for cp in cps:
            g_all = gram.pop(cp)
            n_ab = jnp.where(strict, g_all[:PAIR, :PAIR], 0.0)
            a_ak[cp] = jnp.where(strict, g_all[:PAIR, PAIR:], 0.0).astype(BF16)
            a_rbk[cp] = jnp.where(incl2, g_all[PAIR:, :], 0.0).astype(BF16)
            s_inv[cp] = eye + n_ab
            q_pow[cp] = n_ab.astype(BF16)
        for cp in cps:
            q_pow[cp] = _dot(q_pow[cp], q_pow[cp]).astype(BF16)
            x_rhs[cp] = _dot(a_ak.pop(cp), stacked(v_ref, *cp))
        yield
        span = 2
        while 2 * span < CHUNK:
            for cp in cps:
                both = _dot(q_pow[cp], jnp.concatenate([s_inv[cp].astype(BF16), q_pow[cp]], axis=1))
                s_inv[cp] = s_inv[cp] + both[:, :PAIR]
                q_pow[cp] = both[:, PAIR:].astype(BF16)
            span *= 2
            yield
        for cp in cps:
            s_inv[cp] = (s_inv[cp] + _dot(q_pow.pop(cp), s_inv[cp].astype(BF16))).astype(BF16)
        yield
        for cp in cps:
            x = jnp.concatenate([stacked(at_ref, *cp), x_rhs.pop(cp).astype(BF16)], axis=1)
            sx[cp] = _dot(s_inv.pop(cp), x).astype(BF16)
        yield
        for cp in cps:
            zero = jnp.zeros((PAIR, PAIR), BF16)
            lower_rows = jnp.concatenate([zero, stacked(v_ref, *cp)], axis=1)
            full = jnp.concatenate([sx.pop(cp), lower_rows], axis=0)
            pd = _dot(a_rbk.pop(cp), full)
            p_mat[cp] = (pd[:, :PAIR] + stacked(rt_ref, *cp).astype(F32)).astype(BF16)
            d_mat[cp] = pd[:, PAIR:]
            mc = _dot_tn(full, jnp.concatenate([stacked(bh_ref, *cp), stacked(kh_ref, *cp)], axis=0))
            m_t[cp] = mc[:PAIR].astype(BF16)
            c_t[cp] = mc[PAIR:]
        yield

    group = min(n_chunks, SCAN_GROUP_CHUNKS)
    pending = []
    for c0 in range(0, n_chunks, group):
        chunks = range(c0, min(c0 + group, n_chunks))
        for _ in chunk_stages([(c, p) for c in chunks for p in pairs]):
            if pending:
                chain_step(pending.pop(0))
        pending += list(chunks)
    for c in pending:
        chain_step(c)
    for p in pairs:
        state_ref[p] = states[p]

    y = y_ref[...]
    bdavg = bdavg_ref[...]
    dev = y - _dot_terms_rhs(y, bdavg, HEAD_SUM_TERMS + 1)
    var = _dot_terms_rhs(dev * dev, bdavg, HEAD_SUM_TERMS)
    yn = dev * lax.rsqrt(var + GN_EPS) * gng_ref[...] + gnb_ref[...]
    o_ref[...] = ((yn + bonus_ref[...]) * g_ref[...]).astype(o_ref.dtype)


def _rwkv_scan(at, rt, bt, kt, bh, kh, v, gam, bonus, g, gn_g, gn_b, bdavg, *, batch, seq, tl=SCAN_ROWS):
    t, w = v.shape
    assert seq % tl == 0 and tl % CHUNK == 0
    n_sel = tl // CHUNK
    gam = gam.reshape(t // tl, n_sel, w)
    nt = seq // tl
    tok = pl.BlockSpec((tl, w), lambda b, i: (b * nt + i, 0))
    return pl.pallas_call(
        _rwkv_scan_kernel,
        grid=(batch, nt),
        in_specs=[tok] * 7 + [pl.BlockSpec((1, n_sel, w), lambda b, i: (b * nt + i, 0, 0)), tok, tok,
                              _const_spec(gn_g.shape), _const_spec(gn_b.shape), _const_spec(bdavg.shape)],
        out_specs=tok,
        out_shape=jax.ShapeDtypeStruct((t, w), BF16),
        scratch_shapes=[pltpu.VMEM((w // LANES, LANES, LANES), F32), pltpu.VMEM((tl, w), F32)],
        compiler_params=_params(2),
        name="rwkv_scan",
    )(at, rt, bt, kt, bh, kh, v, gam, bonus, g, gn_g, gn_b, bdavg)


def _merge_kernel(x_ref, fox_ref, rw_ref, mq_ref, kv_ref, gate_ref, wf_ref, wr_ref, wm_ref, wo_ref, g_ref, o_ref):
    d = x_ref.shape[1]
    w = mq_ref.shape[1]
    hd = w // MEM_HEADS
    heads = range(MEM_HEADS)
    scores = [_dot_nt(mq_ref[:, h * hd:(h + 1) * hd], kv_ref[:, h * hd:(h + 1) * hd]) for h in heads]
    merged = (gate_ref[:, 0:d].astype(F32) * _dot(fox_ref[...], wf_ref[...])
              + gate_ref[:, d:2 * d].astype(F32) * _dot(rw_ref[...], wr_ref[...]))
    probs, norms = [], []
    for h in heads:
        p = jnp.exp(scores[h] - jnp.max(scores[h], axis=-1, keepdims=True))
        norms.append(jnp.sum(p, axis=-1, keepdims=True))
        probs.append(p.astype(BF16))
    mem_out = jnp.concatenate(
        [(_dot(probs[h], kv_ref[:, w + h * hd:w + (h + 1) * hd]) / norms[h]).astype(BF16) for h in heads], axis=1)
    merged = merged + gate_ref[:, 2 * d:3 * d].astype(F32) * _dot(mem_out, wm_ref[...])
    y = _dot(merged.astype(BF16), wo_ref[...])
    o_ref[...] = x_ref[...] + _rms(y, g_ref[...])


def _merge(x2, fox, rw, mq, memkv, gates, wf, wr, wm, wo, g, *, seq, mem_len, tm=MERGE_ROWS):
    t, d = x2.shape
    assert seq % tm == 0
    tiles_per_seq = seq // tm
    row = lambda width: pl.BlockSpec((tm, width), lambda i: (i, 0))
    return pl.pallas_call(
        _merge_kernel,
        grid=(t // tm,),
        in_specs=[row(d), row(fox.shape[1]), row(rw.shape[1]), row(mq.shape[1]),
                  pl.BlockSpec((mem_len, memkv.shape[1]), lambda i: (i // tiles_per_seq, 0)), row(gates.shape[1]),
                  _const_spec(wf.shape), _const_spec(wr.shape), _const_spec(wm.shape), _const_spec(wo.shape),
                  _const_spec(g.shape)],
        out_specs=row(d),
        out_shape=jax.ShapeDtypeStruct((t, d), F32),
        compiler_params=_params(1),
        name="merge_out",
    )(x2, fox, rw, mq, memkv, gates, wf, wr, wm, wo, g)


def _ffn_kernel(h_ref, g1_ref, wg_ref, wu_ref, wd_ref, g2_ref, o_ref, *, ff_chunk):
    h = h_ref[...]
    u = _rms(h, g1_ref[...]).astype(BF16)
    d_ff = wg_ref.shape[1]
    acc = jnp.zeros(h.shape, F32)
    for lo in range(0, d_ff, ff_chunk):
        hi = min(lo + ff_chunk, d_ff)
        gt = _dot(u, wg_ref[:, lo:hi])
        up = _dot(u, wu_ref[:, lo:hi])
        act = (gt * _sigmoid(gt) * up).astype(BF16)
        acc = acc + _dot(act, wd_ref[lo:hi, :])
    o_ref[...] = h + _rms(acc, g2_ref[...])


def _ffn(h2, g1, wg, wu, wd, g2, tm=FFN_ROWS, ff_chunk=FFN_CHUNK):
    t, d = h2.shape
    assert t % tm == 0
    row = pl.BlockSpec((tm, d), lambda i: (i, 0))
    return pl.pallas_call(
        functools.partial(_ffn_kernel, ff_chunk=ff_chunk),
        grid=(t // tm,),
        in_specs=[row, _const_spec(g1.shape), _const_spec(wg.shape), _const_spec(wu.shape),
                  _const_spec(wd.shape), _const_spec(g2.shape)],
        out_specs=row,
        out_shape=jax.ShapeDtypeStruct((t, d), F32),
        compiler_params=_params(1),
        name="ffn",
    )(h2, g1, wg, wu, wd, g2)


def _block_diag_ones(width, block, value):
    idx = jnp.arange(width) // block
    return jnp.where(idx[:, None] == idx[None, :], value, 0.0).astype(BF16)


def kernel(x, mem, pre1_g, post1_g, pre2_g, post2_g, mem_norm_g, w_in, fox_f_bias, rwkv_mu, rwkv_w0,
           rwkv_w_up, rwkv_a0, rwkv_a_up, rwkv_g_up, rwkv_k_k, rwkv_k_a, rwkv_r_k, rwkv_gn_g, rwkv_gn_b,
           w_mem_kv, w_fox_out, w_rwkv_out, w_mem_out, w_o, w_ffn_gate, w_ffn_up, w_ffn_down):
    batch, seq, d = x.shape
    mem_len = mem.shape[1]
    depth = w_in.shape[0]
    fox_heads = fox_f_bias.shape[1]
    fox_w = fox_heads * HEAD_DIM
    rw_w = rwkv_w0.shape[1]
    rw_cols = rwkv_mu.shape[1]
    mem_w = w_mem_kv.shape[2] // 2
    dec_lora = rwkv_w_up.shape[1]
    aaa_lora = rwkv_a_up.shape[1]
    assert dec_lora + aaa_lora == LANES and dec_lora == aaa_lora
    assert fox_heads <= SUBLANES and fox_w % LANES == 0 and rw_w % LANES == 0
    row2 = lambda v: v.reshape(1, -1)

    h = x.reshape(batch * seq, d)
    mem2 = mem.reshape(batch * mem_len, d)
    bd_ones = _block_diag_ones(rw_w, HEAD_DIM, 1.0)
    bd_avg = _block_diag_ones(rw_w, HEAD_DIM, 1.0 / HEAD_DIM)
    for l in range(depth):
        wi = w_in[l]
        o_ff = 3 * fox_w
        o_rw = o_ff + fox_heads
        o_mq = o_rw + rw_cols
        o_gt = o_mq + mem_w
        scale = HEAD_DIM ** -0.5 * LOG2E
        sections = [("qkv", jnp.concatenate([wi[:, :fox_w] * scale, wi[:, fox_w:o_ff]], axis=1)),
                    ("ff", jnp.pad(wi[:, o_ff:o_rw], ((0, 0), (0, LANES - fox_heads)))),
                    ("rwkv", wi[:, o_rw:o_mq]), ("memq", wi[:, o_mq:o_gt]), ("gate", wi[:, o_gt:])]
        w_all = jnp.concatenate([sec for _, sec in sections], axis=1).astype(BF16)
        cols, start = {}, 0
        for name, sec in sections:
            assert sec.shape[1] % LANES == 0
            cols[name] = (start, start + sec.shape[1])
            start += sec.shape[1]
        fbias = jnp.pad(row2(fox_f_bias[l]), ((0, 0), (0, LANES - fox_heads)))
        wlora = jnp.zeros((LANES, 2 * rw_w), F32)
        wlora = wlora.at[:dec_lora, :rw_w].set(rwkv_w_up[l]).at[dec_lora:, rw_w:].set(rwkv_a_up[l]).astype(BF16)

        qkv, crow, mq, gates, *scan_in = _inproj(
            h, row2(pre1_g[l]), w_all, cols, fbias, row2(rwkv_mu[l]), row2(rwkv_w0[l]), row2(rwkv_a0[l]),
            row2(rwkv_k_k[l]), row2(rwkv_k_a[l]), row2(rwkv_r_k[l]), wlora, rwkv_g_up[l].astype(BF16), bd_ones,
            seq=seq, mem_scale=(mem_w // MEM_HEADS) ** -0.5)

        fox_out = _fox(qkv, crow, batch=batch, seq=seq)

        memkv = _memkv(mem2, row2(mem_norm_g[l]), w_mem_kv[l].astype(BF16))
        rwkv_out = _rwkv_scan(*scan_in, row2(rwkv_gn_g[l]), row2(rwkv_gn_b[l]), bd_avg, batch=batch, seq=seq)

        h = _merge(h, fox_out, rwkv_out, mq, memkv, gates, w_fox_out[l].astype(BF16),
                   w_rwkv_out[l].astype(BF16), w_mem_out[l].astype(BF16), w_o[l].astype(BF16),
                   row2(post1_g[l]), seq=seq, mem_len=mem_len)
        h = _ffn(h, row2(pre2_g[l]), w_ffn_gate[l].astype(BF16), w_ffn_up[l].astype(BF16),
                 w_ffn_down[l].astype(BF16), row2(post2_g[l]))
    return h.reshape(batch, seq, d).astype(x.dtype)
```

```python
import functools

import jax
import jax.numpy as jnp
from jax import lax
from jax.experimental import pallas as pl
from jax.experimental.pallas import tpu as pltpu

F32 = jnp.float32
BF16 = jnp.bfloat16

NORM_EPS = 1e-6
GN_EPS = 64e-5
HEAD_DIM = 64
LANES = 128
SUBLANES = 8
LOG2E = 1.4426950408889634
MEM_HEADS = 4
CHUNK = 64
PAIR = 2 * CHUNK
F32_TERMS = 3
DECAY_SUM_TERMS = 2
HEAD_SUM_TERMS = 1
NEG_BIG = -1e30
VMEM_LIMIT = 56 * 1024 * 1024

INPROJ_GROUP = 256
INPROJ_ROWS = 512
SCAN_ROWS = 1024
SCAN_GROUP_CHUNKS = 4
FOX_KEY_TILE = 512
FOX_ROW_BLOCK = 128
FOX_SCORES_AHEAD = 2
FOX_PAIRS_PER_STEP = 2
MERGE_ROWS = 1024
FFN_ROWS = 1024
FFN_CHUNK = 1024


def _dot(a, b):
    return jnp.dot(a, b, preferred_element_type=F32)


def _dot_nt(a, b):
    return lax.dot_general(a, b, (((1,), (1,)), ((), ())), preferred_element_type=F32)


def _dot_tn(a, b):
    return lax.dot_general(a, b, (((0,), (0,)), ((), ())), preferred_element_type=F32)


def _bf16_terms(x, n):
    terms = []
    for _ in range(n - 1):
        t = x.astype(BF16)
        terms.append(t)
        x = x - t.astype(F32)
    return terms + [x.astype(BF16)]


def _dot_terms_lhs(m_bf16, x, n):
    return sum(_dot(m_bf16, t) for t in _bf16_terms(x, n))


def _dot_terms_rhs(x, m_bf16, n):
    return sum(_dot(t, m_bf16) for t in _bf16_terms(x, n))


def _rms(xf, g):
    return xf * lax.rsqrt(jnp.mean(xf * xf, axis=-1, keepdims=True) + NORM_EPS) * g


def _sigmoid(x):
    return 1.0 / (1.0 + jnp.exp(-x))


def _log_sigmoid(x):
    return jnp.minimum(x, 0.0) - jnp.log(1.0 + jnp.exp(-jnp.abs(x)))


def _const_spec(shape):
    nd = len(shape)
    return pl.BlockSpec(shape, lambda *_: (0,) * nd, pipeline_mode=pl.Buffered(1))


def _params(n_axes):
    return pltpu.CompilerParams(dimension_semantics=("arbitrary",) * n_axes,
                                vmem_limit_bytes=VMEM_LIMIT)


def _inproj_kernel(x_ref, g_ref, w_ref, fb_ref, mu_ref, w0_ref, a0_ref, kk_ref, ka_ref, rk_ref,
                   wlora_ref, gup_ref, bd_ref,
                   qkv_ref, crow_ref, mq_ref, gate_ref,
                   at_ref, rt_ref, bt_ref, kt_ref, bh_ref, kh_ref, v_ref, gam_ref, bonus_ref, g_out_ref,
                   carry_ref, prev_ref, *, tiles_per_seq, mem_scale, cols):
    i = pl.program_id(0)
    w = v_ref.shape[1]

    @pl.when(i % tiles_per_seq == 0)
    def _():
        carry_ref[...] = jnp.zeros_like(carry_ref)
        prev_ref[...] = jnp.zeros_like(prev_ref)

    n_groups = x_ref.shape[0] // INPROJ_GROUP
    for gi in range(n_groups):
        rows = slice(gi * INPROJ_GROUP, (gi + 1) * INPROJ_GROUP)
        _inproj_rows(rows, gi, x_ref, g_ref, w_ref, fb_ref, mu_ref, w0_ref, a0_ref, kk_ref, ka_ref, rk_ref,
                     wlora_ref, gup_ref, bd_ref, qkv_ref, crow_ref, mq_ref, gate_ref,
                     at_ref, rt_ref, bt_ref, kt_ref, bh_ref, kh_ref, v_ref, gam_ref, bonus_ref, g_out_ref,
                     carry_ref, prev_ref, w=w, mem_scale=mem_scale, cols=cols)


def _inproj_rows(rows, gi, x_ref, g_ref, w_ref, fb_ref, mu_ref, w0_ref, a0_ref, kk_ref, ka_ref, rk_ref,
                 wlora_ref, gup_ref, bd_ref, qkv_ref, crow_ref, mq_ref, gate_ref,
                 at_ref, rt_ref, bt_ref, kt_ref, bh_ref, kh_ref, v_ref, gam_ref, bonus_ref, g_out_ref,
                 carry_ref, prev_ref, *, w, mem_scale, cols):
    tm = INPROJ_GROUP
    chunks_per_group = tm // CHUNK

    def proj(name, lo=0, hi=None):
        start, stop = cols[name]
        return _dot(u, w_ref[:, start + lo:(stop if hi is None else start + hi)])

    u = _rms(x_ref[rows, :], g_ref[...]).astype(BF16)

    p = proj("rwkv")
    rowid = lax.broadcasted_iota(jnp.int32, p.shape, 0)
    shifted = jnp.where(rowid == 0, prev_ref[...], pltpu.roll(p, 1, axis=0))
    prev_ref[...] = p[tm - 1:tm, :]
    xs = p + (shifted - p) * mu_ref[...]
    r = xs[:, 0:w]
    k = xs[:, w:2 * w]
    v = xs[:, 2 * w:3 * w]
    lora_in = xs[:, 3 * w:3 * w + LANES]
    gd = xs[:, 3 * w + LANES:]
    lane = lax.broadcasted_iota(jnp.int32, lora_in.shape, 1)
    lora_in = jnp.where(lane < lora_in.shape[1] // 2, jnp.tanh(lora_in), lora_in)
    qkv_ref[rows, :] = proj("qkv").astype(BF16)
    lora = _dot(lora_in.astype(BF16), wlora_ref[...])
    g_out_ref[rows, :] = _dot(_sigmoid(gd).astype(BF16), gup_ref[...])
    w_log = -jnp.exp(_log_sigmoid(w0_ref[...] + lora[:, :w]) - 0.5)
    a = _sigmoid(a0_ref[...] + lora[:, w:])

    d = gate_ref.shape[1] // 3
    gate_ref[rows, 0:d] = _sigmoid(proj("gate", 0, d)).astype(BF16)

    bd = bd_ref[...]
    kk = k * kk_ref[...]
    kk = kk * lax.rsqrt(jnp.maximum(_dot_terms_rhs(kk * kk, bd, HEAD_SUM_TERMS), 1e-24))
    k2 = k * (1.0 + (a - 1.0) * ka_ref[...])
    bonus_ref[rows, :] = _dot_terms_rhs(r * k2 * rk_ref[...], bd, HEAD_SUM_TERMS) * v

    ti = lax.broadcasted_iota(jnp.int32, (tm, tm), 0)
    si = lax.broadcasted_iota(jnp.int32, (tm, tm), 1)
    same = (ti // CHUNK) == (si // CHUNK)
    lower = jnp.where(same & (si <= ti), 1.0, 0.0).astype(BF16)
    upper = jnp.where(same & (si > ti), 1.0, 0.0).astype(BF16)
    w_terms = _bf16_terms(w_log, DECAY_SUM_TERMS)
    cum = sum(_dot(lower, t) for t in w_terms)
    rem = sum(_dot(upper, t) for t in w_terms)
    ci = lax.broadcasted_iota(jnp.int32, (SUBLANES, tm), 0)
    sj = lax.broadcasted_iota(jnp.int32, (SUBLANES, tm), 1)
    sel = jnp.where(ci == sj // CHUNK, 1.0, 0.0).astype(BF16)
    gam = jnp.exp(sum(_dot(sel, t) for t in w_terms))
    gam_ref[0, gi * chunks_per_group:(gi + 1) * chunks_per_group, :] = gam[:chunks_per_group]

    gate_ref[rows, d:2 * d] = _sigmoid(proj("gate", d, 2 * d)).astype(BF16)

    e_prev = jnp.exp(cum - w_log)
    e_cum = jnp.exp(cum)
    e_neg = jnp.exp(-cum)
    e_rem = jnp.exp(rem)
    b = kk * a
    at_ref[rows, :] = (-kk * e_prev).astype(BF16)
    rt_ref[rows, :] = (r * e_cum).astype(BF16)
    bt_ref[rows, :] = (b * e_neg).astype(BF16)
    kt_ref[rows, :] = (k2 * e_neg).astype(BF16)
    bh_ref[rows, :] = (b * e_rem).astype(BF16)
    kh_ref[rows, :] = (k2 * e_rem).astype(BF16)
    v_ref[rows, :] = v.astype(BF16)

    gate_ref[rows, 2 * d:3 * d] = _sigmoid(proj("gate", 2 * d, 3 * d)).astype(BF16)
    mq_ref[rows, :] = (proj("memq") * mem_scale).astype(BF16)

    ls = _log_sigmoid(proj("ff") + fb_ref[...])
    tri = jnp.where(ti >= si, 1.0, 0.0).astype(BF16)
    c = _dot_terms_lhs(tri, ls, F32_TERMS) + carry_ref[...]
    carry_ref[...] = c[tm - 1:tm, :]
    crow_ref[:, rows] = jnp.transpose(c * LOG2E)[:crow_ref.shape[0], :]


def _inproj(x2, g, w_all, cols, fbias, mu, w0, a0, k_k, k_a, r_k, wlora, gup, bd, *, seq, mem_scale,
            tm=INPROJ_ROWS):
    t, d = x2.shape
    w = w0.shape[1]
    assert seq % tm == 0 and tm % INPROJ_GROUP == 0 and INPROJ_GROUP % CHUNK == 0
    n_tiles = t // tm
    width = lambda name: cols[name][1] - cols[name][0]
    tok = lambda n, dt: jax.ShapeDtypeStruct((t, n), dt)
    n_sel = max(SUBLANES, tm // CHUNK)
    outs = ([tok(width("qkv"), BF16), jax.ShapeDtypeStruct((SUBLANES, t), F32), tok(width("memq"), BF16),
             tok(width("gate"), BF16)] + [tok(w, BF16)] * 7
            + [jax.ShapeDtypeStruct((n_tiles, n_sel, w), F32), tok(w, F32), tok(w, F32)])
    row = lambda n: pl.BlockSpec((tm, n), lambda i: (i, 0))
    consts = (g, w_all, fbias, mu, w0, a0, k_k, k_a, r_k, wlora, gup, bd)
    gam_index = 11
    results = list(pl.pallas_call(
        functools.partial(_inproj_kernel, tiles_per_seq=seq // tm, mem_scale=mem_scale, cols=cols),
        grid=(n_tiles,),
        in_specs=[row(d)] + [_const_spec(c.shape) for c in consts],
        out_specs=([row(width("qkv")), pl.BlockSpec((SUBLANES, tm), lambda i: (0, i)), row(width("memq")),
                    row(width("gate"))] + [row(w)] * 7
                   + [pl.BlockSpec((1, n_sel, w), lambda i: (i, 0, 0)), row(w), row(w)]),
        out_shape=outs,
        scratch_shapes=[pltpu.VMEM((1, LANES), F32), pltpu.VMEM((1, cols["rwkv"][1] - cols["rwkv"][0]), F32)],
        compiler_params=_params(1),
        name="inproj",
    )(x2, *consts))
    results[gam_index] = results[gam_index][:, :tm // CHUNK]
    return results


def _fox_kernel(q_ref, k_ref, v_ref, crow_ref, o_ref, *, bk, rb, ahead):
    seq = q_ref.shape[0]
    n_local = q_ref.shape[1] // LANES
    first_head = 2 * n_local * pl.program_id(1)
    first_q = lax.broadcasted_iota(jnp.int32, (rb, LANES), 1) < HEAD_DIM
    lower = (lax.broadcasted_iota(jnp.int32, (rb, rb), 0) >= lax.broadcasted_iota(jnp.int32, (rb, rb), 1))
    sub = lax.broadcasted_iota(jnp.int32, crow_ref.shape, 0)
    crow_all = crow_ref[...]
    crow = [jnp.sum(jnp.where(sub == first_head + h, crow_all, 0.0), axis=0, keepdims=True)
            for h in range(2 * n_local)]

    units = []
    for r in range(seq // rb):
        stop = (r + 1) * rb
        units += [(r, k0, min(bk, stop - k0), pp) for k0 in range(0, stop, bk) for pp in range(n_local)]
    units.sort(key=lambda u: (u[0] * rb // bk, u[1], u[0], u[3]))

    scores, state = {}, {}

    def issue_scores(u):
        r, k0, width, pp = u
        lanes = slice(pp * LANES, (pp + 1) * LANES)
        q2 = q_ref[r * rb:(r + 1) * rb, lanes]
        zero = jnp.zeros_like(q2)
        kt = k_ref[k0:k0 + width, lanes]
        scores[u, 0] = _dot_nt(jnp.where(first_q, q2, zero), kt)
        scores[u, 1] = _dot_nt(jnp.where(first_q, zero, q2), kt)

    def finish(u):
        r, k0, width, pp = u
        lanes = slice(pp * LANES, (pp + 1) * LANES)
        vt = v_ref[k0:k0 + width, lanes]
        one = jnp.ones_like(vt)
        first_k = lax.broadcasted_iota(jnp.int32, vt.shape, 1) < HEAD_DIM
        on_diagonal = k0 + width == (r + 1) * rb
        for e in range(2):
            v_aug = jnp.where(first_k, vt, one) if e == 0 else jnp.where(first_k, one, vt)
            s = scores.pop((u, e)) - crow[2 * pp + e][:, k0:k0 + width]
            if on_diagonal:
                tail = jnp.where(lower, s[:, width - rb:], NEG_BIG)
                s = tail if width == rb else jnp.concatenate([s[:, :width - rb], tail], axis=1)
            m_tile = jnp.max(s, axis=-1, keepdims=True)
            if k0 == 0:
                m_new = m_tile
                acc = _dot(jnp.exp2(s - m_new).astype(BF16), v_aug)
            else:
                m_old, acc_old = state[pp, r, e]
                m_new = jnp.maximum(m_old, m_tile)
                acc = jnp.exp2(m_old - m_new) * acc_old + _dot(jnp.exp2(s - m_new).astype(BF16), v_aug)
            state[pp, r, e] = (m_new, acc)
        if on_diagonal:
            a0, a1 = state.pop((pp, r, 0))[1], state.pop((pp, r, 1))[1]
            den = jnp.where(first_q, pltpu.roll(a0, HEAD_DIM, axis=1), pltpu.roll(a1, HEAD_DIM, axis=1))
            o_ref[r * rb:(r + 1) * rb, lanes] = (jnp.where(first_q, a0, a1) / den).astype(o_ref.dtype)

    for u in units[:ahead]:
        issue_scores(u)
    for n, u in enumerate(units):
        if n + ahead < len(units):
            issue_scores(units[n + ahead])
        finish(u)


def _fox(qkv, crow, *, batch, seq, bk=FOX_KEY_TILE, rb=FOX_ROW_BLOCK, ahead=FOX_SCORES_AHEAD,
         pairs=FOX_PAIRS_PER_STEP):
    t = qkv.shape[0]
    n_groups = qkv.shape[1] // (3 * LANES * pairs)
    width = pairs * LANES
    return pl.pallas_call(
        functools.partial(_fox_kernel, bk=bk, rb=rb, ahead=ahead),
        grid=(batch, n_groups),
        in_specs=[
            pl.BlockSpec((seq, width), lambda b, h: (b, h)),
            pl.BlockSpec((seq, width), lambda b, h: (b, n_groups + h)),
            pl.BlockSpec((seq, width), lambda b, h: (b, 2 * n_groups + h)),
            pl.BlockSpec((crow.shape[0], seq), lambda b, h: (0, b)),
        ],
        out_specs=pl.BlockSpec((seq, width), lambda b, h: (b, h)),
        out_shape=jax.ShapeDtypeStruct((t, n_groups * width), BF16),
        compiler_params=_params(2),
        name="fox_attention",
    )(qkv, qkv, qkv, crow)


def _rwkv_scan_kernel(at_ref, rt_ref, bt_ref, kt_ref, bh_ref, kh_ref, v_ref, gam_ref, bonus_ref, g_ref,
                      gng_ref, gnb_ref, bdavg_ref, o_ref, state_ref, y_ref):
    @pl.when(pl.program_id(1) == 0)
    def _():
        state_ref[...] = jnp.zeros_like(state_ref)

    n_t, w = v_ref.shape
    n_pairs = w // LANES
    n_chunks = n_t // CHUNK
    lane = lax.broadcasted_iota(jnp.int32, (CHUNK, LANES), 1)
    first = lane < HEAD_DIM
    ri = lax.broadcasted_iota(jnp.int32, (PAIR, PAIR), 0)
    ci = lax.broadcasted_iota(jnp.int32, (PAIR, PAIR), 1)
    strict = ri > ci
    eye = jnp.where(ri == ci, 1.0, 0.0)
    incl2 = (lax.broadcasted_iota(jnp.int32, (PAIR, 2 * PAIR), 0)
             >= (lax.broadcasted_iota(jnp.int32, (PAIR, 2 * PAIR), 1) & (PAIR - 1)))

    def stacked(ref, c, p):
        x = ref[c * CHUNK:(c + 1) * CHUNK, p * LANES:(p + 1) * LANES]
        z = jnp.zeros_like(x)
        return jnp.concatenate([jnp.where(first, x, z), jnp.where(first, z, x)], axis=0)

    pairs = range(n_pairs)
    states = [state_ref[p] for p in pairs]
    p_mat, d_mat, m_t, c_t = {}, {}, {}, {}

    def chain_step(c):
        h_bf = [states[p].astype(BF16) for p in pairs]
        for p in pairs:
            y_s = _dot_nt(p_mat.pop((c, p)), h_bf[p]) + d_mat.pop((c, p))
            gam = gam_ref[0, c:c + 1, p * LANES:(p + 1) * LANES]
            states[p] = states[p] * gam + _dot(h_bf[p], m_t.pop((c, p))) + c_t.pop((c, p))
            y_ref[c * CHUNK:(c + 1) * CHUNK, p * LANES:(p + 1) * LANES] = y_s[:CHUNK] + y_s[CHUNK:]

    def chunk_stages(cps):
        gram, s_inv, q_pow, a_ak, a_rbk, x_rhs, sx = {}, {}, {}, {}, {}, {}, {}
        for cp in cps:
            lhs = jnp.concatenate([stacked(at_ref, *cp), stacked(rt_ref, *cp)], axis=0)
            rhs = jnp.concatenate([stacked(bt_ref, *cp), stacked(kt_ref, *cp)], axis=0)
            gram[cp] = _dot_nt(lhs, rhs)
        yield
        for cp in cps:
            g_all = gram.pop(cp)
            n_ab = jnp.where(strict, g_all[:PAIR, :PAIR], 0.0)
            a_ak[cp] = jnp.where(strict, g_all[:PAIR, PAIR:], 0.0).astype(BF16)
            a_rbk[cp] = jnp.where(incl2, g_all[PAIR:, :], 0.0).astype(BF16)
            s_inv[cp] = eye + n_ab
            q_pow[cp] = n_ab.astype(BF16)
        for cp in cps:
            q_pow[cp] = _dot(q_pow[cp], q_pow[cp]).astype(BF16)
            x_rhs[cp] = _dot(a_ak.pop(cp), stacked(v_ref, *cp))
        yield
        span = 2
        while 2 * span < CHUNK:
            for cp in cps:
                both = _dot(q_pow[cp], jnp.concatenate([s_inv[cp].astype(BF16), q_pow[cp]], axis=1))
                s_inv[cp] = s_inv[cp] + both[:, :PAIR]
                q_pow[cp] = both[:, PAIR:].astype(BF16)
            span *= 2
            yield
        for cp in cps:
            s_inv[cp] = (s_inv[cp] + _dot(q_pow.pop(cp), s_inv[cp].astype(BF16))).astype(BF16)
        yield
        for cp in cps:
            x = jnp.concatenate([stacked(at_ref, *cp), x_rhs.pop(cp).astype(BF16)], axis=1)
            sx[cp] = _dot(s_inv.pop(cp), x).astype(BF16)
        yield
        for cp in cps:
            zero = jnp.zeros((PAIR, PAIR), BF16)
            lower_rows = jnp.concatenate([zero, stacked(v_ref, *cp)], axis=1)
            full = jnp.concatenate([sx.pop(cp), lower_rows], axis=0)
            pd = _dot(a_rbk.pop(cp), full)
            p_mat[cp] = (pd[:, :PAIR] + stacked(rt_ref, *cp).astype(F32)).astype(BF16)
            d_mat[cp] = pd[:, PAIR:]
            mc = _dot_tn(full, jnp.concatenate([stacked(bh_ref, *cp), stacked(kh_ref, *cp)], axis=0))
            m_t[cp] = mc[:PAIR].astype(BF16)
            c_t[cp] = mc[PAIR:]
        yield

    group = min(n_chunks, SCAN_GROUP_CHUNKS)
    pending = []
    for c0 in range(0, n_chunks, group):
        chunks = range(c0, min(c0 + group, n_chunks))
        for _ in chunk_stages([(c, p) for c in chunks for p in pairs]):
            if pending:
                chain_step(pending.pop(0))
        pending += list(chunks)
    for c in pending:
        chain_step(c)
    for p in pairs:
        state_ref[p] = states[p]

    y = y_ref[...]
    bdavg = bdavg_ref[...]
    dev = y - _dot_terms_rhs(y, bdavg, HEAD_SUM_TERMS + 1)
    var = _dot_terms_rhs(dev * dev, bdavg, HEAD_SUM_TERMS)
    yn = dev * lax.rsqrt(var + GN_EPS) * gng_ref[...] + gnb_ref[...]
    o_ref[...] = ((yn + bonus_ref[...]) * g_ref[...]).astype(o_ref.dtype)


def _rwkv_scan(at, rt, bt, kt, bh, kh, v, gam, bonus, g, gn_g, gn_b, bdavg, *, batch, seq, tl=SCAN_ROWS):
    t, w = v.shape
    assert seq % tl == 0 and tl % CHUNK == 0
    n_sel = tl // CHUNK
    gam = gam.reshape(t // tl, n_sel, w)
    nt = seq // tl
    tok = pl.BlockSpec((tl, w), lambda b, i: (b * nt + i, 0))
    return pl.pallas_call(
        _rwkv_scan_kernel,
        grid=(batch, nt),
        in_specs=[tok] * 7 + [pl.BlockSpec((1, n_sel, w), lambda b, i: (b * nt + i, 0, 0)), tok, tok,
                              _const_spec(gn_g.shape), _const_spec(gn_b.shape), _const_spec(bdavg.shape)],
        out_specs=tok,
        out_shape=jax.ShapeDtypeStruct((t, w), BF16),
        scratch_shapes=[pltpu.VMEM((w // LANES, LANES, LANES), F32), pltpu.VMEM((tl, w), F32)],
        compiler_params=_params(2),
        name="rwkv_scan",
    )(at, rt, bt, kt, bh, kh, v, gam, bonus, g, gn_g, gn_b, bdavg)


def _merge_kernel(x_ref, fox_ref, rw_ref, mq_ref, mem_ref, mg_ref, wkv_ref, gate_ref, wf_ref, wr_ref, wm_ref,
                  wo_ref, g_ref, o_ref, kv_ref, *, tiles_per_seq):
    @pl.when(pl.program_id(0) % tiles_per_seq == 0)
    def _():
        kv_ref[...] = _dot(_rms(mem_ref[...], mg_ref[...]).astype(BF16), wkv_ref[...]).astype(BF16)

    d = x_ref.shape[1]
    w = mq_ref.shape[1]
    hd = w // MEM_HEADS
    heads = range(MEM_HEADS)
    scores = [_dot_nt(mq_ref[:, h * hd:(h + 1) * hd], kv_ref[:, h * hd:(h + 1) * hd]) for h in heads]
    merged = (gate_ref[:, 0:d].astype(F32) * _dot(fox_ref[...], wf_ref[...])
              + gate_ref[:, d:2 * d].astype(F32) * _dot(rw_ref[...], wr_ref[...]))
    probs, norms = [], []
    for h in heads:
        p = jnp.exp(scores[h] - jnp.max(scores[h], axis=-1, keepdims=True))
        norms.append(jnp.sum(p, axis=-1, keepdims=True))
        probs.append(p.astype(BF16))
    mem_out = jnp.concatenate(
        [(_dot(probs[h], kv_ref[:, w + h * hd:w + (h + 1) * hd]) / norms[h]).astype(BF16) for h in heads], axis=1)
    merged = merged + gate_ref[:, 2 * d:3 * d].astype(F32) * _dot(mem_out, wm_ref[...])
    y = _dot(merged.astype(BF16), wo_ref[...])
    o_ref[...] = x_ref[...] + _rms(y, g_ref[...])


def _merge(x2, fox, rw, mq, mem2, mem_g, wkv, gates, wf, wr, wm, wo, g, *, seq, mem_len, tm=MERGE_ROWS):
    t, d = x2.shape
    assert seq % tm == 0
    tiles_per_seq = seq // tm
    row = lambda width: pl.BlockSpec((tm, width), lambda i: (i, 0))
    return pl.pallas_call(
        functools.partial(_merge_kernel, tiles_per_seq=tiles_per_seq),
        grid=(t // tm,),
        in_specs=[row(d), row(fox.shape[1]), row(rw.shape[1]), row(mq.shape[1]),
                  pl.BlockSpec((mem_len, d), lambda i: (i // tiles_per_seq, 0)), _const_spec(mem_g.shape),
                  _const_spec(wkv.shape), row(gates.shape[1]),
                  _const_spec(wf.shape), _const_spec(wr.shape), _const_spec(wm.shape), _const_spec(wo.shape),
                  _const_spec(g.shape)],
        out_specs=row(d),
        out_shape=jax.ShapeDtypeStruct((t, d), F32),
        scratch_shapes=[pltpu.VMEM((mem_len, wkv.shape[1]), BF16)],
        compiler_params=_params(1),
        name="merge_out",
    )(x2, fox, rw, mq, mem2, mem_g, wkv, gates, wf, wr, wm, wo, g)


def _ffn_kernel(h_ref, g1_ref, wg_ref, wu_ref, wd_ref, g2_ref, o_ref, *, ff_chunk):
    h = h_ref[...]
    u = _rms(h, g1_ref[...]).astype(BF16)
    d_ff = wg_ref.shape[1]
    acc = jnp.zeros(h.shape, F32)
    for lo in range(0, d_ff, ff_chunk):
        hi = min(lo + ff_chunk, d_ff)
        gt = _dot(u, wg_ref[:, lo:hi])
        up = _dot(u, wu_ref[:, lo:hi])
        act = (gt * _sigmoid(gt) * up).astype(BF16)
        acc = acc + _dot(act, wd_ref[lo:hi, :])
    o_ref[...] = h + _rms(acc, g2_ref[...])


def _ffn(h2, g1, wg, wu, wd, g2, tm=FFN_ROWS, ff_chunk=FFN_CHUNK):
    t, d = h2.shape
    assert t % tm == 0
    row = pl.BlockSpec((tm, d), lambda i: (i, 0))
    return pl.pallas_call(
        functools.partial(_ffn_kernel, ff_chunk=ff_chunk),
        grid=(t // tm,),
        in_specs=[row, _const_spec(g1.shape), _const_spec(wg.shape), _const_spec(wu.shape),
                  _const_spec(wd.shape), _const_spec(g2.shape)],
        out_specs=row,
        out_shape=jax.ShapeDtypeStruct((t, d), F32),
        compiler_params=_params(1),
        name="ffn",
    )(h2, g1, wg, wu, wd, g2)


def _block_diag_ones(width, block, value):
    idx = jnp.arange(width) // block
    return jnp.where(idx[:, None] == idx[None, :], value, 0.0).astype(BF16)


def kernel(x, mem, pre1_g, post1_g, pre2_g, post2_g, mem_norm_g, w_in, fox_f_bias, rwkv_mu, rwkv_w0,
           rwkv_w_up, rwkv_a0, rwkv_a_up, rwkv_g_up, rwkv_k_k, rwkv_k_a, rwkv_r_k, rwkv_gn_g, rwkv_gn_b,
           w_mem_kv, w_fox_out, w_rwkv_out, w_mem_out, w_o, w_ffn_gate, w_ffn_up, w_ffn_down):
    batch, seq, d = x.shape
    mem_len = mem.shape[1]
    depth = w_in.shape[0]
    fox_heads = fox_f_bias.shape[1]
    fox_w = fox_heads * HEAD_DIM
    rw_w = rwkv_w0.shape[1]
    rw_cols = rwkv_mu.shape[1]
    mem_w = w_mem_kv.shape[2] // 2
    dec_lora = rwkv_w_up.shape[1]
    aaa_lora = rwkv_a_up.shape[1]
    assert dec_lora + aaa_lora == LANES and dec_lora == aaa_lora
    assert fox_heads <= SUBLANES and fox_w % LANES == 0 and rw_w % LANES == 0
    row2 = lambda v: v.reshape(1, -1)

    h = x.reshape(batch * seq, d)
    mem2 = mem.reshape(batch * mem_len, d)
    bd_ones = _block_diag_ones(rw_w, HEAD_DIM, 1.0)
    bd_avg = _block_diag_ones(rw_w, HEAD_DIM, 1.0 / HEAD_DIM)
    for l in range(depth):
        wi = w_in[l]
        o_ff = 3 * fox_w
        o_rw = o_ff + fox_heads
        o_mq = o_rw + rw_cols
        o_gt = o_mq + mem_w
        scale = HEAD_DIM ** -0.5 * LOG2E
        sections = [("qkv", jnp.concatenate([wi[:, :fox_w] * scale, wi[:, fox_w:o_ff]], axis=1)),
                    ("ff", jnp.pad(wi[:, o_ff:o_rw], ((0, 0), (0, LANES - fox_heads)))),
                    ("rwkv", wi[:, o_rw:o_mq]), ("memq", wi[:, o_mq:o_gt]), ("gate", wi[:, o_gt:])]
        w_all = jnp.concatenate([sec for _, sec in sections], axis=1).astype(BF16)
        cols, start = {}, 0
        for name, sec in sections:
            assert sec.shape[1] % LANES == 0
            cols[name] = (start, start + sec.shape[1])
            start += sec.shape[1]
        fbias = jnp.pad(row2(fox_f_bias[l]), ((0, 0), (0, LANES - fox_heads)))
        wlora = jnp.zeros((LANES, 2 * rw_w), F32)
        wlora = wlora.at[:dec_lora, :rw_w].set(rwkv_w_up[l]).at[dec_lora:, rw_w:].set(rwkv_a_up[l]).astype(BF16)

        qkv, crow, mq, gates, *scan_in = _inproj(
            h, row2(pre1_g[l]), w_all, cols, fbias, row2(rwkv_mu[l]), row2(rwkv_w0[l]), row2(rwkv_a0[l]),
            row2(rwkv_k_k[l]), row2(rwkv_k_a[l]), row2(rwkv_r_k[l]), wlora, rwkv_g_up[l].astype(BF16), bd_ones,
            seq=seq, mem_scale=(mem_w // MEM_HEADS) ** -0.5)

        fox_out = _fox(qkv, crow, batch=batch, seq=seq)

        rwkv_out = _rwkv_scan(*scan_in, row2(rwkv_gn_g[l]), row2(rwkv_gn_b[l]), bd_avg, batch=batch, seq=seq)

        h = _merge(h, fox_out, rwkv_out, mq, mem2, row2(mem_norm_g[l]), w_mem_kv[l].astype(BF16), gates,
                   w_fox_out[l].astype(BF16),
                   w_rwkv_out[l].astype(BF16), w_mem_out[l].astype(BF16), w_o[l].astype(BF16),
                   row2(post1_g[l]), seq=seq, mem_len=mem_len)
        h = _ffn(h, row2(pre2_g[l]), w_ffn_gate[l].astype(BF16), w_ffn_up[l].astype(BF16),
                 w_ffn_down[l].astype(BF16), row2(post2_g[l]))
    return h.reshape(batch, seq, d).astype(x.dtype)
```
